```python
import math
import jax, jax.numpy as jnp
from jax import lax
import numpy as np


D_MODEL = 4096
BATCH = 4
SEQ = 2048
DEPTH = 1

CHUNK = 64
Q_BLOCK = 2 * CHUNK
MIX_WIDTH = D_MODEL
RWKV_WIDTH = MIX_WIDTH // 2
RWKV_HEAD_DIM = 64
RWKV_HEADS = RWKV_WIDTH // RWKV_HEAD_DIM
DECAY_LORA = max(32, int(round(1.8 * math.sqrt(RWKV_WIDTH) / 32)) * 32)
AAA_LORA = DECAY_LORA
GATE_LORA = max(32, int(round(0.6 * RWKV_WIDTH ** 0.8 / 32)) * 32)
RWKV_PROJ = 3 * RWKV_WIDTH + DECAY_LORA + AAA_LORA + GATE_LORA
FOX_WIDTH = MIX_WIDTH - RWKV_WIDTH
FOX_HEAD_DIM = 128
FOX_HEADS = FOX_WIDTH // FOX_HEAD_DIM
FOX_PROJ = 3 * FOX_WIDTH + FOX_HEADS
IN_PROJ = RWKV_PROJ + FOX_PROJ
D_FF = ((8 * D_MODEL // 3 + 255) // 256) * 256
N_MOD = 9
NORM_EPS = 1e-6
LN_X_EPS = 64e-5

kernel_name = "hybrid_rwkv7_fox_macaron_adaln"


def rmsnorm(x, g, eps=NORM_EPS):
    xf = x.astype(jnp.float32)
    y = xf * lax.rsqrt(jnp.mean(xf * xf, axis=-1, keepdims=True) + eps)
    return (y * g.astype(jnp.float32)).astype(x.dtype)


def swiglu(h, w_gate, w_up, w_down):
    return (jax.nn.silu(h @ w_gate) * (h @ w_up)) @ w_down


def token_shift(p):
    return jnp.pad(p, ((0, 0), (1, 0), (0, 0)))[:, :-1]


def wkv7_scan(r, w, k, v, a, b):
    bsz, _, n_heads, n_dim = r.shape

    def step(state, inp):
        r_t, w_t, k_t, v_t, a_t, b_t = inp
        sa = jnp.einsum('bhvk,bhk->bhv', state, a_t)
        state = (state * w_t[:, :, None, :] + sa[..., None] * b_t[:, :, None, :]
                 + v_t[..., None] * k_t[:, :, None, :])
        return state, jnp.einsum('bhvk,bhk->bhv', state, r_t)

    xs = tuple(jnp.swapaxes(t, 0, 1) for t in (r, w, k, v, a, b))
    s0 = jnp.zeros((bsz, n_heads, n_dim, n_dim), jnp.float32)
    _, y = lax.scan(step, s0, xs)
    return jnp.swapaxes(y, 0, 1)


def rwkv7_group(p, mu, w0, w2, a0, a2, g2, k_k, k_a, r_k, ln_w, ln_b):
    bsz, seq, _ = p.shape
    H, N = RWKV_HEADS, RWKV_HEAD_DIM
    f32 = jnp.float32
    p = p + (token_shift(p) - p) * mu
    c1 = RWKV_WIDTH
    c2, c3 = 2 * c1, 3 * c1
    c4 = c3 + DECAY_LORA
    c5 = c4 + AAA_LORA
    r, k, v = p[..., :c1], p[..., c1:c2], p[..., c2:c3]
    wd, ad, gd = p[..., c3:c4], p[..., c4:c5], p[..., c5:]
    w = -jax.nn.softplus(-(w0 + jnp.tanh(wd) @ w2).astype(f32)) - 0.5
    decay = jnp.exp(-jnp.exp(w))
    a = jax.nn.sigmoid(a0 + ad @ a2)
    g = jax.nn.sigmoid(gd) @ g2
    kk = (k * k_k).astype(f32).reshape(bsz, seq, H, N)
    kk = kk / jnp.maximum(jnp.linalg.norm(kk, axis=-1, keepdims=True), 1e-12)
    k = k * (1 + (a - 1) * k_a)

    def heads(t):
        return t.astype(f32).reshape(bsz, seq, H, N)

    rh, kh, vh, ah = heads(r), heads(k), heads(v), heads(a)
    y = wkv7_scan(rh, heads(decay), kh, vh, -kk, kk * ah)
    mean = jnp.mean(y, axis=-1, keepdims=True)
    var = jnp.mean(jnp.square(y - mean), axis=-1, keepdims=True)
    y = ((y - mean) * lax.rsqrt(var + LN_X_EPS) * ln_w.astype(f32).reshape(H, N)
         + ln_b.astype(f32).reshape(H, N))
    y = y + jnp.sum(rh * kh * r_k.astype(f32), axis=-1, keepdims=True) * vh
    return y.reshape(bsz, seq, RWKV_WIDTH).astype(p.dtype) * g


def fox_group(p, b_f, out_norm):
    bsz, seq, _ = p.shape
    H, dh = FOX_HEADS, FOX_HEAD_DIM
    f32 = jnp.float32
    q = p[..., :FOX_WIDTH].reshape(bsz, seq, H, dh)
    k = p[..., FOX_WIDTH:2 * FOX_WIDTH].reshape(bsz, seq, H, dh)
    v = p[..., 2 * FOX_WIDTH:3 * FOX_WIDTH].reshape(bsz, seq, H, dh)
    logf = jax.nn.log_sigmoid((p[..., 3 * FOX_WIDTH:] + b_f).astype(f32))
    cum = jnp.swapaxes(jnp.cumsum(logf, axis=1), 1, 2)
    scale = FOX_HEAD_DIM ** -0.5
    outs = []
    for i in range(seq // Q_BLOCK):
        q0, q1 = i * Q_BLOCK, (i + 1) * Q_BLOCK
        logits = jnp.einsum('bqhd,bkhd->bhqk', q[:, q0:q1], k[:, :q1]).astype(f32) * scale
        logits = logits + cum[:, :, q0:q1, None] - cum[:, :, None, :q1]
        qpos = jnp.arange(q0, q1)
        kpos = jnp.arange(q1)
        logits = jnp.where(kpos[None, :] <= qpos[:, None], logits, -jnp.inf)
        probs = jax.nn.softmax(logits, axis=-1).astype(v.dtype)
        outs.append(jnp.einsum('bhqk,bkhd->bqhd', probs, v[:, :q1]))
    o = jnp.concatenate(outs, axis=1)
    o = rmsnorm(o, out_norm.reshape(H, dh))
    return o.reshape(bsz, seq, FOX_WIDTH)


def hybrid_mixer(h, w_in, rwkv_mu, rwkv_w0, rwkv_w2, rwkv_a0, rwkv_a2, rwkv_g2,
                 rwkv_k_k, rwkv_k_a, rwkv_r_k, rwkv_ln_w, rwkv_ln_b,
                 fox_b_f, fox_out_norm, w_out):
    p = h @ w_in
    y_r = rwkv7_group(p[..., :RWKV_PROJ], rwkv_mu, rwkv_w0, rwkv_w2, rwkv_a0, rwkv_a2,
                      rwkv_g2, rwkv_k_k, rwkv_k_a, rwkv_r_k, rwkv_ln_w, rwkv_ln_b)
    y_f = fox_group(p[..., RWKV_PROJ:], fox_b_f, fox_out_norm)
    return jnp.concatenate([y_r, y_f], axis=-1) @ w_out


def setup_inputs(seed: int = 0) -> dict:
    key = jax.random.key(seed)
    ks = jax.random.split(key, 32)
    f32 = jnp.float32
    L, D = DEPTH, D_MODEL

    def nrm(k, shape, scale):
        return jax.random.normal(k, shape, f32) * scale

    def gain(k, shape):
        return 1.0 + nrm(k, shape, 0.05)

    return {
        "x": nrm(ks[0], (BATCH, SEQ, D), 1.0),
        "c": nrm(ks[1], (BATCH, D), 1.0),
        "w_mod": nrm(ks[2], (L, D, N_MOD * D), D ** -0.5),
        "b_mod": nrm(ks[3], (L, N_MOD * D), 0.02),
        "norm_ffn1": gain(ks[4], (L, D)),
        "ffn1_gate": nrm(ks[5], (L, D, D_FF), D ** -0.5),
        "ffn1_up": nrm(ks[6], (L, D, D_FF), D ** -0.5),
        "ffn1_down": nrm(ks[7], (L, D_FF, D), D_FF ** -0.5),
        "norm_mix": gain(ks[8], (L, D)),
        "w_in": nrm(ks[9], (L, D, IN_PROJ), D ** -0.5),
        "rwkv_mu": jax.random.uniform(ks[10], (L, RWKV_PROJ), f32),
        "rwkv_w0": jax.random.uniform(ks[11], (L, RWKV_WIDTH), f32, -6.0, -1.0),
        "rwkv_w2": nrm(ks[12], (L, DECAY_LORA, RWKV_WIDTH), 0.1 * DECAY_LORA ** -0.5),
        "rwkv_a0": nrm(ks[13], (L, RWKV_WIDTH), 0.1),
        "rwkv_a2": nrm(ks[14], (L, AAA_LORA, RWKV_WIDTH), 0.5 * AAA_LORA ** -0.5),
        "rwkv_g2": nrm(ks[15], (L, GATE_LORA, RWKV_WIDTH), GATE_LORA ** -0.5),
        "rwkv_k_k": 0.85 + nrm(ks[16], (L, RWKV_WIDTH), 0.05),
        "rwkv_k_a": 1.0 + nrm(ks[17], (L, RWKV_WIDTH), 0.05),
        "rwkv_r_k": nrm(ks[18], (L, RWKV_HEADS, RWKV_HEAD_DIM), 0.1),
        "rwkv_ln_w": gain(ks[19], (L, RWKV_WIDTH)),
        "rwkv_ln_b": nrm(ks[20], (L, RWKV_WIDTH), 0.02),
        "fox_b_f": jax.random.uniform(ks[21], (L, FOX_HEADS), f32, 1.0, 4.0),
        "fox_out_norm": gain(ks[22], (L, FOX_WIDTH)),
        "w_out": nrm(ks[23], (L, MIX_WIDTH, D), MIX_WIDTH ** -0.5),
        "norm_ffn2": gain(ks[24], (L, D)),
        "ffn2_gate": nrm(ks[25], (L, D, D_FF), D ** -0.5),
        "ffn2_up": nrm(ks[26], (L, D, D_FF), D ** -0.5),
        "ffn2_down": nrm(ks[27], (L, D_FF, D), D_FF ** -0.5),
        "norm_final": gain(ks[28], (D,)),
    }


def reference(x, c, w_mod, b_mod, norm_ffn1, ffn1_gate, ffn1_up, ffn1_down,
              norm_mix, w_in, rwkv_mu, rwkv_w0, rwkv_w2, rwkv_a0, rwkv_a2, rwkv_g2,
              rwkv_k_k, rwkv_k_a, rwkv_r_k, rwkv_ln_w, rwkv_ln_b, fox_b_f,
              fox_out_norm, w_out, norm_ffn2, ffn2_gate, ffn2_up, ffn2_down,
              norm_final):
    for l in range(DEPTH):
        mod = (jax.nn.silu(c) @ w_mod[l] + b_mod[l])[:, None, :]
        sh1, sc1, gt1, sh2, sc2, gt2, sh3, sc3, gt3 = jnp.split(mod, N_MOD, axis=-1)
        h = rmsnorm(x, norm_ffn1[l]) * (1 + sc1) + sh1
        x = x + 0.5 * gt1 * swiglu(h, ffn1_gate[l], ffn1_up[l], ffn1_down[l])
        h = rmsnorm(x, norm_mix[l]) * (1 + sc2) + sh2
        x = x + gt2 * hybrid_mixer(h, w_in[l], rwkv_mu[l], rwkv_w0[l], rwkv_w2[l],
                                   rwkv_a0[l], rwkv_a2[l], rwkv_g2[l], rwkv_k_k[l],
                                   rwkv_k_a[l], rwkv_r_k[l], rwkv_ln_w[l], rwkv_ln_b[l],
                                   fox_b_f[l], fox_out_norm[l], w_out[l])
        h = rmsnorm(x, norm_ffn2[l]) * (1 + sc3) + sh3
        x = x + 0.5 * gt3 * swiglu(h, ffn2_gate[l], ffn2_up[l], ffn2_down[l])
    return rmsnorm(x, norm_final)
```

```python
import functools

import jax
import jax.numpy as jnp
from jax import lax
from jax.experimental import pallas as pl
from jax.experimental.pallas import tpu as pltpu

F32 = jnp.float32
BF16 = jnp.bfloat16

LANES = 128
RWKV_HEAD = 64
FOX_HEAD = 128
CHUNK = 64
INV_BASE = 4
NORM_EPS = 1e-6
LN_X_EPS = 64e-5
VMEM_LIMIT = 56 * 1024 * 1024


def _round_up(n, m):
    return (n + m - 1) // m * m


def _pick(n, candidates):
    for c in candidates:
        if n % c == 0:
            return c
    return n


def _dot(a, b):
    return jnp.dot(a, b, preferred_element_type=F32)


def _dot_nt(a, b):
    return lax.dot_general(a, b, (((1,), (1,)), ((), ())), preferred_element_type=F32)


def _dot_tn(a, b):
    return lax.dot_general(a, b, (((0,), (0,)), ((), ())), preferred_element_type=F32)


def _params(*sem):
    return pltpu.CompilerParams(dimension_semantics=sem, vmem_limit_bytes=VMEM_LIMIT)


def _mod_kernel(c_ref, w_ref, b_ref, o_ref):
    c = c_ref[...]
    a = (c * jax.nn.sigmoid(c)).astype(BF16)
    o_ref[...] = _dot(a, w_ref[...].astype(BF16)) + b_ref[...]


def _mod(c, w_mod, b_mod):
    bsz, d = c.shape
    n = w_mod.shape[1]
    rows = _round_up(bsz, 16)
    c_pad = jnp.pad(c, ((0, rows - bsz), (0, 0)))
    tn = _pick(n, (512, 256, 128))
    out = pl.pallas_call(
        _mod_kernel,
        grid=(n // tn,),
        in_specs=[pl.BlockSpec((rows, d), lambda j: (0, 0)),
                  pl.BlockSpec((d, tn), lambda j: (0, j)),
                  pl.BlockSpec((1, tn), lambda j: (0, j))],
        out_specs=pl.BlockSpec((rows, tn), lambda j: (0, j)),
        out_shape=jax.ShapeDtypeStruct((rows, n), F32),
        compiler_params=_params("parallel"),
        name="mod_matmul",
    )(c_pad, w_mod, b_mod.reshape(1, n))
    return out[:bsz]


def _norm_kernel(x_ref, g_ref, sc_ref, sh_ref, o_ref):
    x = x_ref[...]
    y = x * lax.rsqrt(jnp.mean(x * x, axis=-1, keepdims=True) + NORM_EPS) * g_ref[...]
    o_ref[...] = (y * (1.0 + sc_ref[0]) + sh_ref[0]).astype(o_ref.dtype)


def _final_norm_kernel(x_ref, g_ref, o_ref):
    x = x_ref[...]
    o_ref[...] = x * lax.rsqrt(jnp.mean(x * x, axis=-1, keepdims=True) + NORM_EPS) * g_ref[...]


def _norm_mod(x2, g, sc, sh, seq):
    t, d = x2.shape
    ts = _pick(seq, (256, 128, 64, 8))
    return pl.pallas_call(
        _norm_kernel,
        grid=(t // ts,),
        in_specs=[pl.BlockSpec((ts, d), lambda i: (i, 0)),
                  pl.BlockSpec((1, d), lambda i: (0, 0)),
                  pl.BlockSpec((1, 1, d), lambda i: (i * ts // seq, 0, 0)),
                  pl.BlockSpec((1, 1, d), lambda i: (i * ts // seq, 0, 0))],
        out_specs=pl.BlockSpec((ts, d), lambda i: (i, 0)),
        out_shape=jax.ShapeDtypeStruct((t, d), BF16),
        compiler_params=_params("parallel"),
        name="norm_mod",
    )(x2, g.reshape(1, d), sc, sh)


def _final_norm(x2, g):
    t, d = x2.shape
    ts = _pick(t, (256, 128, 64, 8))
    return pl.pallas_call(
        _final_norm_kernel,
        grid=(t // ts,),
        in_specs=[pl.BlockSpec((ts, d), lambda i: (i, 0)),
                  pl.BlockSpec((1, d), lambda i: (0, 0))],
        out_specs=pl.BlockSpec((ts, d), lambda i: (i, 0)),
        out_shape=jax.ShapeDtypeStruct((t, d), F32),
        compiler_params=_params("parallel"),
        name="final_norm",
    )(x2, g.reshape(1, d))


def _mm_kernel(x_ref, w_ref, o_ref, acc_ref, *, nk):
    k = pl.program_id(2)

    @pl.when(k == 0)
    def _():
        acc_ref[...] = jnp.zeros_like(acc_ref)

    acc_ref[...] += _dot(x_ref[...], w_ref[...])

    @pl.when(k == nk - 1)
    def _():
        o_ref[...] = acc_ref[...].astype(o_ref.dtype)


def _matmul(x, w, out_dtype):
    m, kd = x.shape
    n = w.shape[1]
    tm = _pick(m, (1024, 512, 256, 128))
    tn = _pick(n, (1024, 512, 640, 256, 128))
    tk = _pick(kd, (512, 256, 128))
    nk = kd // tk
    return pl.pallas_call(
        functools.partial(_mm_kernel, nk=nk),
        grid=(m // tm, n // tn, nk),
        in_specs=[pl.BlockSpec((tm, tk), lambda i, j, k: (i, k)),
                  pl.BlockSpec((tk, tn), lambda i, j, k: (k, j))],
        out_specs=pl.BlockSpec((tm, tn), lambda i, j, k: (i, j)),
        out_shape=jax.ShapeDtypeStruct((m, n), out_dtype),
        scratch_shapes=[pltpu.VMEM((tm, tn), F32)],
        compiler_params=_params("parallel", "parallel", "arbitrary"),
        name="matmul",
    )(x, w)


def _swiglu_kernel(x_ref, wg_ref, wu_ref, o_ref, accg_ref, accu_ref, *, nk):
    k = pl.program_id(2)

    @pl.when(k == 0)
    def _():
        accg_ref[...] = jnp.zeros_like(accg_ref)
        accu_ref[...] = jnp.zeros_like(accu_ref)

    x = x_ref[...]
    accg_ref[...] += _dot(x, wg_ref[...])
    accu_ref[...] += _dot(x, wu_ref[...])

    @pl.when(k == nk - 1)
    def _():
        g = accg_ref[...]
        o_ref[...] = (g * jax.nn.sigmoid(g) * accu_ref[...]).astype(o_ref.dtype)


def _swiglu_up(h, wg, wu):
    m, kd = h.shape
    n = wg.shape[1]
    tm = _pick(m, (1024, 512, 256, 128))
    tn = _pick(n, (1024, 512, 256, 128))
    tk = _pick(kd, (512, 256, 128))
    nk = kd // tk
    return pl.pallas_call(
        functools.partial(_swiglu_kernel, nk=nk),
        grid=(m // tm, n // tn, nk),
        in_specs=[pl.BlockSpec((tm, tk), lambda i, j, k: (i, k)),
                  pl.BlockSpec((tk, tn), lambda i, j, k: (k, j)),
                  pl.BlockSpec((tk, tn), lambda i, j, k: (k, j))],
        out_specs=pl.BlockSpec((tm, tn), lambda i, j, k: (i, j)),
        out_shape=jax.ShapeDtypeStruct((m, n), BF16),
        scratch_shapes=[pltpu.VMEM((tm, tn), F32), pltpu.VMEM((tm, tn), F32)],
        compiler_params=_params("parallel", "parallel", "arbitrary"),
        name="swiglu_up",
    )(h, wg, wu)


def _resid_kernel(a_ref, w_ref, x_ref, gt_ref, o_ref, acc_ref, *, nk, scale):
    k = pl.program_id(2)

    @pl.when(k == 0)
    def _():
        acc_ref[...] = jnp.zeros_like(acc_ref)

    acc_ref[...] += _dot(a_ref[...], w_ref[...])

    @pl.when(k == nk - 1)
    def _():
        o_ref[...] = x_ref[...] + (scale * gt_ref[0]) * acc_ref[...]


def _matmul_residual(a, w, x2, gate, seq, scale):
    m, kd = a.shape
    n = w.shape[1]
    tm = _pick(seq, (1024, 512, 256, 128))
    tn = _pick(n, (1024, 512, 256, 128))
    tk = _pick(kd, (1024, 512, 256, 128))
    nk = kd // tk
    return pl.pallas_call(
        functools.partial(_resid_kernel, nk=nk, scale=scale),
        grid=(m // tm, n // tn, nk),
        in_specs=[pl.BlockSpec((tm, tk), lambda i, j, k: (i, k)),
                  pl.BlockSpec((tk, tn), lambda i, j, k: (k, j)),
                  pl.BlockSpec((tm, tn), lambda i, j, k: (i, j)),
                  pl.BlockSpec((1, 1, tn), lambda i, j, k: (i * tm // seq, 0, j))],
        out_specs=pl.BlockSpec((tm, tn), lambda i, j, k: (i, j)),
        out_shape=jax.ShapeDtypeStruct((m, n), F32),
        scratch_shapes=[pltpu.VMEM((tm, tn), F32)],
        compiler_params=_params("parallel", "parallel", "arbitrary"),
        name="matmul_residual",
    )(a, w, x2, gate)


ROW_TILE = 256


def _lane_head0(shape):
    return lax.broadcasted_iota(jnp.int32, shape, 1) < RWKV_HEAD


def _stack(x):
    m0 = _lane_head0(x.shape)
    zero = jnp.zeros_like(x)
    return jnp.concatenate([jnp.where(m0, x, zero), jnp.where(m0, zero, x)], axis=0)


def _split_dot(lhs_bf16, x):
    hi = x.astype(BF16)
    lo = (x - hi.astype(F32)).astype(BF16)
    return _dot(lhs_bf16, hi) + _dot(lhs_bf16, lo)


def _rwkv_kernel(pr_ref, pk_ref, pv_ref, ps_ref, mur_ref, muk_ref, muv_ref, mus_ref,
                 w0_ref, a0_ref, kk_ref, ka_ref, rk_ref, lnw_ref, lnb_ref,
                 w2_ref, a2_ref, g2_ref, o_ref,
                 at_s, rt_s, kt_s, bt_s, kg_s, bg_s, v_s, g_s, bonus_s, gc_s,
                 m_s, n_s, q_s, yb_s, st_s, *, seq, lora_w, lora_a):
    n_tiles = seq // ROW_TILE
    n_chunks = seq // CHUNK
    cpt = ROW_TILE // CHUNK

    ri = lax.broadcasted_iota(jnp.int32, (LANES, LANES), 0)
    ci = lax.broadcasted_iota(jnp.int32, (LANES, LANES), 1)
    bd_mask = (ri // RWKV_HEAD) == (ci // RWKV_HEAD)
    bd_ones = bd_mask.astype(BF16)
    tr = lax.broadcasted_iota(jnp.int32, (ROW_TILE, ROW_TILE), 0)
    tc = lax.broadcasted_iota(jnp.int32, (ROW_TILE, ROW_TILE), 1)
    tri = ((tc <= tr) & ((tr // CHUNK) == (tc // CHUNK))).astype(BF16)

    def head_sum(x):
        return _split_dot_rhs(x, bd_ones)

    def shift_mix(ref, t0, mu):
        x = ref[0, pl.ds(t0, ROW_TILE), :]
        p0 = pl.multiple_of(jnp.maximum(t0 - 8, 0), 8)
        prev8 = ref[0, pl.ds(p0, 8), :]
        prev_row = jnp.where(t0 == 0, 0.0, prev8[7:8, :])
        row = lax.broadcasted_iota(jnp.int32, x.shape, 0)
        prev = jnp.where(row == 0, prev_row, pltpu.roll(x, 1, 0))
        return x + (prev - x) * mu

    def phase_a(i, carry):
        t0 = pl.multiple_of(i * ROW_TILE, ROW_TILE)
        r = shift_mix(pr_ref, t0, mur_ref[...])
        k = shift_mix(pk_ref, t0, muk_ref[...])
        v = shift_mix(pv_ref, t0, muv_ref[...])
        sm = shift_mix(ps_ref, t0, mus_ref[...])
        wd = sm[:, :lora_w]
        ad = sm[:, lora_w:lora_w + lora_a]
        gd = sm[:, lora_w + lora_a:]
        wlin = w0_ref[...] + _dot(jnp.tanh(wd).astype(BF16), w2_ref[...])
        wlog = -(jnp.maximum(-wlin, 0.0) + jnp.log1p(jnp.exp(-jnp.abs(wlin)))) - 0.5
        ld = -jnp.exp(wlog)
        alr = jax.nn.sigmoid(a0_ref[...] + _dot(ad.astype(BF16), a2_ref[...]))
        g = _dot(jax.nn.sigmoid(gd).astype(BF16), g2_ref[...])
        kk = k * kk_ref[...]
        nrm = jnp.sqrt(head_sum(kk * kk))
        kk = kk / jnp.maximum(nrm, 1e-12)
        k = k * (1.0 + (alr - 1.0) * ka_ref[...])
        bonus = head_sum(r * k * rk_ref[...]) * v
        b = kk * alr
        cum = _split_dot(tri, ld)
        g_dec = jnp.exp(cum)
        g_inv = jnp.exp(-cum)
        at_s[pl.ds(t0, ROW_TILE), :] = (-kk * jnp.exp(cum - ld)).astype(BF16)
        rt_s[pl.ds(t0, ROW_TILE), :] = (r * g_dec).astype(BF16)
        kt_s[pl.ds(t0, ROW_TILE), :] = (k * g_inv).astype(BF16)
        bt_s[pl.ds(t0, ROW_TILE), :] = (b * g_inv).astype(BF16)
        v_s[pl.ds(t0, ROW_TILE), :] = v.astype(BF16)
        g_s[pl.ds(t0, ROW_TILE), :] = g
        bonus_s[pl.ds(t0, ROW_TILE), :] = bonus
        for c in range(cpt):
            lo, hi = c * CHUNK, (c + 1) * CHUNK
            cum_c = cum[hi - 1:hi, :]
            rem = jnp.exp(cum_c - cum[lo:hi, :])
            kg_s[pl.ds(t0 + lo, CHUNK), :] = (k[lo:hi, :] * rem).astype(BF16)
            bg_s[pl.ds(t0 + lo, CHUNK), :] = (b[lo:hi, :] * rem).astype(BF16)
            gc_s[i * cpt + c] = jnp.broadcast_to(jnp.exp(cum_c), (8, LANES))
        return carry

    lax.fori_loop(0, n_tiles, phase_a, 0)

    row = lax.broadcasted_iota(jnp.int32, (CHUNK, LANES), 0)
    col = lax.broadcasted_iota(jnp.int32, (CHUNK, LANES), 1) % RWKV_HEAD
    strict = col < row
    incl = col <= row
    eye_p = (col == row).astype(F32)

    def same_block(size):
        return (row // size) == (col // size)

    def phase_b(c, carry):
        t0 = pl.multiple_of(c * CHUNK, CHUNK)
        sl = pl.ds(t0, CHUNK)
        at, rt, kt, bt = at_s[sl, :], rt_s[sl, :], kt_s[sl, :], bt_s[sl, :]
        kg, bg, v = kg_s[sl, :], bg_s[sl, :], v_s[sl, :]
        sc = _dot_nt(jnp.concatenate([at, rt], axis=0),
                     jnp.concatenate([_stack(kt), _stack(bt)], axis=0))
        a_ak = jnp.where(strict, sc[:CHUNK, :LANES], 0.0).astype(BF16)
        a_ab = jnp.where(strict, sc[:CHUNK, LANES:], 0.0)
        a_rk = jnp.where(incl, sc[CHUNK:, :LANES], 0.0).astype(BF16)
        a_rb = jnp.where(incl, sc[CHUNK:, LANES:], 0.0).astype(BF16)
        d = jnp.where(same_block(INV_BASE), a_ab, 0.0)
        db = d.astype(BF16)
        d2 = _dot(db, _stack(db)).astype(BF16)
        p = eye_p + d
        p = p + _dot(p.astype(BF16), _stack(d2))
        s = INV_BASE
        while s < CHUNK:
            lower_left = same_block(2 * s) & ((row // s) % 2 == 1) & ((col // s) % 2 == 0)
            a21 = jnp.where(lower_left, a_ab, 0.0).astype(BF16)
            pb = p.astype(BF16)
            a21_t11 = _dot(a21, _stack(pb)).astype(BF16)
            p = p + _dot(pb, _stack(a21_t11))
            s *= 2
        tinv = p.astype(BF16)
        vst = _stack(v)
        x1 = _dot(a_ak, vst).astype(BF16)
        wu = _dot(tinv, jnp.concatenate([_stack(at), _stack(x1)], axis=1))
        wb = wu[:, :LANES].astype(BF16)
        ub = wu[:, LANES:].astype(BF16)
        qy = _dot(a_rb, jnp.concatenate([_stack(wb), _stack(ub)], axis=1))
        q_s[sl, :] = (rt.astype(F32) + qy[:, :LANES]).astype(BF16)
        yb_s[sl, :] = _dot(a_rk, vst) + qy[:, LANES:]
        m_s[c] = jnp.where(bd_mask, _dot_tn(wb, bg), 0.0).astype(BF16)
        n_s[c] = jnp.where(bd_mask,
                           _dot_tn(jnp.concatenate([v, ub], axis=0),
                                   jnp.concatenate([kg, bg], axis=0)), 0.0)
        return carry

    lax.fori_loop(0, n_chunks, phase_b, 0, unroll=2)

    st_s[...] = jnp.zeros_like(st_s)
    inv_n = 1.0 / RWKV_HEAD

    def phase_c(c, carry):
        t0 = pl.multiple_of(c * CHUNK, CHUNK)
        sl = pl.ds(t0, CHUNK)
        s = st_s[...]
        sb = s.astype(BF16)
        y = _dot_nt(q_s[sl, :], sb) + yb_s[sl, :]
        st_s[...] = s * gc_s[c][0:1, :] + _dot(sb, m_s[c]) + n_s[c]
        mean = head_sum(y) * inv_n
        yc = y - mean
        var = head_sum(yc * yc) * inv_n
        yn = yc * lax.rsqrt(var + LN_X_EPS) * lnw_ref[...] + lnb_ref[...]
        o_ref[0, sl, :] = ((yn + bonus_s[sl, :]) * g_s[sl, :]).astype(o_ref.dtype)
        return carry

    lax.fori_loop(0, n_chunks, phase_c, 0)


def _split_dot_rhs(x, rhs_bf16):
    hi = x.astype(BF16)
    lo = (x - hi.astype(F32)).astype(BF16)
    return _dot(hi, rhs_bf16) + _dot(lo, rhs_bf16)


def _rwkv(p_rkv, p_sm, mu_rkv, mu_sm, w0, a0, k_k, k_a, r_k, ln_w, ln_b,
          w2p, a2p, g2p, bsz, seq):
    rw = w0.shape[-1]
    npair = rw // LANES
    lora_w, lora_a, lora_g = w2p.shape[0], a2p.shape[0], g2p.shape[0]
    wsm = lora_w + lora_a + lora_g
    p3 = p_rkv.reshape(bsz, seq, 3 * rw)
    ps = p_sm.reshape(bsz, seq, p_sm.shape[-1])
    n_chunks = seq // CHUNK

    def col(off):
        return pl.BlockSpec((1, seq, LANES), lambda b, h: (b, 0, off + h))

    def vec(off):
        return pl.BlockSpec((1, LANES), lambda b, h: (0, off + h))

    def v2(a):
        return a.reshape(1, rw)

    return pl.pallas_call(
        functools.partial(_rwkv_kernel, seq=seq, lora_w=lora_w, lora_a=lora_a),
        grid=(bsz, npair),
        in_specs=[col(0), col(npair), col(2 * npair),
                  pl.BlockSpec((1, seq, wsm), lambda b, h: (b, 0, 0)),
                  vec(0), vec(npair), vec(2 * npair),
                  pl.BlockSpec((1, wsm), lambda b, h: (0, 0)),
                  vec(0), vec(0), vec(0), vec(0), vec(0), vec(0), vec(0),
                  pl.BlockSpec((lora_w, LANES), lambda b, h: (0, h)),
                  pl.BlockSpec((lora_a, LANES), lambda b, h: (0, h)),
                  pl.BlockSpec((lora_g, LANES), lambda b, h: (0, h))],
        out_specs=pl.BlockSpec((1, seq, LANES), lambda b, h: (b, 0, h)),
        out_shape=jax.ShapeDtypeStruct((bsz, seq, rw), BF16),
        scratch_shapes=(
            [pltpu.VMEM((seq, LANES), BF16) for _ in range(7)]
            + [pltpu.VMEM((seq, LANES), F32) for _ in range(2)]
            + [pltpu.VMEM((n_chunks, 8, LANES), F32),
               pltpu.VMEM((n_chunks, LANES, LANES), BF16),
               pltpu.VMEM((n_chunks, LANES, LANES), F32),
               pltpu.VMEM((seq, LANES), BF16),
               pltpu.VMEM((seq, LANES), F32),
               pltpu.VMEM((LANES, LANES), F32)]),
        compiler_params=_params("parallel", "parallel"),
        name="rwkv7",
    )(p3, p3, p3, ps, mu_rkv, mu_rkv, mu_rkv, mu_sm,
      v2(w0), v2(a0), v2(k_k), v2(k_a), v2(r_k), v2(ln_w), v2(ln_b),
      w2p, a2p, g2p)


def _fgate_kernel(f_ref, bf_ref, col_ref, row_ref, *, seq):
    nblk = seq // LANES
    ri = lax.broadcasted_iota(jnp.int32, (LANES, LANES), 0)
    ci = lax.broadcasted_iota(jnp.int32, (LANES, LANES), 1)
    tri = (ci <= ri).astype(BF16)

    def body(i, carry):
        t0 = pl.multiple_of(i * LANES, LANES)
        z = f_ref[0, pl.ds(t0, LANES), :] + bf_ref[...]
        logf = jnp.minimum(z, 0.0) - jnp.log1p(jnp.exp(-jnp.abs(z)))
        hi = logf.astype(BF16)
        r1 = logf - hi.astype(F32)
        mid = r1.astype(BF16)
        lo = (r1 - mid.astype(F32)).astype(BF16)
        cum = carry + (_dot(tri, hi) + _dot(tri, mid) + _dot(tri, lo))
        col_ref[0, pl.ds(t0, LANES), :] = cum
        row_ref[0, :, pl.ds(t0, LANES)] = cum.T
        return cum[LANES - 1:LANES, :]

    lax.fori_loop(0, nblk, body, jnp.zeros((1, LANES), F32))


def _fgate(p_sm, b_f_pad, off, bsz, seq):
    ps = p_sm.reshape(bsz, seq, p_sm.shape[-1])
    return pl.pallas_call(
        functools.partial(_fgate_kernel, seq=seq),
        grid=(bsz,),
        in_specs=[pl.BlockSpec((1, seq, LANES), lambda b: (b, 0, off // LANES)),
                  pl.BlockSpec((1, LANES), lambda b: (0, 0))],
        out_specs=[pl.BlockSpec((1, seq, LANES), lambda b: (b, 0, 0)),
                   pl.BlockSpec((1, LANES, seq), lambda b: (b, 0, 0))],
        out_shape=[jax.ShapeDtypeStruct((bsz, seq, LANES), F32),
                   jax.ShapeDtypeStruct((bsz, LANES, seq), F32)],
        compiler_params=_params("parallel"),
        name="fox_gates",
    )(ps, b_f_pad)


def _fox_kernel(q_ref, k_ref, v_ref, ccol_ref, crow_ref, on_ref, o_ref, *, tq, scale):
    h = pl.program_id(1)
    i = pl.program_id(2)
    q = q_ref[0]
    lane = lax.broadcasted_iota(jnp.int32, (tq, LANES), 1)
    cq = jnp.sum(jnp.where(lane == h, ccol_ref[0], 0.0), axis=-1, keepdims=True)
    qpos = i * tq + lax.broadcasted_iota(jnp.int32, (tq, tq), 0)
    kio = lax.broadcasted_iota(jnp.int32, (tq, tq), 1)

    def body(j, carry):
        m, l, acc = carry
        k0 = pl.multiple_of(j * tq, tq)
        kb = k_ref[0, pl.ds(k0, tq), :]
        vb = v_ref[0, pl.ds(k0, tq), :]
        ck = crow_ref[0, 0, :, pl.ds(k0, tq)]
        s = _dot_nt(q, kb) * scale + cq - ck
        s = jnp.where(k0 + kio <= qpos, s, -jnp.inf)
        m_new = jnp.maximum(m, jnp.max(s, axis=-1, keepdims=True))
        alpha = jnp.exp(m - m_new)
        p = jnp.exp(s - m_new)
        l = alpha * l + jnp.sum(p, axis=-1, keepdims=True)
        acc = alpha * acc + _dot(p.astype(BF16), vb)
        return m_new, l, acc

    m0 = jnp.full((tq, 1), -jnp.inf, F32)
    l0 = jnp.zeros((tq, 1), F32)
    a0 = jnp.zeros((tq, LANES), F32)
    _, l, acc = lax.fori_loop(0, i + 1, body, (m0, l0, a0))
    o = acc / l
    o = o * lax.rsqrt(jnp.mean(o * o, axis=-1, keepdims=True) + NORM_EPS) * on_ref[0]
    o_ref[0] = o.astype(o_ref.dtype)


def _fox(p_fox, ccol, crow, out_norm, bsz, seq):
    fw = p_fox.shape[-1] // 3
    nh = fw // FOX_HEAD
    tq = _pick(seq, (256, 128))
    p3 = p_fox.reshape(bsz, seq, 3 * fw)
    crow4 = crow.reshape(bsz, LANES, 1, seq)
    on = out_norm.reshape(nh, 1, FOX_HEAD)
    return pl.pallas_call(
        functools.partial(_fox_kernel, tq=tq, scale=FOX_HEAD ** -0.5),
        grid=(bsz, nh, seq // tq),
        in_specs=[pl.BlockSpec((1, tq, FOX_HEAD), lambda b, h, i: (b, i, h)),
                  pl.BlockSpec((1, seq, FOX_HEAD), lambda b, h, i: (b, 0, nh + h)),
                  pl.BlockSpec((1, seq, FOX_HEAD), lambda b, h, i: (b, 0, 2 * nh + h)),
                  pl.BlockSpec((1, tq, LANES), lambda b, h, i: (b, i, 0)),
                  pl.BlockSpec((1, 1, 1, seq), lambda b, h, i: (b, h, 0, 0)),
                  pl.BlockSpec((1, 1, FOX_HEAD), lambda b, h, i: (h, 0, 0))],
        out_specs=pl.BlockSpec((1, tq, FOX_HEAD), lambda b, h, i: (b, i, h)),
        out_shape=jax.ShapeDtypeStruct((bsz, seq, fw), BF16),
        compiler_params=_params("parallel", "parallel", "parallel"),
        name="fox_attention",
    )(p3, p3, p3, ccol, crow4, on)


def _pad_rows(w, rows):
    return jnp.pad(w, ((0, rows - w.shape[0]), (0, 0)))


def _pad_cols(w, cols):
    return jnp.pad(w, ((0, 0), (0, cols - w.shape[1])))


def _ffn(x2, h, w_gate, w_up, w_down, gate, seq):
    dff = w_gate.shape[1]
    dffp = _round_up(dff, 1024)
    wg = _pad_cols(w_gate, dffp).astype(BF16)
    wu = _pad_cols(w_up, dffp).astype(BF16)
    wd = _pad_rows(w_down, dffp).astype(BF16)
    act = _swiglu_up(h, wg, wu)
    return _matmul_residual(act, wd, x2, gate, seq, 0.5)


def kernel(x, c, w_mod, b_mod, norm_ffn1, ffn1_gate, ffn1_up, ffn1_down, norm_mix, w_in, rwkv_mu, rwkv_w0, rwkv_w2, rwkv_a0, rwkv_a2, rwkv_g2, rwkv_k_k, rwkv_k_a, rwkv_r_k, rwkv_ln_w, rwkv_ln_b, fox_b_f, fox_out_norm, w_out, norm_ffn2, ffn2_gate, ffn2_up, ffn2_down, norm_final):
    bsz, seq, d = x.shape
    depth = w_mod.shape[0]
    rw = rwkv_w0.shape[-1]
    fw = fox_out_norm.shape[-1]
    nfh = fox_b_f.shape[-1]
    dl, al, gl = rwkv_w2.shape[1], rwkv_a2.shape[1], rwkv_g2.shape[1]
    dlp, alp, glp = (_round_up(n, LANES) for n in (dl, al, gl))
    c3 = 3 * rw
    c4, c5, c6 = c3 + dl, c3 + dl + al, c3 + dl + al + gl

    x2 = x.reshape(bsz * seq, d)
    for l in range(depth):
        mod = _mod(c, w_mod[l], b_mod[l])
        sh1, sc1, gt1, sh2, sc2, gt2, sh3, sc3, gt3 = (
            m[:, None, :] for m in jnp.split(mod, 9, axis=-1))

        h = _norm_mod(x2, norm_ffn1[l], sc1, sh1, seq)
        x2 = _ffn(x2, h, ffn1_gate[l], ffn1_up[l], ffn1_down[l], gt1, seq)

        h = _norm_mod(x2, norm_mix[l], sc2, sh2, seq)
        wi, mu = w_in[l], rwkv_mu[l]
        w_rkv = wi[:, :c3].astype(BF16)
        w_sm = jnp.concatenate(
            [_pad_cols(wi[:, c3:c4], dlp), _pad_cols(wi[:, c4:c5], alp),
             _pad_cols(wi[:, c5:c6], glp), _pad_cols(wi[:, c6 + 3 * fw:], LANES)],
            axis=1).astype(BF16)
        w_fox = wi[:, c6:c6 + 3 * fw].astype(BF16)
        mu_rkv = mu[:c3].reshape(1, c3)
        mu_sm = jnp.concatenate(
            [jnp.pad(mu[c3:c4], (0, dlp - dl)), jnp.pad(mu[c4:c5], (0, alp - al)),
             jnp.pad(mu[c5:c6], (0, glp - gl))]).reshape(1, dlp + alp + glp)
        p_rkv = _matmul(h, w_rkv, F32)
        p_sm = _matmul(h, w_sm, F32)
        p_fox = _matmul(h, w_fox, BF16)

        y_r = _rwkv(p_rkv, p_sm[:, :dlp + alp + glp], mu_rkv, mu_sm,
                    rwkv_w0[l], rwkv_a0[l], rwkv_k_k[l], rwkv_k_a[l], rwkv_r_k[l],
                    rwkv_ln_w[l], rwkv_ln_b[l],
                    _pad_rows(rwkv_w2[l], dlp).astype(BF16),
                    _pad_rows(rwkv_a2[l], alp).astype(BF16),
                    _pad_rows(rwkv_g2[l], glp).astype(BF16), bsz, seq)

        b_f_pad = jnp.pad(fox_b_f[l], (0, LANES - nfh)).reshape(1, LANES)
        ccol, crow = _fgate(p_sm, b_f_pad, dlp + alp + glp, bsz, seq)
        y_f = _fox(p_fox, ccol, crow, fox_out_norm[l], bsz, seq)

        y = jnp.concatenate([y_r, y_f], axis=-1).reshape(bsz * seq, rw + fw)
        x2 = _matmul_residual(y, w_out[l].astype(BF16), x2, gt2, seq, 1.0)

        h = _norm_mod(x2, norm_ffn2[l], sc3, sh3, seq)
        x2 = _ffn(x2, h, ffn2_gate[l], ffn2_up[l], ffn2_down[l], gt3, seq)
    return _final_norm(x2, norm_final).reshape(bsz, seq, d)
```

```python
import functools

import jax
import jax.numpy as jnp
from jax import lax
from jax.experimental import pallas as pl
from jax.experimental.pallas import tpu as pltpu

F32 = jnp.float32
BF16 = jnp.bfloat16

LANES = 128
RWKV_HEAD = 64
FOX_HEAD = 128
CHUNK = 64
INV_BASE = 4
GROUP = 8
NORM_EPS = 1e-6
LN_X_EPS = 64e-5
VMEM_LIMIT = 56 * 1024 * 1024


def _round_up(n, m):
    return (n + m - 1) // m * m


def _pick(n, candidates):
    for c in candidates:
        if n % c == 0:
            return c
    return n


def _dot(a, b):
    return jnp.dot(a, b, preferred_element_type=F32)


def _dot_nt(a, b):
    return lax.dot_general(a, b, (((1,), (1,)), ((), ())), preferred_element_type=F32)


def _dot_tn(a, b):
    return lax.dot_general(a, b, (((0,), (0,)), ((), ())), preferred_element_type=F32)


def _params(*sem):
    return pltpu.CompilerParams(dimension_semantics=sem, vmem_limit_bytes=VMEM_LIMIT)


def _mod_kernel(c_ref, w_ref, b_ref, o_ref):
    c = c_ref[...]
    a = (c * jax.nn.sigmoid(c)).astype(BF16)
    o_ref[...] = _dot(a, w_ref[...].astype(BF16)) + b_ref[...]


def _mod(c, w_mod, b_mod):
    bsz, d = c.shape
    n = w_mod.shape[1]
    rows = _round_up(bsz, 16)
    c_pad = jnp.pad(c, ((0, rows - bsz), (0, 0)))
    tn = _pick(n, (512, 256, 128))
    out = pl.pallas_call(
        _mod_kernel,
        grid=(n // tn,),
        in_specs=[pl.BlockSpec((rows, d), lambda j: (0, 0)),
                  pl.BlockSpec((d, tn), lambda j: (0, j)),
                  pl.BlockSpec((1, tn), lambda j: (0, j))],
        out_specs=pl.BlockSpec((rows, tn), lambda j: (0, j)),
        out_shape=jax.ShapeDtypeStruct((rows, n), F32),
        compiler_params=_params("parallel"),
        name="mod_matmul",
    )(c_pad, w_mod, b_mod.reshape(1, n))
    return out[:bsz]


def _norm_kernel(x_ref, g_ref, sc_ref, sh_ref, o_ref):
    x = x_ref[...]
    y = x * lax.rsqrt(jnp.mean(x * x, axis=-1, keepdims=True) + NORM_EPS) * g_ref[...]
    o_ref[...] = (y * (1.0 + sc_ref[0]) + sh_ref[0]).astype(o_ref.dtype)


def _final_norm_kernel(x_ref, g_ref, o_ref):
    x = x_ref[...]
    o_ref[...] = x * lax.rsqrt(jnp.mean(x * x, axis=-1, keepdims=True) + NORM_EPS) * g_ref[...]


def _norm_mod(x2, g, sc, sh, seq):
    t, d = x2.shape
    ts = _pick(seq, (256, 128, 64, 8))
    return pl.pallas_call(
        _norm_kernel,
        grid=(t // ts,),
        in_specs=[pl.BlockSpec((ts, d), lambda i: (i, 0)),
                  pl.BlockSpec((1, d), lambda i: (0, 0)),
                  pl.BlockSpec((1, 1, d), lambda i: (i * ts // seq, 0, 0)),
                  pl.BlockSpec((1, 1, d), lambda i: (i * ts // seq, 0, 0))],
        out_specs=pl.BlockSpec((ts, d), lambda i: (i, 0)),
        out_shape=jax.ShapeDtypeStruct((t, d), BF16),
        compiler_params=_params("parallel"),
        name="norm_mod",
    )(x2, g.reshape(1, d), sc, sh)


def _final_norm(x2, g):
    t, d = x2.shape
    ts = _pick(t, (256, 128, 64, 8))
    return pl.pallas_call(
        _final_norm_kernel,
        grid=(t // ts,),
        in_specs=[pl.BlockSpec((ts, d), lambda i: (i, 0)),
                  pl.BlockSpec((1, d), lambda i: (0, 0))],
        out_specs=pl.BlockSpec((ts, d), lambda i: (i, 0)),
        out_shape=jax.ShapeDtypeStruct((t, d), F32),
        compiler_params=_params("parallel"),
        name="final_norm",
    )(x2, g.reshape(1, d))


def _mm_kernel(x_ref, w_ref, o_ref):
    o_ref[...] = _dot(x_ref[...], w_ref[...]).astype(o_ref.dtype)


def _matmul(x, w, out_dtype):
    m, kd = x.shape
    n = w.shape[1]
    tm = _pick(m, (1024, 512, 256, 128))
    tn = _pick(n, (512, 640, 256, 128))
    return pl.pallas_call(
        _mm_kernel,
        grid=(m // tm, n // tn),
        in_specs=[pl.BlockSpec((tm, kd), lambda i, j: (i, 0)),
                  pl.BlockSpec((kd, tn), lambda i, j: (0, j))],
        out_specs=pl.BlockSpec((tm, tn), lambda i, j: (i, j)),
        out_shape=jax.ShapeDtypeStruct((m, n), out_dtype),
        compiler_params=_params("parallel", "parallel"),
        name="matmul",
    )(x, w)


def _swiglu_kernel(x_ref, wg_ref, wu_ref, o_ref):
    x = x_ref[...]
    g = _dot(x, wg_ref[...])
    u = _dot(x, wu_ref[...])
    o_ref[...] = (g * (0.5 * jnp.tanh(0.5 * g) + 0.5) * u).astype(o_ref.dtype)


def _swiglu_up(h, wg, wu):
    m, kd = h.shape
    n = wg.shape[1]
    tm = _pick(m, (1024, 512, 256, 128))
    tn = _pick(n, (512, 256, 128))
    return pl.pallas_call(
        _swiglu_kernel,
        grid=(m // tm, n // tn),
        in_specs=[pl.BlockSpec((tm, kd), lambda i, j: (i, 0)),
                  pl.BlockSpec((kd, tn), lambda i, j: (0, j)),
                  pl.BlockSpec((kd, tn), lambda i, j: (0, j))],
        out_specs=pl.BlockSpec((tm, tn), lambda i, j: (i, j)),
        out_shape=jax.ShapeDtypeStruct((m, n), BF16),
        compiler_params=_params("parallel", "parallel"),
        name="swiglu_up",
    )(h, wg, wu)


def _resid_kernel(a_ref, w_ref, x_ref, gt_ref, o_ref, acc_ref, *, nk, scale):
    k = pl.program_id(2)

    @pl.when(k == 0)
    def _():
        acc_ref[...] = _dot(a_ref[...], w_ref[...])

    @pl.when((k > 0) & (k < nk - 1))
    def _():
        acc_ref[...] += _dot(a_ref[...], w_ref[...])

    @pl.when(k == nk - 1)
    def _():
        o_ref[...] = x_ref[...] + (scale * gt_ref[0]) * (acc_ref[...] + _dot(a_ref[...], w_ref[...]))


def _matmul_residual(a, w, x2, gate, seq, scale):
    m, kd = a.shape
    n = w.shape[1]
    tm = _pick(seq, (1024, 512, 256, 128))
    tn = _pick(n, (1024, 512, 256, 128))
    tk = _pick(kd, (2816, 1024, 512, 256, 128))
    nk = kd // tk
    assert nk >= 2
    return pl.pallas_call(
        functools.partial(_resid_kernel, nk=nk, scale=scale),
        grid=(m // tm, n // tn, nk),
        in_specs=[pl.BlockSpec((tm, tk), lambda i, j, k: (i, k)),
                  pl.BlockSpec((tk, tn), lambda i, j, k: (k, j)),
                  pl.BlockSpec((tm, tn), lambda i, j, k: (i, j)),
                  pl.BlockSpec((1, 1, tn), lambda i, j, k: (i * tm // seq, 0, j))],
        out_specs=pl.BlockSpec((tm, tn), lambda i, j, k: (i, j)),
        out_shape=jax.ShapeDtypeStruct((m, n), F32),
        scratch_shapes=[pltpu.VMEM((tm, tn), F32)],
        compiler_params=_params("parallel", "parallel", "arbitrary"),
        name="matmul_residual",
    )(a, w, x2, gate)


def _resid2_kernel(a1_ref, a2_ref, w1_ref, w2_ref, x_ref, gt_ref, o_ref, *, scale):
    y = _dot(a1_ref[...], w1_ref[...]) + _dot(a2_ref[...], w2_ref[...])
    o_ref[...] = x_ref[...] + (scale * gt_ref[0]) * y


def _matmul_residual2(a1, a2, w, x2, gate, seq, scale):
    m, k1 = a1.shape
    k2 = a2.shape[1]
    n = w.shape[1]
    assert k1 == k2
    tm = _pick(seq, (1024, 512, 256, 128))
    tn = _pick(n, (512, 256, 128))
    return pl.pallas_call(
        functools.partial(_resid2_kernel, scale=scale),
        grid=(m // tm, n // tn),
        in_specs=[pl.BlockSpec((tm, k1), lambda i, j: (i, 0)),
                  pl.BlockSpec((tm, k2), lambda i, j: (i, 0)),
                  pl.BlockSpec((k1, tn), lambda i, j: (0, j)),
                  pl.BlockSpec((k2, tn), lambda i, j: (1, j)),
                  pl.BlockSpec((tm, tn), lambda i, j: (i, j)),
                  pl.BlockSpec((1, 1, tn), lambda i, j: (i * tm // seq, 0, j))],
        out_specs=pl.BlockSpec((tm, tn), lambda i, j: (i, j)),
        out_shape=jax.ShapeDtypeStruct((m, n), F32),
        compiler_params=_params("parallel", "parallel"),
        name="out_proj_residual",
    )(a1, a2, w, w, x2, gate)


ROW_TILE = 256


def _lane_head0(shape):
    return lax.broadcasted_iota(jnp.int32, shape, 1) < RWKV_HEAD


def _stack(x):
    m0 = _lane_head0(x.shape)
    zero = jnp.zeros_like(x)
    return jnp.concatenate([jnp.where(m0, x, zero), jnp.where(m0, zero, x)], axis=0)


def _split_dot(lhs_bf16, x):
    hi = x.astype(BF16)
    lo = (x - hi.astype(F32)).astype(BF16)
    return _dot(lhs_bf16, hi) + _dot(lhs_bf16, lo)


def _rwkv_kernel(pr_ref, pk_ref, pv_ref, ps_ref, mur_ref, muk_ref, muv_ref, mus_ref,
                 w0_ref, a0_ref, kk_ref, ka_ref, rk_ref, lnw_ref, lnb_ref,
                 w2_ref, a2_ref, g2_ref, o_ref,
                 at_s, rt_s, kt_s, bt_s, kg_s, bg_s, v_s, g_s, bonus_s, gc_s,
                 m_s, n_s, q_s, yb_s, sb_s, *, seq, lora_w, lora_a):
    n_tiles = seq // ROW_TILE
    n_chunks = seq // CHUNK
    cpt = ROW_TILE // CHUNK

    ri = lax.broadcasted_iota(jnp.int32, (LANES, LANES), 0)
    ci = lax.broadcasted_iota(jnp.int32, (LANES, LANES), 1)
    bd_mask = (ri // RWKV_HEAD) == (ci // RWKV_HEAD)
    bd_ones = bd_mask.astype(BF16)
    tr = lax.broadcasted_iota(jnp.int32, (ROW_TILE, ROW_TILE), 0)
    tc = lax.broadcasted_iota(jnp.int32, (ROW_TILE, ROW_TILE), 1)
    tri = ((tc <= tr) & ((tr // CHUNK) == (tc // CHUNK))).astype(BF16)

    def head_sum(x):
        return _split_dot_rhs(x, bd_ones)

    def shift_mix(ref, t0, mu):
        x = ref[0, pl.ds(t0, ROW_TILE), :]
        p0 = pl.multiple_of(jnp.maximum(t0 - 8, 0), 8)
        prev8 = ref[0, pl.ds(p0, 8), :]
        prev_row = jnp.where(t0 == 0, 0.0, prev8[7:8, :])
        row = lax.broadcasted_iota(jnp.int32, x.shape, 0)
        prev = jnp.where(row == 0, prev_row, pltpu.roll(x, 1, 0))
        return x + (prev - x) * mu

    def phase_a(i, carry):
        t0 = pl.multiple_of(i * ROW_TILE, ROW_TILE)
        r = shift_mix(pr_ref, t0, mur_ref[...])
        k = shift_mix(pk_ref, t0, muk_ref[...])
        v = shift_mix(pv_ref, t0, muv_ref[...])
        sm = shift_mix(ps_ref, t0, mus_ref[...])
        wd = sm[:, :lora_w]
        ad = sm[:, lora_w:lora_w + lora_a]
        gd = sm[:, lora_w + lora_a:]
        wlin = w0_ref[...] + _dot(jnp.tanh(wd).astype(BF16), w2_ref[...])
        wlog = -(jnp.maximum(-wlin, 0.0) + jnp.log1p(jnp.exp(-jnp.abs(wlin)))) - 0.5
        ld = -jnp.exp(wlog)
        alr = jax.nn.sigmoid(a0_ref[...] + _dot(ad.astype(BF16), a2_ref[...]))
        g = _dot(jax.nn.sigmoid(gd).astype(BF16), g2_ref[...])
        kk = k * kk_ref[...]
        nrm = jnp.sqrt(head_sum(kk * kk))
        kk = kk / jnp.maximum(nrm, 1e-12)
        k = k * (1.0 + (alr - 1.0) * ka_ref[...])
        bonus = head_sum(r * k * rk_ref[...]) * v
        b = kk * alr
        cum = _split_dot(tri, ld)
        g_dec = jnp.exp(cum)
        g_inv = jnp.exp(-cum)
        at_s[pl.ds(t0, ROW_TILE), :] = (-kk * jnp.exp(cum - ld)).astype(BF16)
        rt_s[pl.ds(t0, ROW_TILE), :] = (r * g_dec).astype(BF16)
        kt_s[pl.ds(t0, ROW_TILE), :] = (k * g_inv).astype(BF16)
        bt_s[pl.ds(t0, ROW_TILE), :] = (b * g_inv).astype(BF16)
        v_s[pl.ds(t0, ROW_TILE), :] = v.astype(BF16)
        g_s[pl.ds(t0, ROW_TILE), :] = g
        bonus_s[pl.ds(t0, ROW_TILE), :] = bonus
        for c in range(cpt):
            lo, hi = c * CHUNK, (c + 1) * CHUNK
            cum_c = cum[hi - 1:hi, :]
            rem = jnp.exp(cum_c - cum[lo:hi, :])
            kg_s[pl.ds(t0 + lo, CHUNK), :] = (k[lo:hi, :] * rem).astype(BF16)
            bg_s[pl.ds(t0 + lo, CHUNK), :] = (b[lo:hi, :] * rem).astype(BF16)
            gc_s[i * cpt + c] = jnp.broadcast_to(jnp.exp(cum_c), (8, LANES))
        return carry

    lax.fori_loop(0, n_tiles, phase_a, 0)

    row = lax.broadcasted_iota(jnp.int32, (CHUNK, LANES), 0)
    col = lax.broadcasted_iota(jnp.int32, (CHUNK, LANES), 1) % RWKV_HEAD
    strict = col < row
    incl = col <= row
    eye_p = (col == row).astype(F32)

    def same_block(size):
        return (row // size) == (col // size)

    def each(fn, *lists):
        return [fn(*xs) for xs in zip(*lists)]

    def phase_b(i, carry):
        cs = [i * GROUP + g for g in range(GROUP)]
        sls = [pl.ds(pl.multiple_of(c * CHUNK, CHUNK), CHUNK) for c in cs]
        at = [at_s[sl, :] for sl in sls]
        rt = [rt_s[sl, :] for sl in sls]
        kt = [kt_s[sl, :] for sl in sls]
        bt = [bt_s[sl, :] for sl in sls]
        v = [v_s[sl, :] for sl in sls]
        sc = each(lambda a, r, k, b: _dot_nt(jnp.concatenate([a, r], axis=0),
                                             jnp.concatenate([_stack(k), _stack(b)], axis=0)),
                  at, rt, kt, bt)
        a_ak = [jnp.where(strict, x[:CHUNK, :LANES], 0.0).astype(BF16) for x in sc]
        a_ab = [jnp.where(strict, x[:CHUNK, LANES:], 0.0) for x in sc]
        a_rk = [jnp.where(incl, x[CHUNK:, :LANES], 0.0).astype(BF16) for x in sc]
        a_rb = [jnp.where(incl, x[CHUNK:, LANES:], 0.0).astype(BF16) for x in sc]
        d = [jnp.where(same_block(INV_BASE), x, 0.0) for x in a_ab]
        db = [x.astype(BF16) for x in d]
        d2 = [_dot(x, _stack(x)).astype(BF16) for x in db]
        p = [eye_p + x for x in d]
        p = each(lambda pp, x2: pp + _dot(pp.astype(BF16), _stack(x2)), p, d2)
        s = INV_BASE
        while s < CHUNK:
            lower_left = same_block(2 * s) & ((row // s) % 2 == 1) & ((col // s) % 2 == 0)
            a21 = [jnp.where(lower_left, x, 0.0).astype(BF16) for x in a_ab]
            pb = [x.astype(BF16) for x in p]
            a21_t11 = each(lambda a, t: _dot(a, _stack(t)).astype(BF16), a21, pb)
            p = each(lambda pp, t, y: pp + _dot(t, _stack(y)), p, pb, a21_t11)
            s *= 2
        tinv = [x.astype(BF16) for x in p]
        vst = [_stack(x) for x in v]
        x1 = each(lambda a, vs: _dot(a, vs).astype(BF16), a_ak, vst)
        wu = each(lambda t, a, x: _dot(t, jnp.concatenate([_stack(a), _stack(x)], axis=1)),
                  tinv, at, x1)
        wb = [x[:, :LANES].astype(BF16) for x in wu]
        ub = [x[:, LANES:].astype(BF16) for x in wu]
        qy = each(lambda a, w, u: _dot(a, jnp.concatenate([_stack(w), _stack(u)], axis=1)),
                  a_rb, wb, ub)
        yb = each(lambda a, vs, x: _dot(a, vs) + x[:, LANES:], a_rk, vst, qy)
        for g in range(GROUP):
            q_s[sls[g], :] = (rt[g].astype(F32) + qy[g][:, :LANES]).astype(BF16)
            yb_s[sls[g], :] = yb[g]
        kg = [kg_s[sl, :] for sl in sls]
        bg = [bg_s[sl, :] for sl in sls]
        mm = each(_dot_tn, wb, bg)
        nn = each(lambda vv, u, k, b: _dot_tn(jnp.concatenate([vv, u], axis=0),
                                              jnp.concatenate([k, b], axis=0)), v, ub, kg, bg)
        for g in range(GROUP):
            m_s[cs[g]] = jnp.where(bd_mask, mm[g], 0.0).astype(BF16)
            n_s[cs[g]] = jnp.where(bd_mask, nn[g], 0.0)
        return carry

    lax.fori_loop(0, n_chunks // GROUP, phase_b, 0)

    def phase_c1(c, s):
        sb = s.astype(BF16)
        sb_s[c] = sb
        return s * gc_s[c][0:1, :] + _dot(sb, m_s[c]) + n_s[c]

    lax.fori_loop(0, n_chunks, phase_c1, jnp.zeros((LANES, LANES), F32))

    inv_n = 1.0 / RWKV_HEAD

    def phase_c2(i, carry):
        cs = [i * GROUP + g for g in range(GROUP)]
        sls = [pl.ds(pl.multiple_of(c * CHUNK, CHUNK), CHUNK) for c in cs]
        y = [_dot_nt(q_s[sl, :], sb_s[c]) + yb_s[sl, :] for sl, c in zip(sls, cs)]
        mean = [head_sum(x) * inv_n for x in y]
        yc = each(lambda a, b: a - b, y, mean)
        var = [head_sum(x * x) * inv_n for x in yc]
        for g in range(GROUP):
            yn = yc[g] * lax.rsqrt(var[g] + LN_X_EPS) * lnw_ref[...] + lnb_ref[...]
            o_ref[0, sls[g], :] = ((yn + bonus_s[sls[g], :]) * g_s[sls[g], :]).astype(o_ref.dtype)
        return carry

    lax.fori_loop(0, n_chunks // GROUP, phase_c2, 0)


def _split_dot_rhs(x, rhs_bf16):
    hi = x.astype(BF16)
    lo = (x - hi.astype(F32)).astype(BF16)
    return _dot(hi, rhs_bf16) + _dot(lo, rhs_bf16)


def _rwkv(p_rkv, p_sm, mu_rkv, mu_sm, w0, a0, k_k, k_a, r_k, ln_w, ln_b,
          w2p, a2p, g2p, bsz, seq):
    rw = w0.shape[-1]
    npair = rw // LANES
    lora_w, lora_a, lora_g = w2p.shape[0], a2p.shape[0], g2p.shape[0]
    wsm = lora_w + lora_a + lora_g
    p3 = p_rkv.reshape(bsz, seq, 3 * rw)
    ps = p_sm.reshape(bsz, seq, p_sm.shape[-1])
    n_chunks = seq // CHUNK

    def col(off):
        return pl.BlockSpec((1, seq, LANES), lambda b, h: (b, 0, off + h))

    def vec(off):
        return pl.BlockSpec((1, LANES), lambda b, h: (0, off + h))

    def v2(a):
        return a.reshape(1, rw)

    return pl.pallas_call(
        functools.partial(_rwkv_kernel, seq=seq, lora_w=lora_w, lora_a=lora_a),
        grid=(bsz, npair),
        in_specs=[col(0), col(npair), col(2 * npair),
                  pl.BlockSpec((1, seq, wsm), lambda b, h: (b, 0, 0)),
                  vec(0), vec(npair), vec(2 * npair),
                  pl.BlockSpec((1, wsm), lambda b, h: (0, 0)),
                  vec(0), vec(0), vec(0), vec(0), vec(0), vec(0), vec(0),
                  pl.BlockSpec((lora_w, LANES), lambda b, h: (0, h)),
                  pl.BlockSpec((lora_a, LANES), lambda b, h: (0, h)),
                  pl.BlockSpec((lora_g, LANES), lambda b, h: (0, h))],
        out_specs=pl.BlockSpec((1, seq, LANES), lambda b, h: (b, 0, h)),
        out_shape=jax.ShapeDtypeStruct((bsz, seq, rw), BF16),
        scratch_shapes=(
            [pltpu.VMEM((seq, LANES), BF16) for _ in range(7)]
            + [pltpu.VMEM((seq, LANES), F32) for _ in range(2)]
            + [pltpu.VMEM((n_chunks, 8, LANES), F32),
               pltpu.VMEM((n_chunks, LANES, LANES), BF16),
               pltpu.VMEM((n_chunks, LANES, LANES), F32),
               pltpu.VMEM((seq, LANES), BF16),
               pltpu.VMEM((seq, LANES), F32),
               pltpu.VMEM((n_chunks, LANES, LANES), BF16)]),
        compiler_params=_params("parallel", "parallel"),
        name="rwkv7",
    )(p3, p3, p3, ps, mu_rkv, mu_rkv, mu_rkv, mu_sm,
      v2(w0), v2(a0), v2(k_k), v2(k_a), v2(r_k), v2(ln_w), v2(ln_b),
      w2p, a2p, g2p)


def _fgate_kernel(f_ref, bf_ref, col_ref, row_ref, *, seq):
    nblk = seq // LANES
    ri = lax.broadcasted_iota(jnp.int32, (LANES, LANES), 0)
    ci = lax.broadcasted_iota(jnp.int32, (LANES, LANES), 1)
    tri = (ci <= ri).astype(BF16)

    def body(i, carry):
        t0 = pl.multiple_of(i * LANES, LANES)
        z = f_ref[0, pl.ds(t0, LANES), :] + bf_ref[...]
        logf = jnp.minimum(z, 0.0) - jnp.log1p(jnp.exp(-jnp.abs(z)))
        hi = logf.astype(BF16)
        r1 = logf - hi.astype(F32)
        mid = r1.astype(BF16)
        lo = (r1 - mid.astype(F32)).astype(BF16)
        cum = carry + (_dot(tri, hi) + _dot(tri, mid) + _dot(tri, lo))
        col_ref[0, pl.ds(t0, LANES), :] = cum
        row_ref[0, :, pl.ds(t0, LANES)] = cum.T
        return cum[LANES - 1:LANES, :]

    lax.fori_loop(0, nblk, body, jnp.zeros((1, LANES), F32))


def _fgate(p_sm, b_f_pad, off, bsz, seq):
    ps = p_sm.reshape(bsz, seq, p_sm.shape[-1])
    return pl.pallas_call(
        functools.partial(_fgate_kernel, seq=seq),
        grid=(bsz,),
        in_specs=[pl.BlockSpec((1, seq, LANES), lambda b: (b, 0, off // LANES)),
                  pl.BlockSpec((1, LANES), lambda b: (0, 0))],
        out_specs=[pl.BlockSpec((1, seq, LANES), lambda b: (b, 0, 0)),
                   pl.BlockSpec((1, LANES, seq), lambda b: (b, 0, 0))],
        out_shape=[jax.ShapeDtypeStruct((bsz, seq, LANES), F32),
                   jax.ShapeDtypeStruct((bsz, LANES, seq), F32)],
        compiler_params=_params("parallel"),
        name="fox_gates",
    )(ps, b_f_pad)


FOX_HEADS_PER_STEP = 2


def _fox_kernel(q_ref, k_ref, v_ref, ccol_ref, crow_ref, on_ref, o_ref, *, tq, scale):
    hg = pl.program_id(1)
    i = pl.program_id(2)
    heads = range(FOX_HEADS_PER_STEP)
    hsl = [slice(g * FOX_HEAD, (g + 1) * FOX_HEAD) for g in heads]
    q = [q_ref[0, :, sl] for sl in hsl]
    lane = lax.broadcasted_iota(jnp.int32, (tq, LANES), 1)
    ccol = ccol_ref[0]
    cq = [jnp.sum(jnp.where(lane == hg * FOX_HEADS_PER_STEP + g, ccol, 0.0),
                  axis=-1, keepdims=True) for g in heads]
    causal = (lax.broadcasted_iota(jnp.int32, (tq, tq), 1)
              <= lax.broadcasted_iota(jnp.int32, (tq, tq), 0))

    def block(j, carry, diagonal):
        m, l, acc = carry
        k0 = pl.multiple_of(j * tq, tq)
        kb = [k_ref[0, pl.ds(k0, tq), sl] for sl in hsl]
        vb = [v_ref[0, pl.ds(k0, tq), sl] for sl in hsl]
        ck = [crow_ref[0, g, :, pl.ds(k0, tq)] for g in heads]
        s = [_dot_nt(q[g], kb[g]) * scale + cq[g] - ck[g] for g in heads]
        if diagonal:
            s = [jnp.where(causal, x, -jnp.inf) for x in s]
        m_new = [jnp.maximum(m[g], jnp.max(s[g], axis=-1, keepdims=True)) for g in heads]
        alpha = [jnp.exp(m[g] - m_new[g]) for g in heads]
        p = [jnp.exp(s[g] - m_new[g]) for g in heads]
        l = [alpha[g] * l[g] + jnp.sum(p[g], axis=-1, keepdims=True) for g in heads]
        acc = [alpha[g] * acc[g] + _dot(p[g].astype(BF16), vb[g]) for g in heads]
        return tuple(m_new), tuple(l), tuple(acc)

    init = (tuple(jnp.full((tq, 1), -jnp.inf, F32) for _ in heads),
            tuple(jnp.zeros((tq, 1), F32) for _ in heads),
            tuple(jnp.zeros((tq, FOX_HEAD), F32) for _ in heads))
    carry = lax.fori_loop(0, i, lambda j, c: block(j, c, False), init)
    _, l, acc = block(i, carry, True)
    for g in heads:
        o = acc[g] / l[g]
        o = o * lax.rsqrt(jnp.mean(o * o, axis=-1, keepdims=True) + NORM_EPS) * on_ref[g]
        o_ref[0, :, hsl[g]] = o.astype(o_ref.dtype)


def _fox(p_fox, ccol, crow, out_norm, bsz, seq):
    fw = p_fox.shape[-1] // 3
    nh = fw // FOX_HEAD
    hps = FOX_HEADS_PER_STEP
    ng = nh // hps
    wid = hps * FOX_HEAD
    tq = _pick(seq, (512, 256, 128))
    p3 = p_fox.reshape(bsz, seq, 3 * fw)
    crow4 = crow.reshape(bsz, LANES, 1, seq)
    on = out_norm.reshape(nh, 1, FOX_HEAD)
    return pl.pallas_call(
        functools.partial(_fox_kernel, tq=tq, scale=FOX_HEAD ** -0.5),
        grid=(bsz, ng, seq // tq),
        in_specs=[pl.BlockSpec((1, tq, wid), lambda b, h, i: (b, i, h)),
                  pl.BlockSpec((1, seq, wid), lambda b, h, i: (b, 0, ng + h)),
                  pl.BlockSpec((1, seq, wid), lambda b, h, i: (b, 0, 2 * ng + h)),
                  pl.BlockSpec((1, tq, LANES), lambda b, h, i: (b, i, 0)),
                  pl.BlockSpec((1, hps, 1, seq), lambda b, h, i: (b, h, 0, 0)),
                  pl.BlockSpec((hps, 1, FOX_HEAD), lambda b, h, i: (h, 0, 0))],
        out_specs=pl.BlockSpec((1, tq, wid), lambda b, h, i: (b, i, h)),
        out_shape=jax.ShapeDtypeStruct((bsz, seq, fw), BF16),
        compiler_params=_params("parallel", "parallel", "parallel"),
        name="fox_attention",
    )(p3, p3, p3, ccol, crow4, on)


def _pad_rows(w, rows):
    return jnp.pad(w, ((0, rows - w.shape[0]), (0, 0)))


def _pad_cols(w, cols):
    return jnp.pad(w, ((0, 0), (0, cols - w.shape[1])))


def _ffn(x2, h, w_gate, w_up, w_down, gate, seq):
    dff = w_gate.shape[1]
    dffp = _round_up(dff, 1024)
    wg = _pad_cols(w_gate, dffp).astype(BF16)
    wu = _pad_cols(w_up, dffp).astype(BF16)
    wd = _pad_rows(w_down, dffp).astype(BF16)
    act = _swiglu_up(h, wg, wu)
    return _matmul_residual(act, wd, x2, gate, seq, 0.5)


def kernel(x, c, w_mod, b_mod, norm_ffn1, ffn1_gate, ffn1_up, ffn1_down, norm_mix, w_in, rwkv_mu, rwkv_w0, rwkv_w2, rwkv_a0, rwkv_a2, rwkv_g2, rwkv_k_k, rwkv_k_a, rwkv_r_k, rwkv_ln_w, rwkv_ln_b, fox_b_f, fox_out_norm, w_out, norm_ffn2, ffn2_gate, ffn2_up, ffn2_down, norm_final):
    bsz, seq, d = x.shape
    depth = w_mod.shape[0]
    rw = rwkv_w0.shape[-1]
    fw = fox_out_norm.shape[-1]
    nfh = fox_b_f.shape[-1]
    dl, al, gl = rwkv_w2.shape[1], rwkv_a2.shape[1], rwkv_g2.shape[1]
    dlp, alp, glp = (_round_up(n, LANES) for n in (dl, al, gl))
    c3 = 3 * rw
    c4, c5, c6 = c3 + dl, c3 + dl + al, c3 + dl + al + gl

    x2 = x.reshape(bsz * seq, d)
    for l in range(depth):
        mod = _mod(c, w_mod[l], b_mod[l])
        sh1, sc1, gt1, sh2, sc2, gt2, sh3, sc3, gt3 = (
            m[:, None, :] for m in jnp.split(mod, 9, axis=-1))

        h = _norm_mod(x2, norm_ffn1[l], sc1, sh1, seq)
        x2 = _ffn(x2, h, ffn1_gate[l], ffn1_up[l], ffn1_down[l], gt1, seq)

        h = _norm_mod(x2, norm_mix[l], sc2, sh2, seq)
        wi, mu = w_in[l], rwkv_mu[l]
        w_rkv = wi[:, :c3].astype(BF16)
        w_sm = jnp.concatenate(
            [_pad_cols(wi[:, c3:c4], dlp), _pad_cols(wi[:, c4:c5], alp),
             _pad_cols(wi[:, c5:c6], glp), _pad_cols(wi[:, c6 + 3 * fw:], LANES)],
            axis=1).astype(BF16)
        w_fox = wi[:, c6:c6 + 3 * fw].astype(BF16)
        mu_rkv = mu[:c3].reshape(1, c3)
        mu_sm = jnp.concatenate(
            [jnp.pad(mu[c3:c4], (0, dlp - dl)), jnp.pad(mu[c4:c5], (0, alp - al)),
             jnp.pad(mu[c5:c6], (0, glp - gl))]).reshape(1, dlp + alp + glp)
        p_rkv = _matmul(h, w_rkv, F32)
        p_sm = _matmul(h, w_sm, F32)
        p_fox = _matmul(h, w_fox, BF16)

        y_r = _rwkv(p_rkv, p_sm, mu_rkv, mu_sm,
                    rwkv_w0[l], rwkv_a0[l], rwkv_k_k[l], rwkv_k_a[l], rwkv_r_k[l],
                    rwkv_ln_w[l], rwkv_ln_b[l],
                    _pad_rows(rwkv_w2[l], dlp).astype(BF16),
                    _pad_rows(rwkv_a2[l], alp).astype(BF16),
                    _pad_rows(rwkv_g2[l], glp).astype(BF16), bsz, seq)

        b_f_pad = jnp.pad(fox_b_f[l], (0, LANES - nfh)).reshape(1, LANES)
        ccol, crow = _fgate(p_sm, b_f_pad, dlp + alp + glp, bsz, seq)
        y_f = _fox(p_fox, ccol, crow, fox_out_norm[l], bsz, seq)

        x2 = _matmul_residual2(y_r.reshape(bsz * seq, rw), y_f.reshape(bsz * seq, fw),
                               w_out[l].astype(BF16), x2, gt2, seq, 1.0)

        h = _norm_mod(x2, norm_ffn2[l], sc3, sh3, seq)
        x2 = _ffn(x2, h, ffn2_gate[l], ffn2_up[l], ffn2_down[l], gt3, seq)
    return _final_norm(x2, norm_final).reshape(bsz, seq, d)
```

```python
import functools

import jax
import jax.numpy as jnp
from jax import lax
from jax.experimental import pallas as pl
from jax.experimental.pallas import tpu as pltpu

F32 = jnp.float32
BF16 = jnp.bfloat16

LANES = 128
RWKV_HEAD = 64
FOX_HEAD = 128
CHUNK = 64
INV_BASE = 4
GROUP = 16
NORM_EPS = 1e-6
LN_X_EPS = 64e-5
VMEM_LIMIT = 56 * 1024 * 1024


def _round_up(n, m):
    return (n + m - 1) // m * m


def _pick(n, candidates):
    for c in candidates:
        if n % c == 0:
            return c
    return n


def _dot(a, b):
    return jnp.dot(a, b, preferred_element_type=F32)


def _dot_nt(a, b):
    return lax.dot_general(a, b, (((1,), (1,)), ((), ())), preferred_element_type=F32)


def _dot_tn(a, b):
    return lax.dot_general(a, b, (((0,), (0,)), ((), ())), preferred_element_type=F32)


def _params(*sem):
    return pltpu.CompilerParams(dimension_semantics=sem, vmem_limit_bytes=VMEM_LIMIT)


def _mod_kernel(c_ref, w_ref, b_ref, o_ref):
    c = c_ref[...]
    a = (c * jax.nn.sigmoid(c)).astype(BF16)
    o_ref[...] = _dot(a, w_ref[...].astype(BF16)) + b_ref[...]


def _mod(c, w_mod, b_mod):
    bsz, d = c.shape
    n = w_mod.shape[1]
    rows = _round_up(bsz, 16)
    c_pad = jnp.pad(c, ((0, rows - bsz), (0, 0)))
    tn = _pick(n, (512, 256, 128))
    out = pl.pallas_call(
        _mod_kernel,
        grid=(n // tn,),
        in_specs=[pl.BlockSpec((rows, d), lambda j: (0, 0)),
                  pl.BlockSpec((d, tn), lambda j: (0, j)),
                  pl.BlockSpec((1, tn), lambda j: (0, j))],
        out_specs=pl.BlockSpec((rows, tn), lambda j: (0, j)),
        out_shape=jax.ShapeDtypeStruct((rows, n), F32),
        compiler_params=_params("parallel"),
        name="mod_matmul",
    )(c_pad, w_mod, b_mod.reshape(1, n))
    return out[:bsz]


def _norm_kernel(x_ref, g_ref, sc_ref, sh_ref, o_ref):
    x = x_ref[...]
    y = x * lax.rsqrt(jnp.mean(x * x, axis=-1, keepdims=True) + NORM_EPS) * g_ref[...]
    o_ref[...] = (y * (1.0 + sc_ref[0]) + sh_ref[0]).astype(o_ref.dtype)


def _final_norm_kernel(x_ref, g_ref, o_ref):
    x = x_ref[...]
    o_ref[...] = x * lax.rsqrt(jnp.mean(x * x, axis=-1, keepdims=True) + NORM_EPS) * g_ref[...]


def _norm_mod(x2, g, sc, sh, seq):
    t, d = x2.shape
    ts = _pick(seq, (256, 128, 64, 8))
    return pl.pallas_call(
        _norm_kernel,
        grid=(t // ts,),
        in_specs=[pl.BlockSpec((ts, d), lambda i: (i, 0)),
                  pl.BlockSpec((1, d), lambda i: (0, 0)),
                  pl.BlockSpec((1, 1, d), lambda i: (i * ts // seq, 0, 0)),
                  pl.BlockSpec((1, 1, d), lambda i: (i * ts // seq, 0, 0))],
        out_specs=pl.BlockSpec((ts, d), lambda i: (i, 0)),
        out_shape=jax.ShapeDtypeStruct((t, d), BF16),
        compiler_params=_params("parallel"),
        name="norm_mod",
    )(x2, g.reshape(1, d), sc, sh)


def _final_norm(x2, g):
    t, d = x2.shape
    ts = _pick(t, (256, 128, 64, 8))
    return pl.pallas_call(
        _final_norm_kernel,
        grid=(t // ts,),
        in_specs=[pl.BlockSpec((ts, d), lambda i: (i, 0)),
                  pl.BlockSpec((1, d), lambda i: (0, 0))],
        out_specs=pl.BlockSpec((ts, d), lambda i: (i, 0)),
        out_shape=jax.ShapeDtypeStruct((t, d), F32),
        compiler_params=_params("parallel"),
        name="final_norm",
    )(x2, g.reshape(1, d))


def _mm_kernel(x_ref, w_ref, o_ref):
    o_ref[...] = _dot(x_ref[...], w_ref[...]).astype(o_ref.dtype)


def _matmul(x, w, out_dtype):
    m, kd = x.shape
    n = w.shape[1]
    tm = _pick(m, (1024, 512, 256, 128))
    tn = _pick(n, (512, 640, 256, 128))
    return pl.pallas_call(
        _mm_kernel,
        grid=(m // tm, n // tn),
        in_specs=[pl.BlockSpec((tm, kd), lambda i, j: (i, 0)),
                  pl.BlockSpec((kd, tn), lambda i, j: (0, j))],
        out_specs=pl.BlockSpec((tm, tn), lambda i, j: (i, j)),
        out_shape=jax.ShapeDtypeStruct((m, n), out_dtype),
        compiler_params=_params("parallel", "parallel"),
        name="matmul",
    )(x, w)


def _swiglu_kernel(x_ref, wg_ref, wu_ref, o_ref, wgb_ref, wub_ref, *, n_valid):
    j = pl.program_id(0)

    @pl.when((pl.program_id(1) == 0) & (j < n_valid))
    def _():
        wgb_ref[...] = wg_ref[...].astype(BF16)
        wub_ref[...] = wu_ref[...].astype(BF16)

    @pl.when(j < n_valid)
    def _():
        x = x_ref[...]
        g = _dot(x, wgb_ref[...])
        u = _dot(x, wub_ref[...])
        o_ref[...] = (g * (0.5 * jnp.tanh(0.5 * g) + 0.5) * u).astype(o_ref.dtype)

    @pl.when(j >= n_valid)
    def _():
        o_ref[...] = jnp.zeros_like(o_ref)


def _swiglu_up(h, wg, wu, n_out):
    m, kd = h.shape
    n = wg.shape[1]
    tm = _pick(m, (1024, 512, 256, 128))
    tn = _pick(n, (256, 128))
    assert n % tn == 0 and n_out % tn == 0
    n_valid = n // tn
    wspec = pl.BlockSpec((kd, tn), lambda j, i: (0, jnp.minimum(j, n_valid - 1)))
    return pl.pallas_call(
        functools.partial(_swiglu_kernel, n_valid=n_valid),
        grid=(n_out // tn, m // tm),
        in_specs=[pl.BlockSpec((tm, kd), lambda j, i: (i, 0)), wspec, wspec],
        out_specs=pl.BlockSpec((tm, tn), lambda j, i: (i, j)),
        out_shape=jax.ShapeDtypeStruct((m, n_out), BF16),
        scratch_shapes=[pltpu.VMEM((kd, tn), BF16), pltpu.VMEM((kd, tn), BF16)],
        compiler_params=_params("parallel", "arbitrary"),
        name="swiglu_up",
    )(h, wg, wu)


def _resid_kernel(a_ref, w_ref, x_ref, gt_ref, o_ref, acc_ref, *, nk, scale):
    k = pl.program_id(2)

    @pl.when(k == 0)
    def _():
        acc_ref[...] = _dot(a_ref[...], w_ref[...])

    @pl.when((k > 0) & (k < nk - 1))
    def _():
        acc_ref[...] += _dot(a_ref[...], w_ref[...])

    @pl.when(k == nk - 1)
    def _():
        o_ref[...] = x_ref[...] + (scale * gt_ref[0]) * (acc_ref[...] + _dot(a_ref[...], w_ref[...]))


def _matmul_residual(a, w, x2, gate, seq, scale):
    m, kd = a.shape
    n = w.shape[1]
    tm = _pick(seq, (1024, 512, 256, 128))
    tn = _pick(n, (1024, 512, 256, 128))
    tk = _pick(kd, (2816, 1024, 512, 256, 128))
    nk = kd // tk
    assert nk >= 2
    return pl.pallas_call(
        functools.partial(_resid_kernel, nk=nk, scale=scale),
        grid=(m // tm, n // tn, nk),
        in_specs=[pl.BlockSpec((tm, tk), lambda i, j, k: (i, k)),
                  pl.BlockSpec((tk, tn), lambda i, j, k: (k, j)),
                  pl.BlockSpec((tm, tn), lambda i, j, k: (i, j)),
                  pl.BlockSpec((1, 1, tn), lambda i, j, k: (i * tm // seq, 0, j))],
        out_specs=pl.BlockSpec((tm, tn), lambda i, j, k: (i, j)),
        out_shape=jax.ShapeDtypeStruct((m, n), F32),
        scratch_shapes=[pltpu.VMEM((tm, tn), F32)],
        compiler_params=_params("parallel", "parallel", "arbitrary"),
        name="matmul_residual",
    )(a, w, x2, gate)


def _resid2_kernel(a1_ref, a2_ref, w1_ref, w2_ref, x_ref, gt_ref, o_ref, *, scale):
    y = _dot(a1_ref[...], w1_ref[...]) + _dot(a2_ref[...], w2_ref[...])
    o_ref[...] = x_ref[...] + (scale * gt_ref[0]) * y


def _matmul_residual2(a1, a2, w, x2, gate, seq, scale):
    m, k1 = a1.shape
    k2 = a2.shape[1]
    n = w.shape[1]
    assert k1 == k2
    tm = _pick(seq, (1024, 512, 256, 128))
    tn = _pick(n, (512, 256, 128))
    return pl.pallas_call(
        functools.partial(_resid2_kernel, scale=scale),
        grid=(m // tm, n // tn),
        in_specs=[pl.BlockSpec((tm, k1), lambda i, j: (i, 0)),
                  pl.BlockSpec((tm, k2), lambda i, j: (i, 0)),
                  pl.BlockSpec((k1, tn), lambda i, j: (0, j)),
                  pl.BlockSpec((k2, tn), lambda i, j: (1, j)),
                  pl.BlockSpec((tm, tn), lambda i, j: (i, j)),
                  pl.BlockSpec((1, 1, tn), lambda i, j: (i * tm // seq, 0, j))],
        out_specs=pl.BlockSpec((tm, tn), lambda i, j: (i, j)),
        out_shape=jax.ShapeDtypeStruct((m, n), F32),
        compiler_params=_params("parallel", "parallel"),
        name="out_proj_residual",
    )(a1, a2, w, w, x2, gate)


ROW_TILE = 512


def _lane_head0(shape):
    return lax.broadcasted_iota(jnp.int32, shape, 1) < RWKV_HEAD


def _stack(x):
    m0 = _lane_head0(x.shape)
    zero = jnp.zeros_like(x)
    return jnp.concatenate([jnp.where(m0, x, zero), jnp.where(m0, zero, x)], axis=0)


def _split_dot(lhs_bf16, x):
    hi = x.astype(BF16)
    lo = (x - hi.astype(F32)).astype(BF16)
    return _dot(lhs_bf16, hi) + _dot(lhs_bf16, lo)


def _lora_in_kernel(ps_ref, prev_ref, mu_ref, o_ref, *, lora_w, lora_a):
    x = ps_ref[0]
    prev_row = jnp.where(pl.program_id(1) == 0, 0.0, prev_ref[0][7:8, :])
    row = lax.broadcasted_iota(jnp.int32, x.shape, 0)
    prev = jnp.where(row == 0, prev_row, pltpu.roll(x, 1, 0))
    sm = x + (prev - x) * mu_ref[...]
    o_ref[0, :, :lora_w] = jnp.tanh(sm[:, :lora_w]).astype(o_ref.dtype)
    o_ref[0, :, lora_w:lora_w + lora_a] = sm[:, lora_w:lora_w + lora_a].astype(o_ref.dtype)
    o_ref[0, :, lora_w + lora_a:] = jax.nn.sigmoid(sm[:, lora_w + lora_a:]).astype(o_ref.dtype)


def _lora_in(p_sm, mu_sm, lora_w, lora_a, bsz, seq):
    wsm = mu_sm.shape[-1]
    ps = p_sm.reshape(bsz, seq, p_sm.shape[-1])
    ts = _pick(seq, (ROW_TILE, 128, 64, 8))
    return pl.pallas_call(
        functools.partial(_lora_in_kernel, lora_w=lora_w, lora_a=lora_a),
        grid=(bsz, seq // ts),
        in_specs=[pl.BlockSpec((1, ts, wsm), lambda b, i: (b, i, 0)),
                  pl.BlockSpec((1, 8, wsm), lambda b, i: (b, jnp.maximum(i * (ts // 8) - 1, 0), 0)),
                  pl.BlockSpec((1, wsm), lambda b, i: (0, 0))],
        out_specs=pl.BlockSpec((1, ts, wsm), lambda b, i: (b, i, 0)),
        out_shape=jax.ShapeDtypeStruct((bsz, seq, wsm), BF16),
        compiler_params=_params("parallel", "parallel"),
        name="rwkv_lora_in",
    )(ps, ps, mu_sm)


def _rwkv_kernel(pr_ref, pk_ref, pv_ref, ps_ref, mur_ref, muk_ref, muv_ref,
                 w0_ref, a0_ref, kk_ref, ka_ref, rk_ref, lnw_ref, lnb_ref,
                 w2_ref, a2_ref, g2_ref, o_ref,
                 at_s, rt_s, kt_s, bt_s, kg_s, bg_s, v_s, g_s, bonus_s, gc_s,
                 m_s, n_s, q_s, yb_s, sb_s, *, seq, lora_w, lora_a):
    n_tiles = seq // ROW_TILE
    n_chunks = seq // CHUNK
    cpt = ROW_TILE // CHUNK

    ri = lax.broadcasted_iota(jnp.int32, (LANES, LANES), 0)
    ci = lax.broadcasted_iota(jnp.int32, (LANES, LANES), 1)
    bd_mask = (ri // RWKV_HEAD) == (ci // RWKV_HEAD)
    bd_ones = bd_mask.astype(BF16)
    tr = lax.broadcasted_iota(jnp.int32, (ROW_TILE, ROW_TILE), 0)
    tc = lax.broadcasted_iota(jnp.int32, (ROW_TILE, ROW_TILE), 1)
    tri = ((tc <= tr) & ((tr // CHUNK) == (tc // CHUNK))).astype(BF16)

    def head_sum(x):
        return _split_dot_rhs(x, bd_ones)

    def shift_mix(ref, t0, mu):
        x = ref[0, pl.ds(t0, ROW_TILE), :]
        p0 = pl.multiple_of(jnp.maximum(t0 - 8, 0), 8)
        prev8 = ref[0, pl.ds(p0, 8), :]
        prev_row = jnp.where(t0 == 0, 0.0, prev8[7:8, :])
        row = lax.broadcasted_iota(jnp.int32, x.shape, 0)
        prev = jnp.where(row == 0, prev_row, pltpu.roll(x, 1, 0))
        return x + (prev - x) * mu

    def phase_a(i, carry):
        t0 = pl.multiple_of(i * ROW_TILE, ROW_TILE)
        r = shift_mix(pr_ref, t0, mur_ref[...])
        k = shift_mix(pk_ref, t0, muk_ref[...])
        v = shift_mix(pv_ref, t0, muv_ref[...])
        rows = pl.ds(t0, ROW_TILE)
        wlin = w0_ref[...] + _dot(ps_ref[0, rows, :lora_w], w2_ref[...])
        wlog = -(jnp.maximum(-wlin, 0.0) + jnp.log(1.0 + jnp.exp(-jnp.abs(wlin)))) - 0.5
        ld = -jnp.exp(wlog)
        alr = jax.nn.sigmoid(a0_ref[...] + _dot(ps_ref[0, rows, lora_w:lora_w + lora_a], a2_ref[...]))
        g = _dot(ps_ref[0, rows, lora_w + lora_a:], g2_ref[...])
        kk = k * kk_ref[...]
        kk = kk * lax.rsqrt(jnp.maximum(head_sum(kk * kk), 1e-24))
        k = k * (1.0 + (alr - 1.0) * ka_ref[...])
        bonus = head_sum(r * k * rk_ref[...]) * v
        b = kk * alr
        cum = _split_dot(tri, ld)
        g_dec = jnp.exp(cum)
        g_inv = jnp.exp(-cum)
        at_s[pl.ds(t0, ROW_TILE), :] = (-kk * jnp.exp(cum - ld)).astype(BF16)
        rt_s[pl.ds(t0, ROW_TILE), :] = (r * g_dec).astype(BF16)
        kt_s[pl.ds(t0, ROW_TILE), :] = (k * g_inv).astype(BF16)
        bt_s[pl.ds(t0, ROW_TILE), :] = (b * g_inv).astype(BF16)
        v_s[pl.ds(t0, ROW_TILE), :] = v.astype(BF16)
        g_s[pl.ds(t0, ROW_TILE), :] = g
        bonus_s[pl.ds(t0, ROW_TILE), :] = bonus
        for c in range(cpt):
            lo, hi = c * CHUNK, (c + 1) * CHUNK
            cum_c = cum[hi - 1:hi, :]
            rem = jnp.exp(cum_c - cum[lo:hi, :])
            kg_s[pl.ds(t0 + lo, CHUNK), :] = (k[lo:hi, :] * rem).astype(BF16)
            bg_s[pl.ds(t0 + lo, CHUNK), :] = (b[lo:hi, :] * rem).astype(BF16)
            gc_s[i * cpt + c] = jnp.broadcast_to(jnp.exp(cum_c), (8, LANES))
        return carry

    lax.fori_loop(0, n_tiles, phase_a, 0)

    row = lax.broadcasted_iota(jnp.int32, (CHUNK, LANES), 0)
    col = lax.broadcasted_iota(jnp.int32, (CHUNK, LANES), 1) % RWKV_HEAD
    strict = col < row
    incl = col <= row
    eye_p = (col == row).astype(F32)

    def same_block(size):
        return (row // size) == (col // size)

    def each(fn, *lists):
        return [fn(*xs) for xs in zip(*lists)]

    def phase_b(i, tick):
        cs = [i * GROUP + g for g in range(GROUP)]
        sls = [pl.ds(pl.multiple_of(c * CHUNK, CHUNK), CHUNK) for c in cs]
        at = [at_s[sl, :] for sl in sls]
        rt = [rt_s[sl, :] for sl in sls]
        kt = [kt_s[sl, :] for sl in sls]
        bt = [bt_s[sl, :] for sl in sls]
        v = [v_s[sl, :] for sl in sls]
        sc = each(lambda a, r, k, b: _dot_nt(jnp.concatenate([a, r], axis=0),
                                             jnp.concatenate([_stack(k), _stack(b)], axis=0)),
                  at, rt, kt, bt)
        a_ak = [jnp.where(strict, x[:CHUNK, :LANES], 0.0).astype(BF16) for x in sc]
        a_ab = [jnp.where(strict, x[:CHUNK, LANES:], 0.0) for x in sc]
        a_rk = [jnp.where(incl, x[CHUNK:, :LANES], 0.0).astype(BF16) for x in sc]
        a_rb = [jnp.where(incl, x[CHUNK:, LANES:], 0.0).astype(BF16) for x in sc]
        tick()
        d = [jnp.where(same_block(INV_BASE), x, 0.0) for x in a_ab]
        db = [x.astype(BF16) for x in d]
        d2 = [_dot(x, _stack(x)).astype(BF16) for x in db]
        tick()
        p = [eye_p + x for x in d]
        p = each(lambda pp, x2: pp + _dot(pp.astype(BF16), _stack(x2)), p, d2)
        tick()
        s = INV_BASE
        while s < CHUNK:
            lower_left = same_block(2 * s) & ((row // s) % 2 == 1) & ((col // s) % 2 == 0)
            a21 = [jnp.where(lower_left, x, 0.0).astype(BF16) for x in a_ab]
            pb = [x.astype(BF16) for x in p]
            a21_t11 = each(lambda a, t: _dot(a, _stack(t)).astype(BF16), a21, pb)
            tick()
            p = each(lambda pp, t, y: pp + _dot(t, _stack(y)), p, pb, a21_t11)
            tick()
            s *= 2
        tinv = [x.astype(BF16) for x in p]
        vst = [_stack(x) for x in v]
        x1 = each(lambda a, vs: _dot(a, vs).astype(BF16), a_ak, vst)
        tick()
        wu = each(lambda t, a, x: _dot(t, jnp.concatenate([_stack(a), _stack(x)], axis=1)),
                  tinv, at, x1)
        tick()
        wb = [x[:, :LANES].astype(BF16) for x in wu]
        ub = [x[:, LANES:].astype(BF16) for x in wu]
        qy = each(lambda a, w, u: _dot(a, jnp.concatenate([_stack(w), _stack(u)], axis=1)),
                  a_rb, wb, ub)
        tick()
        yb = each(lambda a, vs, x: _dot(a, vs) + x[:, LANES:], a_rk, vst, qy)
        tick()
        for g in range(GROUP):
            q_s[sls[g], :] = (rt[g].astype(F32) + qy[g][:, :LANES]).astype(BF16)
            yb_s[sls[g], :] = yb[g]
        kg = [kg_s[sl, :] for sl in sls]
        bg = [bg_s[sl, :] for sl in sls]
        mm = each(_dot_tn, wb, bg)
        tick()
        nn = each(lambda vv, u, k, b: _dot_tn(jnp.concatenate([vv, u], axis=0),
                                              jnp.concatenate([k, b], axis=0)), v, ub, kg, bg)
        tick()
        for g in range(GROUP):
            m_s[cs[g]] = jnp.where(bd_mask, mm[g], 0.0).astype(BF16)
            n_s[cs[g]] = jnp.where(bd_mask, nn[g], 0.0)

    def c1_step(c, s):
        sb = s.astype(BF16)
        sb_s[c] = sb
        return s * gc_s[c][0:1, :] + _dot(sb, m_s[c]) + n_s[c]

    def group_with_c1(i, s):
        state = [s]
        pending = [(i - 1) * GROUP + g for g in range(GROUP)]

        def tick():
            if pending:
                state[0] = c1_step(pending.pop(0), state[0])

        phase_b(i, tick)
        while pending:
            tick()
        return state[0]

    n_groups = n_chunks // GROUP
    phase_b(0, lambda: None)
    s_last = lax.fori_loop(1, n_groups, group_with_c1, jnp.zeros((LANES, LANES), F32))

    inv_n = 1.0 / RWKV_HEAD

    def phase_c2(i, tick):
        cs = [i * GROUP + g for g in range(GROUP)]
        sls = [pl.ds(c * CHUNK, CHUNK) for c in cs]

        def staged(fn, *lists):
            out = []
            for g, xs in enumerate(zip(*lists)):
                out.append(fn(*xs))
                if g % 4 == 3:
                    tick()
            return out

        y = staged(lambda sl, c: _dot_nt(q_s[sl, :], sb_s[c]) + yb_s[sl, :], sls, cs)
        mean = staged(lambda x: head_sum(x) * inv_n, y)
        yc = each(lambda a, b: a - b, y, mean)
        var = staged(lambda x: head_sum(x * x) * inv_n, yc)
        for g in range(GROUP):
            yn = yc[g] * lax.rsqrt(var[g] + LN_X_EPS) * lnw_ref[...] + lnb_ref[...]
            o_ref[0, sls[g], :] = ((yn + bonus_s[sls[g], :]) * g_s[sls[g], :]).astype(o_ref.dtype)

    state = [s_last]
    pending = [(n_groups - 1) * GROUP + g for g in range(GROUP)]

    def tail_tick():
        if pending:
            state[0] = c1_step(pending.pop(0), state[0])

    for i in range(n_groups - 1):
        phase_c2(i, tail_tick)
    while pending:
        tail_tick()
    phase_c2(n_groups - 1, lambda: None)


def _split_dot_rhs(x, rhs_bf16):
    hi = x.astype(BF16)
    lo = (x - hi.astype(F32)).astype(BF16)
    return _dot(hi, rhs_bf16) + _dot(lo, rhs_bf16)


def _rwkv(p_rkv, p_sm, mu_rkv, mu_sm, w0, a0, k_k, k_a, r_k, ln_w, ln_b,
          w2p, a2p, g2p, bsz, seq):
    rw = w0.shape[-1]
    npair = rw // LANES
    lora_w, lora_a, lora_g = w2p.shape[0], a2p.shape[0], g2p.shape[0]
    wsm = lora_w + lora_a + lora_g
    p3 = p_rkv.reshape(bsz, seq, 3 * rw)
    ps = _lora_in(p_sm, mu_sm, lora_w, lora_a, bsz, seq)
    n_chunks = seq // CHUNK

    def col(off):
        return pl.BlockSpec((1, seq, LANES), lambda b, h: (b, 0, off + h))

    def vec(off):
        return pl.BlockSpec((1, LANES), lambda b, h: (0, off + h))

    def v2(a):
        return a.reshape(1, rw)

    return pl.pallas_call(
        functools.partial(_rwkv_kernel, seq=seq, lora_w=lora_w, lora_a=lora_a),
        grid=(bsz, npair),
        in_specs=[col(0), col(npair), col(2 * npair),
                  pl.BlockSpec((1, seq, wsm), lambda b, h: (b, 0, 0)),
                  vec(0), vec(npair), vec(2 * npair),
                  vec(0), vec(0), vec(0), vec(0), vec(0), vec(0), vec(0),
                  pl.BlockSpec((lora_w, LANES), lambda b, h: (0, h)),
                  pl.BlockSpec((lora_a, LANES), lambda b, h: (0, h)),
                  pl.BlockSpec((lora_g, LANES), lambda b, h: (0, h))],
        out_specs=pl.BlockSpec((1, seq, LANES), lambda b, h: (b, 0, h)),
        out_shape=jax.ShapeDtypeStruct((bsz, seq, rw), BF16),
        scratch_shapes=(
            [pltpu.VMEM((seq, LANES), BF16) for _ in range(7)]
            + [pltpu.VMEM((seq, LANES), F32) for _ in range(2)]
            + [pltpu.VMEM((n_chunks, 8, LANES), F32),
               pltpu.VMEM((n_chunks, LANES, LANES), BF16),
               pltpu.VMEM((n_chunks, LANES, LANES), F32),
               pltpu.VMEM((seq, LANES), BF16),
               pltpu.VMEM((seq, LANES), F32),
               pltpu.VMEM((n_chunks, LANES, LANES), BF16)]),
        compiler_params=_params("parallel", "parallel"),
        name="rwkv7",
    )(p3, p3, p3, ps, mu_rkv, mu_rkv, mu_rkv,
      v2(w0), v2(a0), v2(k_k), v2(k_a), v2(r_k), v2(ln_w), v2(ln_b),
      w2p, a2p, g2p)


def _fgate_kernel(f_ref, bf_ref, col_ref, row_ref, *, seq):
    nblk = seq // LANES
    ri = lax.broadcasted_iota(jnp.int32, (LANES, LANES), 0)
    ci = lax.broadcasted_iota(jnp.int32, (LANES, LANES), 1)
    tri = (ci <= ri).astype(BF16)

    def body(i, carry):
        t0 = pl.multiple_of(i * LANES, LANES)
        z = f_ref[0, pl.ds(t0, LANES), :] + bf_ref[...]
        logf = jnp.minimum(z, 0.0) - jnp.log1p(jnp.exp(-jnp.abs(z)))
        hi = logf.astype(BF16)
        r1 = logf - hi.astype(F32)
        mid = r1.astype(BF16)
        lo = (r1 - mid.astype(F32)).astype(BF16)
        cum = carry + (_dot(tri, hi) + _dot(tri, mid) + _dot(tri, lo))
        col_ref[0, pl.ds(t0, LANES), :] = cum
        row_ref[0, :, pl.ds(t0, LANES)] = cum.T
        return cum[LANES - 1:LANES, :]

    lax.fori_loop(0, nblk, body, jnp.zeros((1, LANES), F32))


def _fgate(p_sm, b_f_pad, off, bsz, seq):
    ps = p_sm.reshape(bsz, seq, p_sm.shape[-1])
    return pl.pallas_call(
        functools.partial(_fgate_kernel, seq=seq),
        grid=(bsz,),
        in_specs=[pl.BlockSpec((1, seq, LANES), lambda b: (b, 0, off // LANES)),
                  pl.BlockSpec((1, LANES), lambda b: (0, 0))],
        out_specs=[pl.BlockSpec((1, seq, LANES), lambda b: (b, 0, 0)),
                   pl.BlockSpec((1, LANES, seq), lambda b: (b, 0, 0))],
        out_shape=[jax.ShapeDtypeStruct((bsz, seq, LANES), F32),
                   jax.ShapeDtypeStruct((bsz, LANES, seq), F32)],
        compiler_params=_params("parallel"),
        name="fox_gates",
    )(ps, b_f_pad)


FOX_HEADS_PER_STEP = 2


def _fox_kernel(q_ref, k_ref, v_ref, ccol_ref, crow_ref, on_ref, o_ref, *, tq, scale):
    hg = pl.program_id(1)
    i = pl.program_id(2)
    heads = range(FOX_HEADS_PER_STEP)
    hsl = [slice(g * FOX_HEAD, (g + 1) * FOX_HEAD) for g in heads]
    q = [q_ref[0, :, sl] for sl in hsl]
    lane = lax.broadcasted_iota(jnp.int32, (tq, LANES), 1)
    ccol = ccol_ref[0]
    cq = [jnp.sum(jnp.where(lane == hg * FOX_HEADS_PER_STEP + g, ccol, 0.0),
                  axis=-1, keepdims=True) for g in heads]
    causal = (lax.broadcasted_iota(jnp.int32, (tq, tq), 1)
              <= lax.broadcasted_iota(jnp.int32, (tq, tq), 0))

    def block(j, carry, diagonal):
        m, l, acc = carry
        k0 = pl.multiple_of(j * tq, tq)
        kb = [k_ref[0, pl.ds(k0, tq), sl] for sl in hsl]
        vb = [v_ref[0, pl.ds(k0, tq), sl] for sl in hsl]
        ck = [crow_ref[0, g, :, pl.ds(k0, tq)] for g in heads]
        s = [_dot_nt(q[g], kb[g]) * scale + cq[g] - ck[g] for g in heads]
        if diagonal:
            s = [jnp.where(causal, x, -jnp.inf) for x in s]
        m_new = [jnp.maximum(m[g], jnp.max(s[g], axis=-1, keepdims=True)) for g in heads]
        alpha = [jnp.exp(m[g] - m_new[g]) for g in heads]
        p = [jnp.exp(s[g] - m_new[g]) for g in heads]
        l = [alpha[g] * l[g] + jnp.sum(p[g], axis=-1, keepdims=True) for g in heads]
        acc = [alpha[g] * acc[g] + _dot(p[g].astype(BF16), vb[g]) for g in heads]
        return tuple(m_new), tuple(l), tuple(acc)

    init = (tuple(jnp.full((tq, 1), -jnp.inf, F32) for _ in heads),
            tuple(jnp.zeros((tq, 1), F32) for _ in heads),
            tuple(jnp.zeros((tq, FOX_HEAD), F32) for _ in heads))
    carry = lax.fori_loop(0, i, lambda j, c: block(j, c, False), init)
    _, l, acc = block(i, carry, True)
    for g in heads:
        o = acc[g] / l[g]
        o = o * lax.rsqrt(jnp.mean(o * o, axis=-1, keepdims=True) + NORM_EPS) * on_ref[g]
        o_ref[0, :, hsl[g]] = o.astype(o_ref.dtype)


def _fox(p_fox, ccol, crow, out_norm, bsz, seq):
    fw = p_fox.shape[-1] // 3
    nh = fw // FOX_HEAD
    hps = FOX_HEADS_PER_STEP
    ng = nh // hps
    wid = hps * FOX_HEAD
    tq = _pick(seq, (512, 256, 128))
    p3 = p_fox.reshape(bsz, seq, 3 * fw)
    crow4 = crow.reshape(bsz, LANES, 1, seq)
    on = out_norm.reshape(nh, 1, FOX_HEAD)
    return pl.pallas_call(
        functools.partial(_fox_kernel, tq=tq, scale=FOX_HEAD ** -0.5),
        grid=(bsz, ng, seq // tq),
        in_specs=[pl.BlockSpec((1, tq, wid), lambda b, h, i: (b, i, h)),
                  pl.BlockSpec((1, seq, wid), lambda b, h, i: (b, 0, ng + h)),
                  pl.BlockSpec((1, seq, wid), lambda b, h, i: (b, 0, 2 * ng + h)),
                  pl.BlockSpec((1, tq, LANES), lambda b, h, i: (b, i, 0)),
                  pl.BlockSpec((1, hps, 1, seq), lambda b, h, i: (b, h, 0, 0)),
                  pl.BlockSpec((hps, 1, FOX_HEAD), lambda b, h, i: (h, 0, 0))],
        out_specs=pl.BlockSpec((1, tq, wid), lambda b, h, i: (b, i, h)),
        out_shape=jax.ShapeDtypeStruct((bsz, seq, fw), BF16),
        compiler_params=_params("parallel", "parallel", "parallel"),
        name="fox_attention",
    )(p3, p3, p3, ccol, crow4, on)


def _pad_rows(w, rows):
    return jnp.pad(w, ((0, rows - w.shape[0]), (0, 0)))


def _pad_cols(w, cols):
    return jnp.pad(w, ((0, 0), (0, cols - w.shape[1])))


def _ffn(x2, h, w_gate, w_up, w_down, gate, seq):
    dff = w_gate.shape[1]
    dffp = _round_up(dff, 1024)
    wd = _pad_rows(w_down.astype(BF16), dffp)
    act = _swiglu_up(h, w_gate, w_up, dffp)
    return _matmul_residual(act, wd, x2, gate, seq, 0.5)


def kernel(x, c, w_mod, b_mod, norm_ffn1, ffn1_gate, ffn1_up, ffn1_down, norm_mix, w_in, rwkv_mu, rwkv_w0, rwkv_w2, rwkv_a0, rwkv_a2, rwkv_g2, rwkv_k_k, rwkv_k_a, rwkv_r_k, rwkv_ln_w, rwkv_ln_b, fox_b_f, fox_out_norm, w_out, norm_ffn2, ffn2_gate, ffn2_up, ffn2_down, norm_final):
    bsz, seq, d = x.shape
    depth = w_mod.shape[0]
    rw = rwkv_w0.shape[-1]
    fw = fox_out_norm.shape[-1]
    nfh = fox_b_f.shape[-1]
    dl, al, gl = rwkv_w2.shape[1], rwkv_a2.shape[1], rwkv_g2.shape[1]
    dlp, alp, glp = (_round_up(n, LANES) for n in (dl, al, gl))
    c3 = 3 * rw
    c4, c5, c6 = c3 + dl, c3 + dl + al, c3 + dl + al + gl

    x2 = x.reshape(bsz * seq, d)
    for l in range(depth):
        mod = _mod(c, w_mod[l], b_mod[l])
        sh1, sc1, gt1, sh2, sc2, gt2, sh3, sc3, gt3 = (
            m[:, None, :] for m in jnp.split(mod, 9, axis=-1))

        h = _norm_mod(x2, norm_ffn1[l], sc1, sh1, seq)
        x2 = _ffn(x2, h, ffn1_gate[l], ffn1_up[l], ffn1_down[l], gt1, seq)

        h = _norm_mod(x2, norm_mix[l], sc2, sh2, seq)
        wi, mu = w_in[l], rwkv_mu[l]
        w_rkv = wi[:, :c3].astype(BF16)
        w_sm = jnp.concatenate(
            [_pad_cols(wi[:, c3:c4], dlp), _pad_cols(wi[:, c4:c5], alp),
             _pad_cols(wi[:, c5:c6], glp), _pad_cols(wi[:, c6 + 3 * fw:], LANES)],
            axis=1).astype(BF16)
        w_fox = wi[:, c6:c6 + 3 * fw].astype(BF16)
        mu_rkv = mu[:c3].reshape(1, c3)
        mu_sm = jnp.concatenate(
            [jnp.pad(mu[c3:c4], (0, dlp - dl)), jnp.pad(mu[c4:c5], (0, alp - al)),
             jnp.pad(mu[c5:c6], (0, glp - gl))]).reshape(1, dlp + alp + glp)
        p_rkv = _matmul(h, w_rkv, F32)
        p_sm = _matmul(h, w_sm, F32)
        p_fox = _matmul(h, w_fox, BF16)

        y_r = _rwkv(p_rkv, p_sm, mu_rkv, mu_sm,
                    rwkv_w0[l], rwkv_a0[l], rwkv_k_k[l], rwkv_k_a[l], rwkv_r_k[l],
                    rwkv_ln_w[l], rwkv_ln_b[l],
                    _pad_rows(rwkv_w2[l], dlp).astype(BF16),
                    _pad_rows(rwkv_a2[l], alp).astype(BF16),
                    _pad_rows(rwkv_g2[l], glp).astype(BF16), bsz, seq)

        b_f_pad = jnp.pad(fox_b_f[l], (0, LANES - nfh)).reshape(1, LANES)
        ccol, crow = _fgate(p_sm, b_f_pad, dlp + alp + glp, bsz, seq)
        y_f = _fox(p_fox, ccol, crow, fox_out_norm[l], bsz, seq)

        x2 = _matmul_residual2(y_r.reshape(bsz * seq, rw), y_f.reshape(bsz * seq, fw),
                               w_out[l].astype(BF16), x2, gt2, seq, 1.0)

        h = _norm_mod(x2, norm_ffn2[l], sc3, sh3, seq)
        x2 = _ffn(x2, h, ffn2_gate[l], ffn2_up[l], ffn2_down[l], gt3, seq)
    return _final_norm(x2, norm_final).reshape(bsz, seq, d)
```

```python
import functools

import jax
import jax.numpy as jnp
from jax import lax
from jax.experimental import pallas as pl
from jax.experimental.pallas import tpu as pltpu

F32 = jnp.float32
BF16 = jnp.bfloat16

LANES = 128
RWKV_HEAD = 64
FOX_HEAD = 128
CHUNK = 64
INV_BASE = 4
GROUP = 16
NORM_EPS = 1e-6
LN_X_EPS = 64e-5
VMEM_LIMIT = 56 * 1024 * 1024


def _round_up(n, m):
    return (n + m - 1) // m * m


def _pick(n, candidates):
    for c in candidates:
        if n % c == 0:
            return c
    return n


def _dot(a, b):
    return jnp.dot(a, b, preferred_element_type=F32)


def _dot_nt(a, b):
    return lax.dot_general(a, b, (((1,), (1,)), ((), ())), preferred_element_type=F32)


def _dot_tn(a, b):
    return lax.dot_general(a, b, (((0,), (0,)), ((), ())), preferred_element_type=F32)


def _params(*sem):
    return pltpu.CompilerParams(dimension_semantics=sem, vmem_limit_bytes=VMEM_LIMIT)


def _mod_kernel(c_ref, w_ref, b_ref, o_ref):
    c = c_ref[...]
    a = (c * jax.nn.sigmoid(c)).astype(BF16)
    o_ref[...] = _dot(a, w_ref[...].astype(BF16)) + b_ref[...]


def _mod(c, w_mod, b_mod):
    bsz, d = c.shape
    n = w_mod.shape[1]
    rows = _round_up(bsz, 16)
    c_pad = jnp.pad(c, ((0, rows - bsz), (0, 0)))
    tn = _pick(n, (512, 256, 128))
    out = pl.pallas_call(
        _mod_kernel,
        grid=(n // tn,),
        in_specs=[pl.BlockSpec((rows, d), lambda j: (0, 0)),
                  pl.BlockSpec((d, tn), lambda j: (0, j)),
                  pl.BlockSpec((1, tn), lambda j: (0, j))],
        out_specs=pl.BlockSpec((rows, tn), lambda j: (0, j)),
        out_shape=jax.ShapeDtypeStruct((rows, n), F32),
        compiler_params=_params("parallel"),
        name="mod_matmul",
    )(c_pad, w_mod, b_mod.reshape(1, n))
    return out[:bsz]


def _norm_kernel(x_ref, g_ref, sc_ref, sh_ref, o_ref):
    x = x_ref[...]
    y = x * lax.rsqrt(jnp.mean(x * x, axis=-1, keepdims=True) + NORM_EPS) * g_ref[...]
    o_ref[...] = (y * (1.0 + sc_ref[0]) + sh_ref[0]).astype(o_ref.dtype)


def _final_norm_kernel(x_ref, g_ref, o_ref):
    x = x_ref[...]
    o_ref[...] = x * lax.rsqrt(jnp.mean(x * x, axis=-1, keepdims=True) + NORM_EPS) * g_ref[...]


def _norm_mod(x2, g, sc, sh, seq):
    t, d = x2.shape
    ts = _pick(seq, (256, 128, 64, 8))
    return pl.pallas_call(
        _norm_kernel,
        grid=(t // ts,),
        in_specs=[pl.BlockSpec((ts, d), lambda i: (i, 0)),
                  pl.BlockSpec((1, d), lambda i: (0, 0)),
                  pl.BlockSpec((1, 1, d), lambda i: (i * ts // seq, 0, 0)),
                  pl.BlockSpec((1, 1, d), lambda i: (i * ts // seq, 0, 0))],
        out_specs=pl.BlockSpec((ts, d), lambda i: (i, 0)),
        out_shape=jax.ShapeDtypeStruct((t, d), BF16),
        compiler_params=_params("parallel"),
        name="norm_mod",
    )(x2, g.reshape(1, d), sc, sh)


def _final_norm(x2, g):
    t, d = x2.shape
    ts = _pick(t, (256, 128, 64, 8))
    return pl.pallas_call(
        _final_norm_kernel,
        grid=(t // ts,),
        in_specs=[pl.BlockSpec((ts, d), lambda i: (i, 0)),
                  pl.BlockSpec((1, d), lambda i: (0, 0))],
        out_specs=pl.BlockSpec((ts, d), lambda i: (i, 0)),
        out_shape=jax.ShapeDtypeStruct((t, d), F32),
        compiler_params=_params("parallel"),
        name="final_norm",
    )(x2, g.reshape(1, d))


def _mm_kernel(x_ref, w_ref, o_ref):
    o_ref[...] = _dot(x_ref[...], w_ref[...]).astype(o_ref.dtype)


def _matmul(x, w, out_dtype):
    m, kd = x.shape
    n = w.shape[1]
    tm = _pick(m, (1024, 512, 256, 128))
    tn = _pick(n, (512, 640, 256, 128))
    return pl.pallas_call(
        _mm_kernel,
        grid=(m // tm, n // tn),
        in_specs=[pl.BlockSpec((tm, kd), lambda i, j: (i, 0)),
                  pl.BlockSpec((kd, tn), lambda i, j: (0, j))],
        out_specs=pl.BlockSpec((tm, tn), lambda i, j: (i, j)),
        out_shape=jax.ShapeDtypeStruct((m, n), out_dtype),
        compiler_params=_params("parallel", "parallel"),
        name="matmul",
    )(x, w)


def _swiglu_kernel(x_ref, wg_hbm, wu_hbm, o_ref, stage_ref, wgb_ref, wub_ref, sem,
                   *, n_full, rem, tn):
    j = pl.program_id(0)

    def full_copies(jj):
        cols = pl.ds(pl.multiple_of(jj * tn, tn), tn)
        return (pltpu.make_async_copy(wg_hbm.at[:, cols], stage_ref.at[0], sem.at[0]),
                pltpu.make_async_copy(wu_hbm.at[:, cols], stage_ref.at[1], sem.at[1]))

    def tail_copies():
        cols = pl.ds(n_full * tn, rem)
        return (pltpu.make_async_copy(wg_hbm.at[:, cols], stage_ref.at[0, :, :rem], sem.at[0]),
                pltpu.make_async_copy(wu_hbm.at[:, cols], stage_ref.at[1, :, :rem], sem.at[1]))

    def start(jj):
        @pl.when(jj < n_full)
        def _():
            for c in full_copies(jj):
                c.start()

        if rem:
            @pl.when(jj == n_full)
            def _():
                for c in tail_copies():
                    c.start()

    @pl.when(pl.program_id(1) == 0)
    def _():
        @pl.when(j == 0)
        def _():
            start(j)

        @pl.when(j < n_full)
        def _():
            for c in full_copies(j):
                c.wait()
            wgb_ref[...] = stage_ref[0].astype(BF16)
            wub_ref[...] = stage_ref[1].astype(BF16)

        if rem:
            @pl.when(j == n_full)
            def _():
                for c in tail_copies():
                    c.wait()
                wgb_ref[:, :rem] = stage_ref[0, :, :rem].astype(BF16)
                wub_ref[:, :rem] = stage_ref[1, :, :rem].astype(BF16)
                wgb_ref[:, rem:] = jnp.zeros((wgb_ref.shape[0], tn - rem), BF16)
                wub_ref[:, rem:] = jnp.zeros((wub_ref.shape[0], tn - rem), BF16)

        start(j + 1)

    x = x_ref[...]
    g = _dot(x, wgb_ref[...])
    u = _dot(x, wub_ref[...])
    o_ref[...] = (g * (0.5 * jnp.tanh(0.5 * g) + 0.5) * u).astype(o_ref.dtype)


def _swiglu_up(h, wg, wu, n_out):
    m, kd = h.shape
    n = wg.shape[1]
    tm = _pick(m, (1024, 512, 256, 128))
    tn = 512
    n_full, rem = divmod(n, tn)
    assert rem % LANES == 0 and n_out == _round_up(n, tn)
    return pl.pallas_call(
        functools.partial(_swiglu_kernel, n_full=n_full, rem=rem, tn=tn),
        grid=(n_out // tn, m // tm),
        in_specs=[pl.BlockSpec((tm, kd), lambda j, i: (i, 0)),
                  pl.BlockSpec(memory_space=pl.ANY),
                  pl.BlockSpec(memory_space=pl.ANY)],
        out_specs=pl.BlockSpec((tm, tn), lambda j, i: (i, j)),
        out_shape=jax.ShapeDtypeStruct((m, n_out), BF16),
        scratch_shapes=[pltpu.VMEM((2, kd, tn), F32),
                        pltpu.VMEM((kd, tn), BF16), pltpu.VMEM((kd, tn), BF16),
                        pltpu.SemaphoreType.DMA((2,))],
        compiler_params=_params("arbitrary", "arbitrary"),
        name="swiglu_up",
    )(h, wg, wu)


def _resid_kernel(a_ref, w_ref, x_ref, gt_ref, o_ref, acc_ref, *, nk, scale):
    k = pl.program_id(2)

    @pl.when(k == 0)
    def _():
        acc_ref[...] = _dot(a_ref[...], w_ref[...])

    @pl.when((k > 0) & (k < nk - 1))
    def _():
        acc_ref[...] += _dot(a_ref[...], w_ref[...])

    @pl.when(k == nk - 1)
    def _():
        o_ref[...] = x_ref[...] + (scale * gt_ref[0]) * (acc_ref[...] + _dot(a_ref[...], w_ref[...]))


def _matmul_residual(a, w, x2, gate, seq, scale):
    m, kd = a.shape
    n = w.shape[1]
    tm = _pick(seq, (1024, 512, 256, 128))
    tn = _pick(n, (1024, 512, 256, 128))
    tk = _pick(kd, (2816, 1024, 512, 256, 128))
    nk = kd // tk
    assert nk >= 2
    return pl.pallas_call(
        functools.partial(_resid_kernel, nk=nk, scale=scale),
        grid=(m // tm, n // tn, nk),
        in_specs=[pl.BlockSpec((tm, tk), lambda i, j, k: (i, k)),
                  pl.BlockSpec((tk, tn), lambda i, j, k: (k, j)),
                  pl.BlockSpec((tm, tn), lambda i, j, k: (i, j)),
                  pl.BlockSpec((1, 1, tn), lambda i, j, k: (i * tm // seq, 0, j))],
        out_specs=pl.BlockSpec((tm, tn), lambda i, j, k: (i, j)),
        out_shape=jax.ShapeDtypeStruct((m, n), F32),
        scratch_shapes=[pltpu.VMEM((tm, tn), F32)],
        compiler_params=_params("parallel", "parallel", "arbitrary"),
        name="matmul_residual",
    )(a, w, x2, gate)


def _resid2_kernel(a1_ref, a2_ref, w1_ref, w2_ref, x_ref, gt_ref, o_ref, *, scale):
    y = _dot(a1_ref[...], w1_ref[...]) + _dot(a2_ref[...], w2_ref[...])
    o_ref[...] = x_ref[...] + (scale * gt_ref[0]) * y


def _matmul_residual2(a1, a2, w, x2, gate, seq, scale):
    m, k1 = a1.shape
    k2 = a2.shape[1]
    n = w.shape[1]
    assert k1 == k2
    tm = _pick(seq, (1024, 512, 256, 128))
    tn = _pick(n, (512, 256, 128))
    return pl.pallas_call(
        functools.partial(_resid2_kernel, scale=scale),
        grid=(m // tm, n // tn),
        in_specs=[pl.BlockSpec((tm, k1), lambda i, j: (i, 0)),
                  pl.BlockSpec((tm, k2), lambda i, j: (i, 0)),
                  pl.BlockSpec((k1, tn), lambda i, j: (0, j)),
                  pl.BlockSpec((k2, tn), lambda i, j: (1, j)),
                  pl.BlockSpec((tm, tn), lambda i, j: (i, j)),
                  pl.BlockSpec((1, 1, tn), lambda i, j: (i * tm // seq, 0, j))],
        out_specs=pl.BlockSpec((tm, tn), lambda i, j: (i, j)),
        out_shape=jax.ShapeDtypeStruct((m, n), F32),
        compiler_params=_params("parallel", "parallel"),
        name="out_proj_residual",
    )(a1, a2, w, w, x2, gate)


ROW_TILE = 512


def _lane_head0(shape):
    return lax.broadcasted_iota(jnp.int32, shape, 1) < RWKV_HEAD


def _stack(x):
    m0 = _lane_head0(x.shape)
    zero = jnp.zeros_like(x)
    return jnp.concatenate([jnp.where(m0, x, zero), jnp.where(m0, zero, x)], axis=0)


def _split_dot(lhs_bf16, x):
    hi = x.astype(BF16)
    lo = (x - hi.astype(F32)).astype(BF16)
    return _dot(lhs_bf16, hi) + _dot(lhs_bf16, lo)


def _lora_in_kernel(ps_ref, prev_ref, mu_ref, o_ref, *, lora_w, lora_a):
    x = ps_ref[0]
    prev_row = jnp.where(pl.program_id(1) == 0, 0.0, prev_ref[0][7:8, :])
    row = lax.broadcasted_iota(jnp.int32, x.shape, 0)
    prev = jnp.where(row == 0, prev_row, pltpu.roll(x, 1, 0))
    sm = x + (prev - x) * mu_ref[...]
    o_ref[0, :, :lora_w] = jnp.tanh(sm[:, :lora_w]).astype(o_ref.dtype)
    o_ref[0, :, lora_w:lora_w + lora_a] = sm[:, lora_w:lora_w + lora_a].astype(o_ref.dtype)
    o_ref[0, :, lora_w + lora_a:] = jax.nn.sigmoid(sm[:, lora_w + lora_a:]).astype(o_ref.dtype)


def _lora_in(p_sm, mu_sm, lora_w, lora_a, bsz, seq):
    wsm = mu_sm.shape[-1]
    ps = p_sm.reshape(bsz, seq, p_sm.shape[-1])
    ts = _pick(seq, (ROW_TILE, 128, 64, 8))
    return pl.pallas_call(
        functools.partial(_lora_in_kernel, lora_w=lora_w, lora_a=lora_a),
        grid=(bsz, seq // ts),
        in_specs=[pl.BlockSpec((1, ts, wsm), lambda b, i: (b, i, 0)),
                  pl.BlockSpec((1, 8, wsm), lambda b, i: (b, jnp.maximum(i * (ts // 8) - 1, 0), 0)),
                  pl.BlockSpec((1, wsm), lambda b, i: (0, 0))],
        out_specs=pl.BlockSpec((1, ts, wsm), lambda b, i: (b, i, 0)),
        out_shape=jax.ShapeDtypeStruct((bsz, seq, wsm), BF16),
        compiler_params=_params("parallel", "parallel"),
        name="rwkv_lora_in",
    )(ps, ps, mu_sm)


(VEC_MU_R, VEC_MU_K, VEC_MU_V, VEC_W0, VEC_A0, VEC_KK, VEC_KA, VEC_RK, VEC_LNW,
 VEC_LNB) = range(10)
VEC_ROWS = 16


def _rwkv_kernel(pr_ref, pk_ref, pv_ref, ps_ref, vec_ref, lora_ref, o_ref,
                 at_s, rt_s, kt_s, bt_s, kg_s, bg_s, v_s, g_s, bonus_s, gc_s,
                 m_s, n_s, q_s, yb_s, sb_s, *, seq, lora_w, lora_a):
    n_tiles = seq // ROW_TILE
    n_chunks = seq // CHUNK
    cpt = ROW_TILE // CHUNK

    def vec(idx):
        return vec_ref[idx:idx + 1, :]

    ri = lax.broadcasted_iota(jnp.int32, (LANES, LANES), 0)
    ci = lax.broadcasted_iota(jnp.int32, (LANES, LANES), 1)
    bd_mask = (ri // RWKV_HEAD) == (ci // RWKV_HEAD)
    bd_ones = bd_mask.astype(BF16)
    tr = lax.broadcasted_iota(jnp.int32, (ROW_TILE, ROW_TILE), 0)
    tc = lax.broadcasted_iota(jnp.int32, (ROW_TILE, ROW_TILE), 1)
    tri = ((tc <= tr) & ((tr // CHUNK) == (tc // CHUNK))).astype(BF16)

    def head_sum(x):
        return _dot(x.astype(BF16), bd_ones)

    def shift_mix(ref, t0, mu):
        x = ref[0, pl.ds(t0, ROW_TILE), :]
        p0 = pl.multiple_of(jnp.maximum(t0 - 8, 0), 8)
        prev8 = ref[0, pl.ds(p0, 8), :]
        prev_row = jnp.where(t0 == 0, 0.0, prev8[7:8, :])
        row = lax.broadcasted_iota(jnp.int32, x.shape, 0)
        prev = jnp.where(row == 0, prev_row, pltpu.roll(x, 1, 0))
        return x + (prev - x) * mu

    def phase_a(i, carry):
        t0 = pl.multiple_of(i * ROW_TILE, ROW_TILE)
        r = shift_mix(pr_ref, t0, vec(VEC_MU_R))
        k = shift_mix(pk_ref, t0, vec(VEC_MU_K))
        v = shift_mix(pv_ref, t0, vec(VEC_MU_V))
        rows = pl.ds(t0, ROW_TILE)
        wlin = vec(VEC_W0) + _dot(ps_ref[0, rows, :lora_w], lora_ref[:lora_w, :])
        wlog = -(jnp.maximum(-wlin, 0.0) + jnp.log(1.0 + jnp.exp(-jnp.abs(wlin)))) - 0.5
        ld = -jnp.exp(wlog)
        alr = jax.nn.sigmoid(vec(VEC_A0) + _dot(ps_ref[0, rows, lora_w:lora_w + lora_a],
                                                lora_ref[lora_w:lora_w + lora_a, :]))
        g = _dot(ps_ref[0, rows, lora_w + lora_a:], lora_ref[lora_w + lora_a:, :])
        kk = k * vec(VEC_KK)
        kk = kk * lax.rsqrt(jnp.maximum(head_sum(kk * kk), 1e-24))
        k = k * (1.0 + (alr - 1.0) * vec(VEC_KA))
        bonus = head_sum(r * k * vec(VEC_RK)) * v
        b = kk * alr
        cum = _split_dot(tri, ld)
        g_dec = jnp.exp(cum)
        g_inv = jnp.exp(-cum)
        at_s[pl.ds(t0, ROW_TILE), :] = (-kk * jnp.exp(cum - ld)).astype(BF16)
        rt_s[pl.ds(t0, ROW_TILE), :] = (r * g_dec).astype(BF16)
        kt_s[pl.ds(t0, ROW_TILE), :] = (k * g_inv).astype(BF16)
        bt_s[pl.ds(t0, ROW_TILE), :] = (b * g_inv).astype(BF16)
        v_s[pl.ds(t0, ROW_TILE), :] = v.astype(BF16)
        g_s[pl.ds(t0, ROW_TILE), :] = g
        bonus_s[pl.ds(t0, ROW_TILE), :] = bonus
        for c in range(cpt):
            lo, hi = c * CHUNK, (c + 1) * CHUNK
            cum_c = cum[hi - 1:hi, :]
            rem = jnp.exp(cum_c - cum[lo:hi, :])
            kg_s[pl.ds(t0 + lo, CHUNK), :] = (k[lo:hi, :] * rem).astype(BF16)
            bg_s[pl.ds(t0 + lo, CHUNK), :] = (b[lo:hi, :] * rem).astype(BF16)
            gc_s[i * cpt + c] = jnp.broadcast_to(jnp.exp(cum_c), (8, LANES))
        return carry

    lax.fori_loop(0, n_tiles, phase_a, 0)

    row = lax.broadcasted_iota(jnp.int32, (CHUNK, LANES), 0)
    col = lax.broadcasted_iota(jnp.int32, (CHUNK, LANES), 1) % RWKV_HEAD
    strict = col < row
    incl = col <= row
    eye_p = (col == row).astype(F32)

    def same_block(size):
        return (row // size) == (col // size)

    def each(fn, *lists):
        return [fn(*xs) for xs in zip(*lists)]

    def phase_b(i, tick):
        cs = [i * GROUP + g for g in range(GROUP)]
        sls = [pl.ds(pl.multiple_of(c * CHUNK, CHUNK), CHUNK) for c in cs]
        at = [at_s[sl, :] for sl in sls]
        rt = [rt_s[sl, :] for sl in sls]
        kt = [kt_s[sl, :] for sl in sls]
        bt = [bt_s[sl, :] for sl in sls]
        v = [v_s[sl, :] for sl in sls]
        sc = each(lambda a, r, k, b: _dot_nt(jnp.concatenate([a, r], axis=0),
                                             jnp.concatenate([_stack(k), _stack(b)], axis=0)),
                  at, rt, kt, bt)
        a_ak = [jnp.where(strict, x[:CHUNK, :LANES], 0.0).astype(BF16) for x in sc]
        a_ab = [jnp.where(strict, x[:CHUNK, LANES:], 0.0) for x in sc]
        a_rk = [jnp.where(incl, x[CHUNK:, :LANES], 0.0).astype(BF16) for x in sc]
        a_rb = [jnp.where(incl, x[CHUNK:, LANES:], 0.0).astype(BF16) for x in sc]
        tick()
        d = [jnp.where(same_block(INV_BASE), x, 0.0) for x in a_ab]
        db = [x.astype(BF16) for x in d]
        d2 = [_dot(x, _stack(x)).astype(BF16) for x in db]
        tick()
        p = [eye_p + x for x in d]
        p = each(lambda pp, x2: pp + _dot(pp.astype(BF16), _stack(x2)), p, d2)
        tick()
        s = INV_BASE
        while s < CHUNK:
            lower_left = same_block(2 * s) & ((row // s) % 2 == 1) & ((col // s) % 2 == 0)
            a21 = [jnp.where(lower_left, x, 0.0).astype(BF16) for x in a_ab]
            pb = [x.astype(BF16) for x in p]
            a21_t11 = each(lambda a, t: _dot(a, _stack(t)).astype(BF16), a21, pb)
            tick()
            p = each(lambda pp, t, y: pp + _dot(t, _stack(y)), p, pb, a21_t11)
            tick()
            s *= 2
        tinv = [x.astype(BF16) for x in p]
        vst = [_stack(x) for x in v]
        x1 = each(lambda a, vs: _dot(a, vs).astype(BF16), a_ak, vst)
        tick()
        wu = each(lambda t, a, x: _dot(t, jnp.concatenate([_stack(a), _stack(x)], axis=1)),
                  tinv, at, x1)
        tick()
        wb = [x[:, :LANES].astype(BF16) for x in wu]
        ub = [x[:, LANES:].astype(BF16) for x in wu]
        qy = each(lambda a, w, u: _dot(a, jnp.concatenate([_stack(w), _stack(u)], axis=1)),
                  a_rb, wb, ub)
        tick()
        yb = each(lambda a, vs, x: _dot(a, vs) + x[:, LANES:], a_rk, vst, qy)
        tick()
        for g in range(GROUP):
            q_s[sls[g], :] = (rt[g].astype(F32) + qy[g][:, :LANES]).astype(BF16)
            yb_s[sls[g], :] = yb[g]
        kg = [kg_s[sl, :] for sl in sls]
        bg = [bg_s[sl, :] for sl in sls]
        mm = each(_dot_tn, wb, bg)
        tick()
        nn = each(lambda vv, u, k, b: _dot_tn(jnp.concatenate([vv, u], axis=0),
                                              jnp.concatenate([k, b], axis=0)), v, ub, kg, bg)
        tick()
        for g in range(GROUP):
            m_s[cs[g]] = jnp.where(bd_mask, mm[g], 0.0).astype(BF16)
            n_s[cs[g]] = jnp.where(bd_mask, nn[g], 0.0)

    def c1_step(c, s):
        sb = s.astype(BF16)
        sb_s[c] = sb
        return s * gc_s[c][0:1, :] + _dot(sb, m_s[c]) + n_s[c]

    def group_with_c1(i, s):
        state = [s]
        pending = [(i - 1) * GROUP + g for g in range(GROUP)]

        def tick():
            if pending:
                state[0] = c1_step(pending.pop(0), state[0])

        phase_b(i, tick)
        while pending:
            tick()
        return state[0]

    n_groups = n_chunks // GROUP
    phase_b(0, lambda: None)
    s_last = lax.fori_loop(1, n_groups, group_with_c1, jnp.zeros((LANES, LANES), F32))

    inv_n = 1.0 / RWKV_HEAD

    def phase_c2(i, tick):
        cs = [i * GROUP + g for g in range(GROUP)]
        sls = [pl.ds(c * CHUNK, CHUNK) for c in cs]

        def staged(fn, *lists):
            out = []
            for g, xs in enumerate(zip(*lists)):
                out.append(fn(*xs))
                if g % 4 == 3:
                    tick()
            return out

        y = staged(lambda sl, c: _dot_nt(q_s[sl, :], sb_s[c]) + yb_s[sl, :], sls, cs)
        mean = staged(lambda x: head_sum(x) * inv_n, y)
        yc = each(lambda a, b: a - b, y, mean)
        var = staged(lambda x: head_sum(x * x) * inv_n, yc)
        for g in range(GROUP):
            yn = yc[g] * lax.rsqrt(var[g] + LN_X_EPS) * vec(VEC_LNW) + vec(VEC_LNB)
            o_ref[0, sls[g], :] = ((yn + bonus_s[sls[g], :]) * g_s[sls[g], :]).astype(o_ref.dtype)

    state = [s_last]
    pending = [(n_groups - 1) * GROUP + g for g in range(GROUP)]

    def tail_tick():
        if pending:
            state[0] = c1_step(pending.pop(0), state[0])

    for i in range(n_groups - 1):
        phase_c2(i, tail_tick)
    while pending:
        tail_tick()
    phase_c2(n_groups - 1, lambda: None)


def _rwkv(p_rkv, p_sm, mu_rkv, mu_sm, w0, a0, k_k, k_a, r_k, ln_w, ln_b,
          w2p, a2p, g2p, bsz, seq):
    rw = w0.shape[-1]
    npair = rw // LANES
    lora_w, lora_a, lora_g = w2p.shape[0], a2p.shape[0], g2p.shape[0]
    wsm = lora_w + lora_a + lora_g
    p3 = p_rkv.reshape(bsz, seq, 3 * rw)
    ps = _lora_in(p_sm, mu_sm, lora_w, lora_a, bsz, seq)
    n_chunks = seq // CHUNK

    def col(off):
        return pl.BlockSpec((1, seq, LANES), lambda b, h: (b, 0, off + h))

    rows = [mu_rkv[0, :rw], mu_rkv[0, rw:2 * rw], mu_rkv[0, 2 * rw:],
            w0, a0, k_k, k_a, r_k.reshape(rw), ln_w, ln_b]
    vecs = jnp.concatenate([jnp.stack(rows), jnp.zeros((VEC_ROWS - len(rows), rw), F32)], axis=0)
    lora = jnp.concatenate([w2p, a2p, g2p], axis=0)

    return pl.pallas_call(
        functools.partial(_rwkv_kernel, seq=seq, lora_w=lora_w, lora_a=lora_a),
        grid=(bsz, npair),
        in_specs=[col(0), col(npair), col(2 * npair),
                  pl.BlockSpec((1, seq, wsm), lambda b, h: (b, 0, 0)),
                  pl.BlockSpec((VEC_ROWS, LANES), lambda b, h: (0, h)),
                  pl.BlockSpec((wsm, LANES), lambda b, h: (0, h))],
        out_specs=pl.BlockSpec((1, seq, LANES), lambda b, h: (b, 0, h)),
        out_shape=jax.ShapeDtypeStruct((bsz, seq, rw), BF16),
        scratch_shapes=(
            [pltpu.VMEM((seq, LANES), BF16) for _ in range(7)]
            + [pltpu.VMEM((seq, LANES), F32) for _ in range(2)]
            + [pltpu.VMEM((n_chunks, 8, LANES), F32),
               pltpu.VMEM((n_chunks, LANES, LANES), BF16),
               pltpu.VMEM((n_chunks, LANES, LANES), F32),
               pltpu.VMEM((seq, LANES), BF16),
               pltpu.VMEM((seq, LANES), F32),
               pltpu.VMEM((n_chunks, LANES, LANES), BF16)]),
        compiler_params=_params("parallel", "parallel"),
        name="rwkv7",
    )(p3, p3, p3, ps, vecs, lora)


def _fgate_kernel(f_ref, bf_ref, col_ref, row_ref, *, seq):
    nblk = seq // LANES
    ri = lax.broadcasted_iota(jnp.int32, (LANES, LANES), 0)
    ci = lax.broadcasted_iota(jnp.int32, (LANES, LANES), 1)
    tri = (ci <= ri).astype(BF16)

    def body(i, carry):
        t0 = pl.multiple_of(i * LANES, LANES)
        z = f_ref[0, pl.ds(t0, LANES), :] + bf_ref[...]
        logf = jnp.minimum(z, 0.0) - jnp.log1p(jnp.exp(-jnp.abs(z)))
        hi = logf.astype(BF16)
        r1 = logf - hi.astype(F32)
        mid = r1.astype(BF16)
        lo = (r1 - mid.astype(F32)).astype(BF16)
        cum = carry + (_dot(tri, hi) + _dot(tri, mid) + _dot(tri, lo))
        col_ref[0, pl.ds(t0, LANES), :] = cum
        row_ref[0, :, pl.ds(t0, LANES)] = cum.T
        return cum[LANES - 1:LANES, :]

    lax.fori_loop(0, nblk, body, jnp.zeros((1, LANES), F32))


def _fgate(p_sm, b_f_pad, off, bsz, seq):
    ps = p_sm.reshape(bsz, seq, p_sm.shape[-1])
    return pl.pallas_call(
        functools.partial(_fgate_kernel, seq=seq),
        grid=(bsz,),
        in_specs=[pl.BlockSpec((1, seq, LANES), lambda b: (b, 0, off // LANES)),
                  pl.BlockSpec((1, LANES), lambda b: (0, 0))],
        out_specs=[pl.BlockSpec((1, seq, LANES), lambda b: (b, 0, 0)),
                   pl.BlockSpec((1, LANES, seq), lambda b: (b, 0, 0))],
        out_shape=[jax.ShapeDtypeStruct((bsz, seq, LANES), F32),
                   jax.ShapeDtypeStruct((bsz, LANES, seq), F32)],
        compiler_params=_params("parallel"),
        name="fox_gates",
    )(ps, b_f_pad)


FOX_HEADS_PER_STEP = 2
LOG2E = 1.4426950408889634


def _fox_kernel(q_ref, k_ref, v_ref, ccol_ref, crow_ref, on_ref, o_ref, *, tq, seq, scale):
    hg = pl.program_id(1)
    heads = range(FOX_HEADS_PER_STEP)
    hsl = [slice(g * FOX_HEAD, (g + 1) * FOX_HEAD) for g in heads]
    lane = lax.broadcasted_iota(jnp.int32, (tq, LANES), 1)
    causal = (lax.broadcasted_iota(jnp.int32, (tq, tq), 1)
              <= lax.broadcasted_iota(jnp.int32, (tq, tq), 0))

    for i in range(seq // tq):
        qrows = pl.ds(i * tq, tq)
        q = [q_ref[0, qrows, sl] for sl in hsl]
        ccol = ccol_ref[0, qrows, :]
        cq = [LOG2E * jnp.sum(jnp.where(lane == hg * FOX_HEADS_PER_STEP + g, ccol, 0.0),
                              axis=-1, keepdims=True) for g in heads]

        def block(j, carry, diagonal, q=q, cq=cq):
            m, l, acc = carry
            k0 = pl.multiple_of(j * tq, tq)
            kb = [k_ref[0, pl.ds(k0, tq), sl] for sl in hsl]
            vb = [v_ref[0, pl.ds(k0, tq), sl] for sl in hsl]
            ck = [LOG2E * crow_ref[0, g, :, pl.ds(k0, tq)] for g in heads]
            s = [_dot_nt(q[g], kb[g]) * (scale * LOG2E) + cq[g] - ck[g] for g in heads]
            if diagonal:
                s = [jnp.where(causal, x, -jnp.inf) for x in s]
            m_new = [jnp.maximum(m[g], jnp.max(s[g], axis=-1, keepdims=True)) for g in heads]
            alpha = [jnp.exp2(m[g] - m_new[g]) for g in heads]
            p = [jnp.exp2(s[g] - m_new[g]) for g in heads]
            l = [alpha[g] * l[g] + jnp.sum(p[g], axis=-1, keepdims=True) for g in heads]
            acc = [alpha[g] * acc[g] + _dot(p[g].astype(BF16), vb[g]) for g in heads]
            return tuple(m_new), tuple(l), tuple(acc)

        init = (tuple(jnp.full((tq, 1), -jnp.inf, F32) for _ in heads),
                tuple(jnp.zeros((tq, 1), F32) for _ in heads),
                tuple(jnp.zeros((tq, FOX_HEAD), F32) for _ in heads))
        carry = lax.fori_loop(0, i, lambda j, c, block=block: block(j, c, False), init)
        _, l, acc = block(i, carry, True)
        for g in heads:
            o = acc[g] / l[g]
            o = o * lax.rsqrt(jnp.mean(o * o, axis=-1, keepdims=True) + NORM_EPS) * on_ref[g]
            o_ref[0, qrows, hsl[g]] = o.astype(o_ref.dtype)


def _fox(p_fox, ccol, crow, out_norm, bsz, seq):
    fw = p_fox.shape[-1] // 3
    nh = fw // FOX_HEAD
    hps = FOX_HEADS_PER_STEP
    ng = nh // hps
    wid = hps * FOX_HEAD
    tq = _pick(seq, (512, 256, 128))
    p3 = p_fox.reshape(bsz, seq, 3 * fw)
    crow4 = crow.reshape(bsz, LANES, 1, seq)
    on = out_norm.reshape(nh, 1, FOX_HEAD)
    return pl.pallas_call(
        functools.partial(_fox_kernel, tq=tq, seq=seq, scale=FOX_HEAD ** -0.5),
        grid=(bsz, ng),
        in_specs=[pl.BlockSpec((1, seq, wid), lambda b, h: (b, 0, h)),
                  pl.BlockSpec((1, seq, wid), lambda b, h: (b, 0, ng + h)),
                  pl.BlockSpec((1, seq, wid), lambda b, h: (b, 0, 2 * ng + h)),
                  pl.BlockSpec((1, seq, LANES), lambda b, h: (b, 0, 0)),
                  pl.BlockSpec((1, hps, 1, seq), lambda b, h: (b, h, 0, 0)),
                  pl.BlockSpec((hps, 1, FOX_HEAD), lambda b, h: (h, 0, 0))],
        out_specs=pl.BlockSpec((1, seq, wid), lambda b, h: (b, 0, h)),
        out_shape=jax.ShapeDtypeStruct((bsz, seq, fw), BF16),
        compiler_params=_params("parallel", "parallel"),
        name="fox_attention",
    )(p3, p3, p3, ccol, crow4, on)


def _pad_rows(w, rows):
    return jnp.pad(w, ((0, rows - w.shape[0]), (0, 0)))


def _pad_cols(w, cols):
    return jnp.pad(w, ((0, 0), (0, cols - w.shape[1])))


def _ffn(x2, h, w_gate, w_up, w_down, gate, seq):
    dff = w_gate.shape[1]
    dffp = _round_up(dff, 512)
    wd = jnp.concatenate(
        [w_down.astype(BF16), jnp.zeros((dffp - dff, w_down.shape[1]), BF16)], axis=0)
    act = _swiglu_up(h, w_gate, w_up, dffp)
    return _matmul_residual(act, wd, x2, gate, seq, 0.5)


def kernel(x, c, w_mod, b_mod, norm_ffn1, ffn1_gate, ffn1_up, ffn1_down, norm_mix, w_in, rwkv_mu, rwkv_w0, rwkv_w2, rwkv_a0, rwkv_a2, rwkv_g2, rwkv_k_k, rwkv_k_a, rwkv_r_k, rwkv_ln_w, rwkv_ln_b, fox_b_f, fox_out_norm, w_out, norm_ffn2, ffn2_gate, ffn2_up, ffn2_down, norm_final):
    bsz, seq, d = x.shape
    depth = w_mod.shape[0]
    rw = rwkv_w0.shape[-1]
    fw = fox_out_norm.shape[-1]
    nfh = fox_b_f.shape[-1]
    dl, al, gl = rwkv_w2.shape[1], rwkv_a2.shape[1], rwkv_g2.shape[1]
    dlp, alp, glp = (_round_up(n, LANES) for n in (dl, al, gl))
    c3 = 3 * rw
    c4, c5, c6 = c3 + dl, c3 + dl + al, c3 + dl + al + gl

    x2 = x.reshape(bsz * seq, d)
    for l in range(depth):
        mod = _mod(c, w_mod[l], b_mod[l])
        sh1, sc1, gt1, sh2, sc2, gt2, sh3, sc3, gt3 = (
            m[:, None, :] for m in jnp.split(mod, 9, axis=-1))

        h = _norm_mod(x2, norm_ffn1[l], sc1, sh1, seq)
        x2 = _ffn(x2, h, ffn1_gate[l], ffn1_up[l], ffn1_down[l], gt1, seq)

        h = _norm_mod(x2, norm_mix[l], sc2, sh2, seq)
        wi, mu = w_in[l], rwkv_mu[l]
        w_rkv = wi[:, :c3].astype(BF16)
        w_sm = jnp.concatenate(
            [_pad_cols(wi[:, c3:c4], dlp), _pad_cols(wi[:, c4:c5], alp),
             _pad_cols(wi[:, c5:c6], glp), _pad_cols(wi[:, c6 + 3 * fw:], LANES)],
            axis=1).astype(BF16)
        w_fox = wi[:, c6:c6 + 3 * fw].astype(BF16)
        mu_rkv = mu[:c3].reshape(1, c3)
        mu_sm = jnp.concatenate(
            [jnp.pad(mu[c3:c4], (0, dlp - dl)), jnp.pad(mu[c4:c5], (0, alp - al)),
             jnp.pad(mu[c5:c6], (0, glp - gl))]).reshape(1, dlp + alp + glp)
        p_rkv = _matmul(h, w_rkv, F32)
        p_sm = _matmul(h, w_sm, F32)
        p_fox = _matmul(h, w_fox, BF16)

        y_r = _rwkv(p_rkv, p_sm, mu_rkv, mu_sm,
                    rwkv_w0[l], rwkv_a0[l], rwkv_k_k[l], rwkv_k_a[l], rwkv_r_k[l],
                    rwkv_ln_w[l], rwkv_ln_b[l],
                    _pad_rows(rwkv_w2[l], dlp).astype(BF16),
                    _pad_rows(rwkv_a2[l], alp).astype(BF16),
                    _pad_rows(rwkv_g2[l], glp).astype(BF16), bsz, seq)

        b_f_pad = jnp.pad(fox_b_f[l], (0, LANES - nfh)).reshape(1, LANES)
        ccol, crow = _fgate(p_sm, b_f_pad, dlp + alp + glp, bsz, seq)
        y_f = _fox(p_fox, ccol, crow, fox_out_norm[l], bsz, seq)

        x2 = _matmul_residual2(y_r.reshape(bsz * seq, rw), y_f.reshape(bsz * seq, fw),
                               w_out[l].astype(BF16), x2, gt2, seq, 1.0)

        h = _norm_mod(x2, norm_ffn2[l], sc3, sh3, seq)
        x2 = _ffn(x2, h, ffn2_gate[l], ffn2_up[l], ffn2_down[l], gt3, seq)
    return _final_norm(x2, norm_final).reshape(bsz, seq, d)
```

```python
import functools

import jax
import jax.numpy as jnp
from jax import lax
from jax.experimental import pallas as pl
from jax.experimental.pallas import tpu as pltpu

F32 = jnp.float32
BF16 = jnp.bfloat16

LANES = 128
RWKV_HEAD = 64
FOX_HEAD = 128
CHUNK = 64
INV_BASE = 4
GROUP = 16
NORM_EPS = 1e-6
LN_X_EPS = 64e-5
VMEM_LIMIT = 56 * 1024 * 1024


def _round_up(n, m):
    return (n + m - 1) // m * m


def _pick(n, candidates):
    for c in candidates:
        if n % c == 0:
            return c
    return n


def _dot(a, b):
    return jnp.dot(a, b, preferred_element_type=F32)


def _dot_nt(a, b):
    return lax.dot_general(a, b, (((1,), (1,)), ((), ())), preferred_element_type=F32)


def _dot_tn(a, b):
    return lax.dot_general(a, b, (((0,), (0,)), ((), ())), preferred_element_type=F32)


def _params(*sem):
    return pltpu.CompilerParams(dimension_semantics=sem, vmem_limit_bytes=VMEM_LIMIT)


def _mod_kernel(c_ref, w_ref, b_ref, o_ref):
    c = c_ref[...]
    a = (c * jax.nn.sigmoid(c)).astype(BF16)
    o_ref[...] = _dot(a, w_ref[...].astype(BF16)) + b_ref[...]


def _mod(c, w_mod, b_mod):
    bsz, d = c.shape
    n = w_mod.shape[1]
    rows = _round_up(bsz, 16)
    c_pad = jnp.pad(c, ((0, rows - bsz), (0, 0)))
    tn = _pick(n, (512, 256, 128))
    out = pl.pallas_call(
        _mod_kernel,
        grid=(n // tn,),
        in_specs=[pl.BlockSpec((rows, d), lambda j: (0, 0)),
                  pl.BlockSpec((d, tn), lambda j: (0, j)),
                  pl.BlockSpec((1, tn), lambda j: (0, j))],
        out_specs=pl.BlockSpec((rows, tn), lambda j: (0, j)),
        out_shape=jax.ShapeDtypeStruct((rows, n), F32),
        compiler_params=_params("parallel"),
        name="mod_matmul",
    )(c_pad, w_mod, b_mod.reshape(1, n))
    return out[:bsz]


def _norm_kernel(x_ref, g_ref, sc_ref, sh_ref, o_ref):
    x = x_ref[...]
    y = x * lax.rsqrt(jnp.mean(x * x, axis=-1, keepdims=True) + NORM_EPS) * g_ref[...]
    o_ref[...] = (y * (1.0 + sc_ref[0]) + sh_ref[0]).astype(o_ref.dtype)


def _final_norm_kernel(x_ref, g_ref, o_ref):
    x = x_ref[...]
    o_ref[...] = x * lax.rsqrt(jnp.mean(x * x, axis=-1, keepdims=True) + NORM_EPS) * g_ref[...]


def _norm_mod(x2, g, sc, sh, seq):
    t, d = x2.shape
    ts = _pick(seq, (512, 256, 128, 64, 8))
    return pl.pallas_call(
        _norm_kernel,
        grid=(t // ts,),
        in_specs=[pl.BlockSpec((ts, d), lambda i: (i, 0)),
                  pl.BlockSpec((1, d), lambda i: (0, 0)),
                  pl.BlockSpec((1, 1, d), lambda i: (i * ts // seq, 0, 0)),
                  pl.BlockSpec((1, 1, d), lambda i: (i * ts // seq, 0, 0))],
        out_specs=pl.BlockSpec((ts, d), lambda i: (i, 0)),
        out_shape=jax.ShapeDtypeStruct((t, d), BF16),
        compiler_params=_params("parallel"),
        name="norm_mod",
    )(x2, g.reshape(1, d), sc, sh)


def _final_norm(x2, g):
    t, d = x2.shape
    ts = _pick(t, (512, 256, 128, 64, 8))
    return pl.pallas_call(
        _final_norm_kernel,
        grid=(t // ts,),
        in_specs=[pl.BlockSpec((ts, d), lambda i: (i, 0)),
                  pl.BlockSpec((1, d), lambda i: (0, 0))],
        out_specs=pl.BlockSpec((ts, d), lambda i: (i, 0)),
        out_shape=jax.ShapeDtypeStruct((t, d), F32),
        compiler_params=_params("parallel"),
        name="final_norm",
    )(x2, g.reshape(1, d))


def _mm_nt_kernel(x_ref, wt_ref, o_ref):
    o_ref[...] = _dot_nt(x_ref[...], wt_ref[...]).astype(o_ref.dtype)


def _matmul_nt(x, wt, out_dtype, row0=0, n=None):
    m, kd = x.shape
    n = wt.shape[0] if n is None else n
    tm = _pick(m, (1024, 512, 256, 128))
    tn = _pick(n, (1024, 512, 640, 256, 128))
    return pl.pallas_call(
        _mm_nt_kernel,
        grid=(m // tm, n // tn),
        in_specs=[pl.BlockSpec((tm, kd), lambda i, j: (i, 0)),
                  pl.BlockSpec((pl.Element(tn), pl.Element(kd)),
                               lambda i, j: ((row0 // 16 + j * (tn // 16)) * 16, 0))],
        out_specs=pl.BlockSpec((tm, tn), lambda i, j: (i, j)),
        out_shape=jax.ShapeDtypeStruct((m, n), out_dtype),
        compiler_params=_params("parallel", "parallel"),
        name="matmul_nt",
    )(x, wt)


def _swiglu_kernel(x_ref, wg_hbm, wu_hbm, o_ref, stage_ref, wgb_ref, wub_ref, sem,
                   *, n_full, rem, tn):
    j = pl.program_id(0)

    def full_copies(jj):
        cols = pl.ds(pl.multiple_of(jj * tn, tn), tn)
        return (pltpu.make_async_copy(wg_hbm.at[:, cols], stage_ref.at[0], sem.at[0]),
                pltpu.make_async_copy(wu_hbm.at[:, cols], stage_ref.at[1], sem.at[1]))

    def tail_copies():
        cols = pl.ds(n_full * tn, rem)
        return (pltpu.make_async_copy(wg_hbm.at[:, cols], stage_ref.at[0, :, :rem], sem.at[0]),
                pltpu.make_async_copy(wu_hbm.at[:, cols], stage_ref.at[1, :, :rem], sem.at[1]))

    def start(jj):
        @pl.when(jj < n_full)
        def _():
            for c in full_copies(jj):
                c.start()

        if rem:
            @pl.when(jj == n_full)
            def _():
                for c in tail_copies():
                    c.start()

    @pl.when(pl.program_id(1) == 0)
    def _():
        @pl.when(j == 0)
        def _():
            start(j)

        @pl.when(j < n_full)
        def _():
            for c in full_copies(j):
                c.wait()
            wgb_ref[...] = stage_ref[0].astype(BF16)
            wub_ref[...] = stage_ref[1].astype(BF16)

        if rem:
            @pl.when(j == n_full)
            def _():
                for c in tail_copies():
                    c.wait()
                wgb_ref[:, :rem] = stage_ref[0, :, :rem].astype(BF16)
                wub_ref[:, :rem] = stage_ref[1, :, :rem].astype(BF16)
                wgb_ref[:, rem:] = jnp.zeros((wgb_ref.shape[0], tn - rem), BF16)
                wub_ref[:, rem:] = jnp.zeros((wub_ref.shape[0], tn - rem), BF16)

        start(j + 1)

    x = x_ref[...]
    g = _dot(x, wgb_ref[...])
    u = _dot(x, wub_ref[...])
    o_ref[...] = (g * (0.5 * jnp.tanh(0.5 * g) + 0.5) * u).astype(o_ref.dtype)


def _swiglu_up(h, wg, wu, n_out):
    m, kd = h.shape
    n = wg.shape[1]
    tm = _pick(m, (1024, 512, 256, 128))
    tn = 512
    n_full, rem = divmod(n, tn)
    assert rem % LANES == 0 and n_out == _round_up(n, tn)
    return pl.pallas_call(
        functools.partial(_swiglu_kernel, n_full=n_full, rem=rem, tn=tn),
        grid=(n_out // tn, m // tm),
        in_specs=[pl.BlockSpec((tm, kd), lambda j, i: (i, 0)),
                  pl.BlockSpec(memory_space=pl.ANY),
                  pl.BlockSpec(memory_space=pl.ANY)],
        out_specs=pl.BlockSpec((tm, tn), lambda j, i: (i, j)),
        out_shape=jax.ShapeDtypeStruct((m, n_out), BF16),
        scratch_shapes=[pltpu.VMEM((2, kd, tn), F32),
                        pltpu.VMEM((kd, tn), BF16), pltpu.VMEM((kd, tn), BF16),
                        pltpu.SemaphoreType.DMA((2,))],
        compiler_params=_params("arbitrary", "arbitrary"),
        name="swiglu_up",
    )(h, wg, wu)


def _resid_kernel(a_ref, w_ref, x_ref, gt_ref, o_ref, acc_ref, *, nk, scale):
    k = pl.program_id(2)

    @pl.when(k == 0)
    def _():
        acc_ref[...] = _dot(a_ref[...], w_ref[...])

    @pl.when((k > 0) & (k < nk - 1))
    def _():
        acc_ref[...] += _dot(a_ref[...], w_ref[...])

    @pl.when(k == nk - 1)
    def _():
        o_ref[...] = x_ref[...] + (scale * gt_ref[0]) * (acc_ref[...] + _dot(a_ref[...], w_ref[...]))


def _matmul_residual(a, w, x2, gate, seq, scale):
    m, kd = a.shape
    n = w.shape[1]
    tm = _pick(seq, (1024, 512, 256, 128))
    tn = _pick(n, (1024, 512, 256, 128))
    tk = _pick(kd, (2816, 1024, 512, 256, 128))
    nk = kd // tk
    assert nk >= 2
    return pl.pallas_call(
        functools.partial(_resid_kernel, nk=nk, scale=scale),
        grid=(m // tm, n // tn, nk),
        in_specs=[pl.BlockSpec((tm, tk), lambda i, j, k: (i, k)),
                  pl.BlockSpec((tk, tn), lambda i, j, k: (k, j)),
                  pl.BlockSpec((tm, tn), lambda i, j, k: (i, j)),
                  pl.BlockSpec((1, 1, tn), lambda i, j, k: (i * tm // seq, 0, j))],
        out_specs=pl.BlockSpec((tm, tn), lambda i, j, k: (i, j)),
        out_shape=jax.ShapeDtypeStruct((m, n), F32),
        scratch_shapes=[pltpu.VMEM((tm, tn), F32)],
        compiler_params=_params("parallel", "parallel", "arbitrary"),
        name="matmul_residual",
    )(a, w, x2, gate)


def _resid2_kernel(a1_ref, a2_ref, w1_ref, w2_ref, x_ref, gt_ref, o_ref, *, scale):
    y = _dot(a1_ref[...], w1_ref[...]) + _dot(a2_ref[...], w2_ref[...])
    o_ref[...] = x_ref[...] + (scale * gt_ref[0]) * y


def _matmul_residual2(a1, a2, w, x2, gate, seq, scale):
    m, k1 = a1.shape
    k2 = a2.shape[1]
    n = w.shape[1]
    assert k1 == k2
    tm = _pick(seq, (1024, 512, 256, 128))
    tn = _pick(n, (512, 256, 128))
    return pl.pallas_call(
        functools.partial(_resid2_kernel, scale=scale),
        grid=(m // tm, n // tn),
        in_specs=[pl.BlockSpec((tm, k1), lambda i, j: (i, 0)),
                  pl.BlockSpec((tm, k2), lambda i, j: (i, 0)),
                  pl.BlockSpec((k1, tn), lambda i, j: (0, j)),
                  pl.BlockSpec((k2, tn), lambda i, j: (1, j)),
                  pl.BlockSpec((tm, tn), lambda i, j: (i, j)),
                  pl.BlockSpec((1, 1, tn), lambda i, j: (i * tm // seq, 0, j))],
        out_specs=pl.BlockSpec((tm, tn), lambda i, j: (i, j)),
        out_shape=jax.ShapeDtypeStruct((m, n), F32),
        compiler_params=_params("parallel", "parallel"),
        name="out_proj_residual",
    )(a1, a2, w, w, x2, gate)


ROW_TILE = 512


def _lane_head0(shape):
    return lax.broadcasted_iota(jnp.int32, shape, 1) < RWKV_HEAD


def _stack(x):
    m0 = _lane_head0(x.shape)
    zero = jnp.zeros_like(x)
    return jnp.concatenate([jnp.where(m0, x, zero), jnp.where(m0, zero, x)], axis=0)


def _split_dot(lhs_bf16, x):
    hi = x.astype(BF16)
    lo = (x - hi.astype(F32)).astype(BF16)
    return _dot(lhs_bf16, hi) + _dot(lhs_bf16, lo)


def _lora_in_kernel(ps_ref, prev_ref, mu_ref, o_ref, *, lora_w, lora_a):
    x = ps_ref[0]
    prev_row = jnp.where(pl.program_id(1) == 0, 0.0, prev_ref[0][7:8, :])
    row = lax.broadcasted_iota(jnp.int32, x.shape, 0)
    prev = jnp.where(row == 0, prev_row, pltpu.roll(x, 1, 0))
    sm = x + (prev - x) * mu_ref[...]
    o_ref[0, :, :lora_w] = jnp.tanh(sm[:, :lora_w]).astype(o_ref.dtype)
    o_ref[0, :, lora_w:lora_w + lora_a] = sm[:, lora_w:lora_w + lora_a].astype(o_ref.dtype)
    o_ref[0, :, lora_w + lora_a:] = jax.nn.sigmoid(sm[:, lora_w + lora_a:]).astype(o_ref.dtype)


def _lora_in(p_sm, mu_sm, lora_w, lora_a, bsz, seq):
    wsm = mu_sm.shape[-1]
    ps = p_sm.reshape(bsz, seq, p_sm.shape[-1])
    ts = _pick(seq, (ROW_TILE, 128, 64, 8))
    return pl.pallas_call(
        functools.partial(_lora_in_kernel, lora_w=lora_w, lora_a=lora_a),
        grid=(bsz, seq // ts),
        in_specs=[pl.BlockSpec((1, ts, wsm), lambda b, i: (b, i, 0)),
                  pl.BlockSpec((1, 8, wsm), lambda b, i: (b, jnp.maximum(i * (ts // 8) - 1, 0), 0)),
                  pl.BlockSpec((1, wsm), lambda b, i: (0, 0))],
        out_specs=pl.BlockSpec((1, ts, wsm), lambda b, i: (b, i, 0)),
        out_shape=jax.ShapeDtypeStruct((bsz, seq, wsm), BF16),
        compiler_params=_params("parallel", "parallel"),
        name="rwkv_lora_in",
    )(ps, ps, mu_sm)


(VEC_MU_R, VEC_MU_K, VEC_MU_V, VEC_W0, VEC_A0, VEC_KK, VEC_KA, VEC_RK, VEC_LNW,
 VEC_LNB) = range(10)
VEC_ROWS = 16


def _rwkv_kernel(pr_ref, pk_ref, pv_ref, ps_ref, vec_ref, lora_ref, o_ref,
                 at_s, rt_s, kt_s, bt_s, kg_s, bg_s, v_s, g_s, bonus_s, gc_s,
                 m_s, n_s, q_s, yb_s, sb_s, *, seq, lora_w, lora_a):
    n_chunks = seq // CHUNK
    cpt = ROW_TILE // CHUNK

    def vec(idx):
        return vec_ref[idx:idx + 1, :]

    ri = lax.broadcasted_iota(jnp.int32, (LANES, LANES), 0)
    ci = lax.broadcasted_iota(jnp.int32, (LANES, LANES), 1)
    bd_mask = (ri // RWKV_HEAD) == (ci // RWKV_HEAD)
    bd_ones = bd_mask.astype(BF16)
    tr = lax.broadcasted_iota(jnp.int32, (ROW_TILE, ROW_TILE), 0)
    tc = lax.broadcasted_iota(jnp.int32, (ROW_TILE, ROW_TILE), 1)
    tri = ((tc <= tr) & ((tr // CHUNK) == (tc // CHUNK))).astype(BF16)

    def head_sum(x):
        return _dot(x.astype(BF16), bd_ones)

    def shift_mix(ref, t0, mu):
        x = ref[0, pl.ds(t0, ROW_TILE), :]
        if t0 == 0:
            prev_row = jnp.zeros((1, x.shape[1]), F32)
        else:
            prev_row = ref[0, pl.ds(t0 - 8, 8), :][7:8, :]
        row = lax.broadcasted_iota(jnp.int32, x.shape, 0)
        prev = jnp.where(row == 0, prev_row, pltpu.roll(x, 1, 0))
        return x + (prev - x) * mu

    def phase_a_stages(i):
        t0 = i * ROW_TILE
        rows = pl.ds(t0, ROW_TILE)
        st = {}

        def load_and_lora():
            st["r"] = shift_mix(pr_ref, t0, vec(VEC_MU_R))
            st["k"] = shift_mix(pk_ref, t0, vec(VEC_MU_K))
            st["v"] = shift_mix(pv_ref, t0, vec(VEC_MU_V))
            st["wlin"] = vec(VEC_W0) + _dot(ps_ref[0, rows, :lora_w], lora_ref[:lora_w, :])
            st["alin"] = vec(VEC_A0) + _dot(ps_ref[0, rows, lora_w:lora_w + lora_a],
                                            lora_ref[lora_w:lora_w + lora_a, :])
            st["g"] = _dot(ps_ref[0, rows, lora_w + lora_a:], lora_ref[lora_w + lora_a:, :])

        def decay_and_norm():
            wlin = st.pop("wlin")
            wlog = -(jnp.maximum(-wlin, 0.0) + jnp.log(1.0 + jnp.exp(-jnp.abs(wlin)))) - 0.5
            st["ld"] = -jnp.exp(wlog)
            st["alr"] = jax.nn.sigmoid(st.pop("alin"))
            st["kk"] = st["k"] * vec(VEC_KK)
            st["kk_ss"] = head_sum(st["kk"] * st["kk"])
            st["cum"] = _split_dot(tri, st["ld"])

        def keys_and_bonus():
            st["kk"] = st["kk"] * lax.rsqrt(jnp.maximum(st.pop("kk_ss"), 1e-24))
            st["k"] = st["k"] * (1.0 + (st["alr"] - 1.0) * vec(VEC_KA))
            st["bonus_sum"] = head_sum(st["r"] * st["k"] * vec(VEC_RK))
            st["b"] = st["kk"] * st["alr"]

        def scale_and_store():
            cum, ld = st["cum"], st.pop("ld")
            g_dec = jnp.exp(cum)
            g_inv = jnp.exp(-cum)
            at_s[rows, :] = (-st.pop("kk") * jnp.exp(cum - ld)).astype(BF16)
            rt_s[rows, :] = (st.pop("r") * g_dec).astype(BF16)
            kt_s[rows, :] = (st["k"] * g_inv).astype(BF16)
            bt_s[rows, :] = (st["b"] * g_inv).astype(BF16)
            v_s[rows, :] = st["v"].astype(BF16)
            g_s[rows, :] = st.pop("g")
            bonus_s[rows, :] = st.pop("bonus_sum") * st.pop("v")

        def chunk_tails():
            cum, k, b = st.pop("cum"), st.pop("k"), st.pop("b")
            for c in range(cpt):
                lo, hi = c * CHUNK, (c + 1) * CHUNK
                cum_c = cum[hi - 1:hi, :]
                rem = jnp.exp(cum_c - cum[lo:hi, :])
                kg_s[pl.ds(t0 + lo, CHUNK), :] = (k[lo:hi, :] * rem).astype(BF16)
                bg_s[pl.ds(t0 + lo, CHUNK), :] = (b[lo:hi, :] * rem).astype(BF16)
                gc_s[i * cpt + c] = jnp.broadcast_to(jnp.exp(cum_c), (8, LANES))

        return [load_and_lora, decay_and_norm, keys_and_bonus, scale_and_store, chunk_tails]

    row = lax.broadcasted_iota(jnp.int32, (CHUNK, LANES), 0)
    col = lax.broadcasted_iota(jnp.int32, (CHUNK, LANES), 1) % RWKV_HEAD
    strict = col < row
    incl = col <= row
    eye_p = (col == row).astype(F32)

    def same_block(size):
        return (row // size) == (col // size)

    def each(fn, *lists):
        return [fn(*xs) for xs in zip(*lists)]

    def phase_b(i, tick):
        cs = [i * GROUP + g for g in range(GROUP)]
        sls = [pl.ds(c * CHUNK, CHUNK) for c in cs]
        at = [at_s[sl, :] for sl in sls]
        rt = [rt_s[sl, :] for sl in sls]
        kt = [kt_s[sl, :] for sl in sls]
        bt = [bt_s[sl, :] for sl in sls]
        v = [v_s[sl, :] for sl in sls]
        sc = each(lambda a, r, k, b: _dot_nt(jnp.concatenate([a, r], axis=0),
                                             jnp.concatenate([_stack(k), _stack(b)], axis=0)),
                  at, rt, kt, bt)
        a_ak = [jnp.where(strict, x[:CHUNK, :LANES], 0.0).astype(BF16) for x in sc]
        a_ab = [jnp.where(strict, x[:CHUNK, LANES:], 0.0) for x in sc]
        a_rk = [jnp.where(incl, x[CHUNK:, :LANES], 0.0).astype(BF16) for x in sc]
        a_rb = [jnp.where(incl, x[CHUNK:, LANES:], 0.0).astype(BF16) for x in sc]
        tick()
        d = [jnp.where(same_block(INV_BASE), x, 0.0) for x in a_ab]
        db = [x.astype(BF16) for x in d]
        d2 = [_dot(x, _stack(x)).astype(BF16) for x in db]
        tick()
        p = [eye_p + x for x in d]
        p = each(lambda pp, x2: pp + _dot(pp.astype(BF16), _stack(x2)), p, d2)
        tick()
        s = INV_BASE
        while s < CHUNK:
            lower_left = same_block(2 * s) & ((row // s) % 2 == 1) & ((col // s) % 2 == 0)
            a21 = [jnp.where(lower_left, x, 0.0).astype(BF16) for x in a_ab]
            pb = [x.astype(BF16) for x in p]
            a21_t11 = each(lambda a, t: _dot(a, _stack(t)).astype(BF16), a21, pb)
            tick()
            p = each(lambda pp, t, y: pp + _dot(t, _stack(y)), p, pb, a21_t11)
            tick()
            s *= 2
        tinv = [x.astype(BF16) for x in p]
        vst = [_stack(x) for x in v]
        x1 = each(lambda a, vs: _dot(a, vs).astype(BF16), a_ak, vst)
        tick()
        wu = each(lambda t, a, x: _dot(t, jnp.concatenate([_stack(a), _stack(x)], axis=1)),
                  tinv, at, x1)
        tick()
        wb = [x[:, :LANES].astype(BF16) for x in wu]
        ub = [x[:, LANES:].astype(BF16) for x in wu]
        qy = each(lambda a, w, u: _dot(a, jnp.concatenate([_stack(w), _stack(u)], axis=1)),
                  a_rb, wb, ub)
        tick()
        yb = each(lambda a, vs, x: _dot(a, vs) + x[:, LANES:], a_rk, vst, qy)
        tick()
        for g in range(GROUP):
            q_s[sls[g], :] = (rt[g].astype(F32) + qy[g][:, :LANES]).astype(BF16)
            yb_s[sls[g], :] = yb[g]
        kg = [kg_s[sl, :] for sl in sls]
        bg = [bg_s[sl, :] for sl in sls]
        mm = each(_dot_tn, wb, bg)
        tick()
        nn = each(lambda vv, u, k, b: _dot_tn(jnp.concatenate([vv, u], axis=0),
                                              jnp.concatenate([k, b], axis=0)), v, ub, kg, bg)
        tick()
        for g in range(GROUP):
            m_s[cs[g]] = jnp.where(bd_mask, mm[g], 0.0).astype(BF16)
            n_s[cs[g]] = jnp.where(bd_mask, nn[g], 0.0)

    def c1_step(c, s):
        sb = s.astype(BF16)
        sb_s[c] = sb
        return s * gc_s[c][0:1, :] + _dot(sb, m_s[c]) + n_s[c]

    n_groups = n_chunks // GROUP
    tiles_per_group = GROUP * CHUNK // ROW_TILE
    state = [jnp.zeros((LANES, LANES), F32)]

    def c1_thunk(c):
        def run():
            state[0] = c1_step(c, state[0])
        return run

    def prep_stages(group):
        per_tile = [phase_a_stages(group * tiles_per_group + t) for t in range(tiles_per_group)]
        return [stage for stages in zip(*per_tile) for stage in stages]

    def interleave(xs, ys):
        out = []
        for j in range(max(len(xs), len(ys))):
            out += xs[j:j + 1] + ys[j:j + 1]
        return out

    def run_with_fillers(body, fillers):
        def tick():
            if fillers:
                fillers.pop(0)()
        body(tick)
        while fillers:
            fillers.pop(0)()

    for stage in prep_stages(0):
        stage()
    for i in range(n_groups):
        fillers = interleave(
            [c1_thunk((i - 1) * GROUP + g) for g in range(GROUP)] if i > 0 else [],
            prep_stages(i + 1) if i + 1 < n_groups else [])
        run_with_fillers(functools.partial(phase_b, i), fillers)

    inv_n = 1.0 / RWKV_HEAD

    def phase_c2(i, tick):
        cs = [i * GROUP + g for g in range(GROUP)]
        sls = [pl.ds(c * CHUNK, CHUNK) for c in cs]

        def staged(fn, *lists):
            out = []
            for g, xs in enumerate(zip(*lists)):
                out.append(fn(*xs))
                if g % 4 == 3:
                    tick()
            return out

        y = staged(lambda sl, c: _dot_nt(q_s[sl, :], sb_s[c]) + yb_s[sl, :], sls, cs)
        mean = staged(lambda x: head_sum(x) * inv_n, y)
        yc = each(lambda a, b: a - b, y, mean)
        var = staged(lambda x: head_sum(x * x) * inv_n, yc)
        for g in range(GROUP):
            yn = yc[g] * lax.rsqrt(var[g] + LN_X_EPS) * vec(VEC_LNW) + vec(VEC_LNB)
            o_ref[0, sls[g], :] = ((yn + bonus_s[sls[g], :]) * g_s[sls[g], :]).astype(o_ref.dtype)

    tail = [c1_thunk((n_groups - 1) * GROUP + g) for g in range(GROUP)]
    for i in range(n_groups - 1):
        run_with_fillers(functools.partial(phase_c2, i), tail if i == n_groups - 2 else [])
    while tail:
        tail.pop(0)()
    phase_c2(n_groups - 1, lambda: None)


def _rwkv(p_rkv, p_sm, mu_rkv, mu_sm, w0, a0, k_k, k_a, r_k, ln_w, ln_b,
          w2p, a2p, g2p, bsz, seq):
    rw = w0.shape[-1]
    npair = rw // LANES
    lora_w, lora_a, lora_g = w2p.shape[0], a2p.shape[0], g2p.shape[0]
    wsm = lora_w + lora_a + lora_g
    p3 = p_rkv.reshape(bsz, seq, 3 * rw)
    ps = _lora_in(p_sm, mu_sm, lora_w, lora_a, bsz, seq)
    n_chunks = seq // CHUNK

    def col(off):
        return pl.BlockSpec((1, seq, LANES), lambda b, h: (b, 0, off + h))

    rows = [mu_rkv[0, :rw], mu_rkv[0, rw:2 * rw], mu_rkv[0, 2 * rw:],
            w0, a0, k_k, k_a, r_k.reshape(rw), ln_w, ln_b]
    vecs = jnp.concatenate([jnp.stack(rows), jnp.zeros((VEC_ROWS - len(rows), rw), F32)], axis=0)
    lora = jnp.concatenate([w2p, a2p, g2p], axis=0)

    return pl.pallas_call(
        functools.partial(_rwkv_kernel, seq=seq, lora_w=lora_w, lora_a=lora_a),
        grid=(bsz, npair),
        in_specs=[col(0), col(npair), col(2 * npair),
                  pl.BlockSpec((1, seq, wsm), lambda b, h: (b, 0, 0)),
                  pl.BlockSpec((VEC_ROWS, LANES), lambda b, h: (0, h)),
                  pl.BlockSpec((wsm, LANES), lambda b, h: (0, h))],
        out_specs=pl.BlockSpec((1, seq, LANES), lambda b, h: (b, 0, h)),
        out_shape=jax.ShapeDtypeStruct((bsz, seq, rw), BF16),
        scratch_shapes=(
            [pltpu.VMEM((seq, LANES), BF16) for _ in range(7)]
            + [pltpu.VMEM((seq, LANES), F32) for _ in range(2)]
            + [pltpu.VMEM((n_chunks, 8, LANES), F32),
               pltpu.VMEM((n_chunks, LANES, LANES), BF16),
               pltpu.VMEM((n_chunks, LANES, LANES), F32),
               pltpu.VMEM((seq, LANES), BF16),
               pltpu.VMEM((seq, LANES), F32),
               pltpu.VMEM((n_chunks, LANES, LANES), BF16)]),
        compiler_params=_params("parallel", "parallel"),
        name="rwkv7",
    )(p3, p3, p3, ps, vecs, lora)


def _fgate_kernel(f_ref, bf_ref, col_ref, row_ref, *, seq):
    nblk = seq // LANES
    ri = lax.broadcasted_iota(jnp.int32, (LANES, LANES), 0)
    ci = lax.broadcasted_iota(jnp.int32, (LANES, LANES), 1)
    tri = (ci <= ri).astype(BF16)

    def body(i, carry):
        t0 = pl.multiple_of(i * LANES, LANES)
        z = f_ref[0, pl.ds(t0, LANES), :] + bf_ref[...]
        logf = jnp.minimum(z, 0.0) - jnp.log1p(jnp.exp(-jnp.abs(z)))
        hi = logf.astype(BF16)
        r1 = logf - hi.astype(F32)
        mid = r1.astype(BF16)
        lo = (r1 - mid.astype(F32)).astype(BF16)
        cum = carry + (_dot(tri, hi) + _dot(tri, mid) + _dot(tri, lo))
        col_ref[0, pl.ds(t0, LANES), :] = cum
        row_ref[0, :, pl.ds(t0, LANES)] = cum.T
        return cum[LANES - 1:LANES, :]

    lax.fori_loop(0, nblk, body, jnp.zeros((1, LANES), F32))


def _fgate(p_sm, b_f_pad, off, bsz, seq):
    ps = p_sm.reshape(bsz, seq, p_sm.shape[-1])
    return pl.pallas_call(
        functools.partial(_fgate_kernel, seq=seq),
        grid=(bsz,),
        in_specs=[pl.BlockSpec((1, seq, LANES), lambda b: (b, 0, off // LANES)),
                  pl.BlockSpec((1, LANES), lambda b: (0, 0))],
        out_specs=[pl.BlockSpec((1, seq, LANES), lambda b: (b, 0, 0)),
                   pl.BlockSpec((1, LANES, seq), lambda b: (b, 0, 0))],
        out_shape=[jax.ShapeDtypeStruct((bsz, seq, LANES), F32),
                   jax.ShapeDtypeStruct((bsz, LANES, seq), F32)],
        compiler_params=_params("parallel"),
        name="fox_gates",
    )(ps, b_f_pad)


FOX_HEADS_PER_STEP = 2
LOG2E = 1.4426950408889634


def _fox_kernel(q_ref, k_ref, v_ref, ccol_ref, crow_ref, on_ref, o_ref, *, tq, seq, scale):
    hg = pl.program_id(1)
    heads = range(FOX_HEADS_PER_STEP)
    hsl = [slice(g * FOX_HEAD, (g + 1) * FOX_HEAD) for g in heads]
    lane = lax.broadcasted_iota(jnp.int32, (tq, LANES), 1)
    causal = (lax.broadcasted_iota(jnp.int32, (tq, tq), 1)
              <= lax.broadcasted_iota(jnp.int32, (tq, tq), 0))

    for i in range(seq // tq):
        qrows = pl.ds(i * tq, tq)
        q = [q_ref[0, qrows, sl] for sl in hsl]
        ccol = ccol_ref[0, qrows, :]
        cq = [LOG2E * jnp.sum(jnp.where(lane == hg * FOX_HEADS_PER_STEP + g, ccol, 0.0),
                              axis=-1, keepdims=True) for g in heads]

        def block(j, carry, diagonal, q=q, cq=cq):
            m, l, acc = carry
            k0 = pl.multiple_of(j * tq, tq)
            kb = [k_ref[0, pl.ds(k0, tq), sl] for sl in hsl]
            vb = [v_ref[0, pl.ds(k0, tq), sl] for sl in hsl]
            ck = [LOG2E * crow_ref[0, g, :, pl.ds(k0, tq)] for g in heads]
            s = [_dot_nt(q[g], kb[g]) * (scale * LOG2E) + cq[g] - ck[g] for g in heads]
            if diagonal:
                s = [jnp.where(causal, x, -jnp.inf) for x in s]
            m_new = [jnp.maximum(m[g], jnp.max(s[g], axis=-1, keepdims=True)) for g in heads]
            alpha = [jnp.exp2(m[g] - m_new[g]) for g in heads]
            p = [jnp.exp2(s[g] - m_new[g]) for g in heads]
            l = [alpha[g] * l[g] + jnp.sum(p[g], axis=-1, keepdims=True) for g in heads]
            acc = [alpha[g] * acc[g] + _dot(p[g].astype(BF16), vb[g]) for g in heads]
            return tuple(m_new), tuple(l), tuple(acc)

        init = (tuple(jnp.full((tq, 1), -jnp.inf, F32) for _ in heads),
                tuple(jnp.zeros((tq, 1), F32) for _ in heads),
                tuple(jnp.zeros((tq, FOX_HEAD), F32) for _ in heads))
        carry = lax.fori_loop(0, i, lambda j, c, block=block: block(j, c, False), init)
        _, l, acc = block(i, carry, True)
        for g in heads:
            o = acc[g] / l[g]
            o = o * lax.rsqrt(jnp.mean(o * o, axis=-1, keepdims=True) + NORM_EPS) * on_ref[g]
            o_ref[0, qrows, hsl[g]] = o.astype(o_ref.dtype)


def _fox(p_fox, ccol, crow, out_norm, bsz, seq):
    fw = p_fox.shape[-1] // 3
    nh = fw // FOX_HEAD
    hps = FOX_HEADS_PER_STEP
    ng = nh // hps
    wid = hps * FOX_HEAD
    tq = _pick(seq, (512, 256, 128))
    p3 = p_fox.reshape(bsz, seq, 3 * fw)
    crow4 = crow.reshape(bsz, LANES, 1, seq)
    on = out_norm.reshape(nh, 1, FOX_HEAD)
    return pl.pallas_call(
        functools.partial(_fox_kernel, tq=tq, seq=seq, scale=FOX_HEAD ** -0.5),
        grid=(bsz, ng),
        in_specs=[pl.BlockSpec((1, seq, wid), lambda b, h: (b, 0, h)),
                  pl.BlockSpec((1, seq, wid), lambda b, h: (b, 0, ng + h)),
                  pl.BlockSpec((1, seq, wid), lambda b, h: (b, 0, 2 * ng + h)),
                  pl.BlockSpec((1, seq, LANES), lambda b, h: (b, 0, 0)),
                  pl.BlockSpec((1, hps, 1, seq), lambda b, h: (b, h, 0, 0)),
                  pl.BlockSpec((hps, 1, FOX_HEAD), lambda b, h: (h, 0, 0))],
        out_specs=pl.BlockSpec((1, seq, wid), lambda b, h: (b, 0, h)),
        out_shape=jax.ShapeDtypeStruct((bsz, seq, fw), BF16),
        compiler_params=_params("parallel", "parallel"),
        name="fox_attention",
    )(p3, p3, p3, ccol, crow4, on)


def _pad_rows(w, rows):
    return jnp.pad(w, ((0, rows - w.shape[0]), (0, 0)))


def _cast_pad_kernel(w_ref, o_ref, *, n_valid):
    @pl.when(pl.program_id(0) < n_valid)
    def _():
        o_ref[...] = w_ref[...].astype(o_ref.dtype)

    @pl.when(pl.program_id(0) >= n_valid)
    def _():
        o_ref[...] = jnp.zeros_like(o_ref)


def _cast_pad_rows(w, rows_out):
    rows, cols = w.shape
    tr = _pick(rows, (256, 128, 64, 16))
    assert rows_out % tr == 0
    n_valid = rows // tr
    return pl.pallas_call(
        functools.partial(_cast_pad_kernel, n_valid=n_valid),
        grid=(rows_out // tr,),
        in_specs=[pl.BlockSpec((tr, cols), lambda i: (jnp.minimum(i, n_valid - 1), 0))],
        out_specs=pl.BlockSpec((tr, cols), lambda i: (i, 0)),
        out_shape=jax.ShapeDtypeStruct((rows_out, cols), BF16),
        compiler_params=_params("parallel"),
        name="cast_pad_rows",
    )(w)


def _ffn(x2, h, w_gate, w_up, w_down, gate, seq):
    dff = w_gate.shape[1]
    dffp = _round_up(dff, 512)
    wd = _cast_pad_rows(w_down, dffp)
    act = _swiglu_up(h, w_gate, w_up, dffp)
    return _matmul_residual(act, wd, x2, gate, seq, 0.5)


def kernel(x, c, w_mod, b_mod, norm_ffn1, ffn1_gate, ffn1_up, ffn1_down, norm_mix, w_in, rwkv_mu, rwkv_w0, rwkv_w2, rwkv_a0, rwkv_a2, rwkv_g2, rwkv_k_k, rwkv_k_a, rwkv_r_k, rwkv_ln_w, rwkv_ln_b, fox_b_f, fox_out_norm, w_out, norm_ffn2, ffn2_gate, ffn2_up, ffn2_down, norm_final):
    bsz, seq, d = x.shape
    depth = w_mod.shape[0]
    rw = rwkv_w0.shape[-1]
    fw = fox_out_norm.shape[-1]
    nfh = fox_b_f.shape[-1]
    dl, al, gl = rwkv_w2.shape[1], rwkv_a2.shape[1], rwkv_g2.shape[1]
    dlp, alp, glp = (_round_up(n, LANES) for n in (dl, al, gl))
    c3 = 3 * rw
    c4, c5, c6 = c3 + dl, c3 + dl + al, c3 + dl + al + gl

    x2 = x.reshape(bsz * seq, d)
    for l in range(depth):
        mod = _mod(c, w_mod[l], b_mod[l])
        sh1, sc1, gt1, sh2, sc2, gt2, sh3, sc3, gt3 = (
            m[:, None, :] for m in jnp.split(mod, 9, axis=-1))

        h = _norm_mod(x2, norm_ffn1[l], sc1, sh1, seq)
        x2 = _ffn(x2, h, ffn1_gate[l], ffn1_up[l], ffn1_down[l], gt1, seq)

        h = _norm_mod(x2, norm_mix[l], sc2, sh2, seq)
        wi, mu = w_in[l], rwkv_mu[l]
        wt = jnp.swapaxes(wi, 0, 1).astype(BF16)
        wt_sm = jnp.concatenate(
            [_pad_rows(wt[c3:c4], dlp), _pad_rows(wt[c4:c5], alp),
             _pad_rows(wt[c5:c6], glp), _pad_rows(wt[c6 + 3 * fw:], LANES)], axis=0)
        mu_rkv = mu[:c3].reshape(1, c3)
        mu_sm = jnp.concatenate(
            [jnp.pad(mu[c3:c4], (0, dlp - dl)), jnp.pad(mu[c4:c5], (0, alp - al)),
             jnp.pad(mu[c5:c6], (0, glp - gl))]).reshape(1, dlp + alp + glp)
        p_rkv = _matmul_nt(h, wt, F32, 0, c3)
        p_sm = _matmul_nt(h, wt_sm, F32)
        p_fox = _matmul_nt(h, wt, BF16, c6, 3 * fw)

        y_r = _rwkv(p_rkv, p_sm, mu_rkv, mu_sm,
                    rwkv_w0[l], rwkv_a0[l], rwkv_k_k[l], rwkv_k_a[l], rwkv_r_k[l],
                    rwkv_ln_w[l], rwkv_ln_b[l],
                    _pad_rows(rwkv_w2[l], dlp).astype(BF16),
                    _pad_rows(rwkv_a2[l], alp).astype(BF16),
                    _pad_rows(rwkv_g2[l], glp).astype(BF16), bsz, seq)

        b_f_pad = jnp.pad(fox_b_f[l], (0, LANES - nfh)).reshape(1, LANES)
        ccol, crow = _fgate(p_sm, b_f_pad, dlp + alp + glp, bsz, seq)
        y_f = _fox(p_fox, ccol, crow, fox_out_norm[l], bsz, seq)

        x2 = _matmul_residual2(y_r.reshape(bsz * seq, rw), y_f.reshape(bsz * seq, fw),
                               w_out[l].astype(BF16), x2, gt2, seq, 1.0)

        h = _norm_mod(x2, norm_ffn2[l], sc3, sh3, seq)
        x2 = _ffn(x2, h, ffn2_gate[l], ffn2_up[l], ffn2_down[l], gt3, seq)
    return _final_norm(x2, norm_final).reshape(bsz, seq, d)
```

```python
import functools

import jax
import jax.numpy as jnp
from jax import lax
from jax.experimental import pallas as pl
from jax.experimental.pallas import tpu as pltpu

F32 = jnp.float32
BF16 = jnp.bfloat16

LANES = 128
RWKV_HEAD = 64
FOX_HEAD = 128
CHUNK = 64
INV_BASE = 4
GROUP = 16
NORM_EPS = 1e-6
LN_X_EPS = 64e-5
VMEM_LIMIT = 56 * 1024 * 1024


def _round_up(n, m):
    return (n + m - 1) // m * m


def _pick(n, candidates):
    for c in candidates:
        if n % c == 0:
            return c
    return n


def _dot(a, b):
    return jnp.dot(a, b, preferred_element_type=F32)


def _dot_nt(a, b):
    return lax.dot_general(a, b, (((1,), (1,)), ((), ())), preferred_element_type=F32)


def _dot_tn(a, b):
    return lax.dot_general(a, b, (((0,), (0,)), ((), ())), preferred_element_type=F32)


def _params(*sem):
    return pltpu.CompilerParams(dimension_semantics=sem, vmem_limit_bytes=VMEM_LIMIT)


def _mod_kernel(c_ref, w_ref, b_ref, o_ref):
    c = c_ref[...]
    a = (c * jax.nn.sigmoid(c)).astype(BF16)
    o_ref[...] = _dot(a, w_ref[...].astype(BF16)) + b_ref[...]


def _mod(c, w_mod, b_mod):
    bsz, d = c.shape
    n = w_mod.shape[1]
    rows = _round_up(bsz, 16)
    c_pad = jnp.pad(c, ((0, rows - bsz), (0, 0)))
    tn = _pick(n, (512, 256, 128))
    out = pl.pallas_call(
        _mod_kernel,
        grid=(n // tn,),
        in_specs=[pl.BlockSpec((rows, d), lambda j: (0, 0)),
                  pl.BlockSpec((d, tn), lambda j: (0, j)),
                  pl.BlockSpec((1, tn), lambda j: (0, j))],
        out_specs=pl.BlockSpec((rows, tn), lambda j: (0, j)),
        out_shape=jax.ShapeDtypeStruct((rows, n), F32),
        compiler_params=_params("parallel"),
        name="mod_matmul",
    )(c_pad, w_mod, b_mod.reshape(1, n))
    return out[:bsz]


def _norm_kernel(x_ref, g_ref, sc_ref, sh_ref, o_ref):
    x = x_ref[...]
    y = x * lax.rsqrt(jnp.mean(x * x, axis=-1, keepdims=True) + NORM_EPS) * g_ref[...]
    o_ref[...] = (y * (1.0 + sc_ref[0]) + sh_ref[0]).astype(o_ref.dtype)


def _final_norm_kernel(x_ref, g_ref, o_ref):
    x = x_ref[...]
    o_ref[...] = x * lax.rsqrt(jnp.mean(x * x, axis=-1, keepdims=True) + NORM_EPS) * g_ref[...]


def _norm_mod(x2, g, sc, sh, seq):
    t, d = x2.shape
    ts = _pick(seq, (512, 256, 128, 64, 8))
    return pl.pallas_call(
        _norm_kernel,
        grid=(t // ts,),
        in_specs=[pl.BlockSpec((ts, d), lambda i: (i, 0)),
                  pl.BlockSpec((1, d), lambda i: (0, 0)),
                  pl.BlockSpec((1, 1, d), lambda i: (i * ts // seq, 0, 0)),
                  pl.BlockSpec((1, 1, d), lambda i: (i * ts // seq, 0, 0))],
        out_specs=pl.BlockSpec((ts, d), lambda i: (i, 0)),
        out_shape=jax.ShapeDtypeStruct((t, d), BF16),
        compiler_params=_params("parallel"),
        name="norm_mod",
    )(x2, g.reshape(1, d), sc, sh)


def _final_norm(x2, g):
    t, d = x2.shape
    ts = _pick(t, (512, 256, 128, 64, 8))
    return pl.pallas_call(
        _final_norm_kernel,
        grid=(t // ts,),
        in_specs=[pl.BlockSpec((ts, d), lambda i: (i, 0)),
                  pl.BlockSpec((1, d), lambda i: (0, 0))],
        out_specs=pl.BlockSpec((ts, d), lambda i: (i, 0)),
        out_shape=jax.ShapeDtypeStruct((t, d), F32),
        compiler_params=_params("parallel"),
        name="final_norm",
    )(x2, g.reshape(1, d))


def _mm_nt_kernel(x_ref, wt_ref, o_ref):
    o_ref[...] = _dot_nt(x_ref[...], wt_ref[...]).astype(o_ref.dtype)


def _matmul_nt(x, wt, out_dtype, row0=0, n=None):
    m, kd = x.shape
    n = wt.shape[0] if n is None else n
    tm = _pick(m, (1024, 512, 256, 128))
    tn = _pick(n, (1024, 512, 640, 256, 128))
    return pl.pallas_call(
        _mm_nt_kernel,
        grid=(m // tm, n // tn),
        in_specs=[pl.BlockSpec((tm, kd), lambda i, j: (i, 0)),
                  pl.BlockSpec((pl.Element(tn), pl.Element(kd)),
                               lambda i, j: ((row0 // 16 + j * (tn // 16)) * 16, 0))],
        out_specs=pl.BlockSpec((tm, tn), lambda i, j: (i, j)),
        out_shape=jax.ShapeDtypeStruct((m, n), out_dtype),
        compiler_params=_params("parallel", "parallel"),
        name="matmul_nt",
    )(x, wt)


def _swiglu_kernel(x_ref, wg_hbm, wu_hbm, o_ref, stage_ref, wgb_ref, wub_ref, sem,
                   *, n_full, rem, tn):
    j = pl.program_id(0)

    def full_copies(jj):
        cols = pl.ds(pl.multiple_of(jj * tn, tn), tn)
        return (pltpu.make_async_copy(wg_hbm.at[:, cols], stage_ref.at[0], sem.at[0]),
                pltpu.make_async_copy(wu_hbm.at[:, cols], stage_ref.at[1], sem.at[1]))

    def tail_copies():
        cols = pl.ds(n_full * tn, rem)
        return (pltpu.make_async_copy(wg_hbm.at[:, cols], stage_ref.at[0, :, :rem], sem.at[0]),
                pltpu.make_async_copy(wu_hbm.at[:, cols], stage_ref.at[1, :, :rem], sem.at[1]))

    def start(jj):
        @pl.when(jj < n_full)
        def _():
            for c in full_copies(jj):
                c.start()

        if rem:
            @pl.when(jj == n_full)
            def _():
                for c in tail_copies():
                    c.start()

    @pl.when(pl.program_id(1) == 0)
    def _():
        @pl.when(j == 0)
        def _():
            start(j)

        @pl.when(j < n_full)
        def _():
            for c in full_copies(j):
                c.wait()
            wgb_ref[...] = stage_ref[0].astype(BF16)
            wub_ref[...] = stage_ref[1].astype(BF16)

        if rem:
            @pl.when(j == n_full)
            def _():
                for c in tail_copies():
                    c.wait()
                wgb_ref[:, :rem] = stage_ref[0, :, :rem].astype(BF16)
                wub_ref[:, :rem] = stage_ref[1, :, :rem].astype(BF16)
                wgb_ref[:, rem:] = jnp.zeros((wgb_ref.shape[0], tn - rem), BF16)
                wub_ref[:, rem:] = jnp.zeros((wub_ref.shape[0], tn - rem), BF16)

        start(j + 1)

    x = x_ref[...]
    g = _dot(x, wgb_ref[...])
    u = _dot(x, wub_ref[...])
    o_ref[...] = (g * (0.5 * jnp.tanh(0.5 * g) + 0.5) * u).astype(o_ref.dtype)


def _swiglu_up(h, wg, wu, n_out):
    m, kd = h.shape
    n = wg.shape[1]
    tm = _pick(m, (1024, 512, 256, 128))
    tn = 512
    n_full, rem = divmod(n, tn)
    assert rem % LANES == 0 and n_out == _round_up(n, tn)
    return pl.pallas_call(
        functools.partial(_swiglu_kernel, n_full=n_full, rem=rem, tn=tn),
        grid=(n_out // tn, m // tm),
        in_specs=[pl.BlockSpec((tm, kd), lambda j, i: (i, 0)),
                  pl.BlockSpec(memory_space=pl.ANY),
                  pl.BlockSpec(memory_space=pl.ANY)],
        out_specs=pl.BlockSpec((tm, tn), lambda j, i: (i, j)),
        out_shape=jax.ShapeDtypeStruct((m, n_out), BF16),
        scratch_shapes=[pltpu.VMEM((2, kd, tn), F32),
                        pltpu.VMEM((kd, tn), BF16), pltpu.VMEM((kd, tn), BF16),
                        pltpu.SemaphoreType.DMA((2,))],
        compiler_params=_params("arbitrary", "arbitrary"),
        name="swiglu_up",
    )(h, wg, wu)


def _resid_kernel(a_ref, w_ref, x_ref, gt_ref, o_ref, acc_ref, *, nk, scale):
    k = pl.program_id(2)

    @pl.when(k == 0)
    def _():
        acc_ref[...] = _dot(a_ref[...], w_ref[...])

    @pl.when((k > 0) & (k < nk - 1))
    def _():
        acc_ref[...] += _dot(a_ref[...], w_ref[...])

    @pl.when(k == nk - 1)
    def _():
        o_ref[...] = x_ref[...] + (scale * gt_ref[0]) * (acc_ref[...] + _dot(a_ref[...], w_ref[...]))


def _matmul_residual(a, w, x2, gate, seq, scale):
    m, kd = a.shape
    n = w.shape[1]
    tm = _pick(seq, (1024, 512, 256, 128))
    tn = _pick(n, (1024, 512, 256, 128))
    tk = _pick(kd, (2816, 1024, 512, 256, 128))
    nk = kd // tk
    assert nk >= 2
    return pl.pallas_call(
        functools.partial(_resid_kernel, nk=nk, scale=scale),
        grid=(m // tm, n // tn, nk),
        in_specs=[pl.BlockSpec((tm, tk), lambda i, j, k: (i, k)),
                  pl.BlockSpec((tk, tn), lambda i, j, k: (k, j)),
                  pl.BlockSpec((tm, tn), lambda i, j, k: (i, j)),
                  pl.BlockSpec((1, 1, tn), lambda i, j, k: (i * tm // seq, 0, j))],
        out_specs=pl.BlockSpec((tm, tn), lambda i, j, k: (i, j)),
        out_shape=jax.ShapeDtypeStruct((m, n), F32),
        scratch_shapes=[pltpu.VMEM((tm, tn), F32)],
        compiler_params=_params("parallel", "parallel", "arbitrary"),
        name="matmul_residual",
    )(a, w, x2, gate)


def _resid2_kernel(a1_ref, a2_ref, w1_ref, w2_ref, x_ref, gt_ref, o_ref, *, scale):
    y = _dot(a1_ref[...], w1_ref[...]) + _dot(a2_ref[...], w2_ref[...])
    o_ref[...] = x_ref[...] + (scale * gt_ref[0]) * y


def _matmul_residual2(a1, a2, w, x2, gate, seq, scale):
    m, k1 = a1.shape
    k2 = a2.shape[1]
    n = w.shape[1]
    assert k1 == k2
    tm = _pick(seq, (1024, 512, 256, 128))
    tn = _pick(n, (512, 256, 128))
    return pl.pallas_call(
        functools.partial(_resid2_kernel, scale=scale),
        grid=(m // tm, n // tn),
        in_specs=[pl.BlockSpec((tm, k1), lambda i, j: (i, 0)),
                  pl.BlockSpec((tm, k2), lambda i, j: (i, 0)),
                  pl.BlockSpec((k1, tn), lambda i, j: (0, j)),
                  pl.BlockSpec((k2, tn), lambda i, j: (1, j)),
                  pl.BlockSpec((tm, tn), lambda i, j: (i, j)),
                  pl.BlockSpec((1, 1, tn), lambda i, j: (i * tm // seq, 0, j))],
        out_specs=pl.BlockSpec((tm, tn), lambda i, j: (i, j)),
        out_shape=jax.ShapeDtypeStruct((m, n), F32),
        compiler_params=_params("parallel", "parallel"),
        name="out_proj_residual",
    )(a1, a2, w, w, x2, gate)


ROW_TILE = 512


def _lane_head0(shape):
    return lax.broadcasted_iota(jnp.int32, shape, 1) < RWKV_HEAD


def _stack(x):
    m0 = _lane_head0(x.shape)
    zero = jnp.zeros_like(x)
    return jnp.concatenate([jnp.where(m0, x, zero), jnp.where(m0, zero, x)], axis=0)


def _split_dot(lhs_bf16, x):
    hi = x.astype(BF16)
    lo = (x - hi.astype(F32)).astype(BF16)
    return _dot(lhs_bf16, hi) + _dot(lhs_bf16, lo)


def _lora_in_kernel(ps_ref, prev_ref, mu_ref, o_ref, *, lora_w, lora_a):
    x = ps_ref[0]
    prev_row = jnp.where(pl.program_id(1) == 0, 0.0, prev_ref[0][7:8, :])
    row = lax.broadcasted_iota(jnp.int32, x.shape, 0)
    prev = jnp.where(row == 0, prev_row, pltpu.roll(x, 1, 0))
    sm = x + (prev - x) * mu_ref[...]
    o_ref[0, :, :lora_w] = jnp.tanh(sm[:, :lora_w]).astype(o_ref.dtype)
    o_ref[0, :, lora_w:lora_w + lora_a] = sm[:, lora_w:lora_w + lora_a].astype(o_ref.dtype)
    o_ref[0, :, lora_w + lora_a:] = jax.nn.sigmoid(sm[:, lora_w + lora_a:]).astype(o_ref.dtype)


def _lora_in(p_sm, mu_sm, lora_w, lora_a, bsz, seq):
    wsm = mu_sm.shape[-1]
    ps = p_sm.reshape(bsz, seq, p_sm.shape[-1])
    ts = _pick(seq, (ROW_TILE, 128, 64, 8))
    return pl.pallas_call(
        functools.partial(_lora_in_kernel, lora_w=lora_w, lora_a=lora_a),
        grid=(bsz, seq // ts),
        in_specs=[pl.BlockSpec((1, ts, wsm), lambda b, i: (b, i, 0)),
                  pl.BlockSpec((1, 8, wsm), lambda b, i: (b, jnp.maximum(i * (ts // 8) - 1, 0), 0)),
                  pl.BlockSpec((1, wsm), lambda b, i: (0, 0))],
        out_specs=pl.BlockSpec((1, ts, wsm), lambda b, i: (b, i, 0)),
        out_shape=jax.ShapeDtypeStruct((bsz, seq, wsm), BF16),
        compiler_params=_params("parallel", "parallel"),
        name="rwkv_lora_in",
    )(ps, ps, mu_sm)


(VEC_MU_R, VEC_MU_K, VEC_MU_V, VEC_W0, VEC_A0, VEC_KK, VEC_KA, VEC_RK, VEC_LNW,
 VEC_LNB) = range(10)
VEC_ROWS = 16


def _rwkv_kernel(pr_ref, pk_ref, pv_ref, ps_ref, vec_ref, lora_ref, o_ref,
                 at_s, rt_s, kt_s, bt_s, kg_s, bg_s, v_s, g_s, bonus_s, gc_s,
                 m_s, n_s, q_s, yb_s, sb_s, *, seq, lora_w, lora_a):
    n_chunks = seq // CHUNK
    cpt = ROW_TILE // CHUNK

    def vec(idx):
        return vec_ref[idx:idx + 1, :]

    ri = lax.broadcasted_iota(jnp.int32, (LANES, LANES), 0)
    ci = lax.broadcasted_iota(jnp.int32, (LANES, LANES), 1)
    bd_mask = (ri // RWKV_HEAD) == (ci // RWKV_HEAD)
    bd_ones = bd_mask.astype(BF16)
    tr = lax.broadcasted_iota(jnp.int32, (ROW_TILE, ROW_TILE), 0)
    tc = lax.broadcasted_iota(jnp.int32, (ROW_TILE, ROW_TILE), 1)
    tri = ((tc <= tr) & ((tr // CHUNK) == (tc // CHUNK))).astype(BF16)

    def head_sum(x):
        return _dot(x.astype(BF16), bd_ones)

    def shift_mix(ref, t0, mu):
        x = ref[0, pl.ds(t0, ROW_TILE), :]
        if t0 == 0:
            prev_row = jnp.zeros((1, x.shape[1]), F32)
        else:
            prev_row = ref[0, pl.ds(t0 - 8, 8), :][7:8, :]
        row = lax.broadcasted_iota(jnp.int32, x.shape, 0)
        prev = jnp.where(row == 0, prev_row, pltpu.roll(x, 1, 0))
        return x + (prev - x) * mu

    def phase_a_stages(i):
        t0 = i * ROW_TILE
        rows = pl.ds(t0, ROW_TILE)
        st = {}

        def load_and_lora():
            st["r"] = shift_mix(pr_ref, t0, vec(VEC_MU_R))
            st["k"] = shift_mix(pk_ref, t0, vec(VEC_MU_K))
            st["v"] = shift_mix(pv_ref, t0, vec(VEC_MU_V))
            st["wlin"] = vec(VEC_W0) + _dot(ps_ref[0, rows, :lora_w], lora_ref[:lora_w, :])
            st["alin"] = vec(VEC_A0) + _dot(ps_ref[0, rows, lora_w:lora_w + lora_a],
                                            lora_ref[lora_w:lora_w + lora_a, :])
            st["g"] = _dot(ps_ref[0, rows, lora_w + lora_a:], lora_ref[lora_w + lora_a:, :])

        def decay_and_norm():
            wlin = st.pop("wlin")
            wlog = -(jnp.maximum(-wlin, 0.0) + jnp.log(1.0 + jnp.exp(-jnp.abs(wlin)))) - 0.5
            st["ld"] = -jnp.exp(wlog)
            st["alr"] = jax.nn.sigmoid(st.pop("alin"))
            st["kk"] = st["k"] * vec(VEC_KK)
            st["kk_ss"] = head_sum(st["kk"] * st["kk"])
            st["cum"] = _split_dot(tri, st["ld"])

        def keys_and_bonus():
            st["kk"] = st["kk"] * lax.rsqrt(jnp.maximum(st.pop("kk_ss"), 1e-24))
            st["k"] = st["k"] * (1.0 + (st["alr"] - 1.0) * vec(VEC_KA))
            st["bonus_sum"] = head_sum(st["r"] * st["k"] * vec(VEC_RK))
            st["b"] = st["kk"] * st["alr"]

        def scale_and_store():
            cum, ld = st["cum"], st.pop("ld")
            g_dec = jnp.exp(cum)
            g_inv = jnp.exp(-cum)
            at_s[rows, :] = (-st.pop("kk") * jnp.exp(cum - ld)).astype(BF16)
            rt_s[rows, :] = (st.pop("r") * g_dec).astype(BF16)
            kt_s[rows, :] = (st["k"] * g_inv).astype(BF16)
            bt_s[rows, :] = (st["b"] * g_inv).astype(BF16)
            v_s[rows, :] = st["v"].astype(BF16)
            g_s[rows, :] = st.pop("g")
            bonus_s[rows, :] = st.pop("bonus_sum") * st.pop("v")

        def chunk_tails():
            cum, k, b = st.pop("cum"), st.pop("k"), st.pop("b")
            for c in range(cpt):
                lo, hi = c * CHUNK, (c + 1) * CHUNK
                cum_c = cum[hi - 1:hi, :]
                rem = jnp.exp(cum_c - cum[lo:hi, :])
                kg_s[pl.ds(t0 + lo, CHUNK), :] = (k[lo:hi, :] * rem).astype(BF16)
                bg_s[pl.ds(t0 + lo, CHUNK), :] = (b[lo:hi, :] * rem).astype(BF16)
                gc_s[i * cpt + c] = jnp.broadcast_to(jnp.exp(cum_c), (8, LANES))

        return [load_and_lora, decay_and_norm, keys_and_bonus, scale_and_store, chunk_tails]

    row = lax.broadcasted_iota(jnp.int32, (CHUNK, LANES), 0)
    col = lax.broadcasted_iota(jnp.int32, (CHUNK, LANES), 1) % RWKV_HEAD
    strict = col < row
    incl = col <= row
    eye_p = (col == row).astype(F32)

    def same_block(size):
        return (row // size) == (col // size)

    def each(fn, *lists):
        return [fn(*xs) for xs in zip(*lists)]

    def phase_b(i, tick):
        cs = [i * GROUP + g for g in range(GROUP)]
        sls = [pl.ds(c * CHUNK, CHUNK) for c in cs]
        at = [at_s[sl, :] for sl in sls]
        rt = [rt_s[sl, :] for sl in sls]
        kt = [kt_s[sl, :] for sl in sls]
        bt = [bt_s[sl, :] for sl in sls]
        v = [v_s[sl, :] for sl in sls]
        sc = each(lambda a, r, k, b: _dot_nt(jnp.concatenate([a, r], axis=0),
                                             jnp.concatenate([_stack(k), _stack(b)], axis=0)),
                  at, rt, kt, bt)
        a_ak = [jnp.where(strict, x[:CHUNK, :LANES], 0.0).astype(BF16) for x in sc]
        a_ab = [jnp.where(strict, x[:CHUNK, LANES:], 0.0) for x in sc]
        a_rk = [jnp.where(incl, x[CHUNK:, :LANES], 0.0).astype(BF16) for x in sc]
        a_rb = [jnp.where(incl, x[CHUNK:, LANES:], 0.0).astype(BF16) for x in sc]
        tick()
        d = [jnp.where(same_block(INV_BASE), x, 0.0) for x in a_ab]
        db = [x.astype(BF16) for x in d]
        d2 = [_dot(x, _stack(x)).astype(BF16) for x in db]
        tick()
        p = [eye_p + x for x in d]
        p = each(lambda pp, x2: pp + _dot(pp.astype(BF16), _stack(x2)), p, d2)
        tick()
        s = INV_BASE
        while s < CHUNK:
            lower_left = same_block(2 * s) & ((row // s) % 2 == 1) & ((col // s) % 2 == 0)
            a21 = [jnp.where(lower_left, x, 0.0).astype(BF16) for x in a_ab]
            pb = [x.astype(BF16) for x in p]
            a21_t11 = each(lambda a, t: _dot(a, _stack(t)).astype(BF16), a21, pb)
            tick()
            p = each(lambda pp, t, y: pp + _dot(t, _stack(y)), p, pb, a21_t11)
            tick()
            s *= 2
        tinv = [x.astype(BF16) for x in p]
        vst = [_stack(x) for x in v]
        x1 = each(lambda a, vs: _dot(a, vs).astype(BF16), a_ak, vst)
        tick()
        wu = each(lambda t, a, x: _dot(t, jnp.concatenate([_stack(a), _stack(x)], axis=1)),
                  tinv, at, x1)
        tick()
        wb = [x[:, :LANES].astype(BF16) for x in wu]
        ub = [x[:, LANES:].astype(BF16) for x in wu]
        qy = each(lambda a, w, u: _dot(a, jnp.concatenate([_stack(w), _stack(u)], axis=1)),
                  a_rb, wb, ub)
        tick()
        yb = each(lambda a, vs, x: _dot(a, vs) + x[:, LANES:], a_rk, vst, qy)
        tick()
        for g in range(GROUP):
            q_s[sls[g], :] = (rt[g].astype(F32) + qy[g][:, :LANES]).astype(BF16)
            yb_s[sls[g], :] = yb[g]
        kg = [kg_s[sl, :] for sl in sls]
        bg = [bg_s[sl, :] for sl in sls]
        mm = each(_dot_tn, wb, bg)
        tick()
        nn = each(lambda vv, u, k, b: _dot_tn(jnp.concatenate([vv, u], axis=0),
                                              jnp.concatenate([k, b], axis=0)), v, ub, kg, bg)
        tick()
        for g in range(GROUP):
            m_s[cs[g]] = jnp.where(bd_mask, mm[g], 0.0).astype(BF16)
            n_s[cs[g]] = jnp.where(bd_mask, nn[g], 0.0)

    def c1_step(c, s):
        sb = s.astype(BF16)
        sb_s[c] = sb
        return s * gc_s[c][0:1, :] + _dot(sb, m_s[c]) + n_s[c]

    n_groups = n_chunks // GROUP
    tiles_per_group = GROUP * CHUNK // ROW_TILE
    state = [jnp.zeros((LANES, LANES), F32)]

    def c1_thunk(c):
        def run():
            state[0] = c1_step(c, state[0])
        return run

    def prep_stages(group):
        per_tile = [phase_a_stages(group * tiles_per_group + t) for t in range(tiles_per_group)]
        return [stage for stages in zip(*per_tile) for stage in stages]

    def interleave(xs, ys):
        out = []
        for j in range(max(len(xs), len(ys))):
            out += xs[j:j + 1] + ys[j:j + 1]
        return out

    def run_with_fillers(body, fillers):
        def tick():
            if fillers:
                fillers.pop(0)()
        body(tick)
        while fillers:
            fillers.pop(0)()

    for stage in prep_stages(0):
        stage()
    for i in range(n_groups):
        fillers = interleave(
            [c1_thunk((i - 1) * GROUP + g) for g in range(GROUP)] if i > 0 else [],
            prep_stages(i + 1) if i + 1 < n_groups else [])
        run_with_fillers(functools.partial(phase_b, i), fillers)

    inv_n = 1.0 / RWKV_HEAD

    def phase_c2(i, tick):
        cs = [i * GROUP + g for g in range(GROUP)]
        sls = [pl.ds(c * CHUNK, CHUNK) for c in cs]

        def staged(fn, *lists):
            out = []
            for g, xs in enumerate(zip(*lists)):
                out.append(fn(*xs))
                if g % 4 == 3:
                    tick()
            return out

        y = staged(lambda sl, c: _dot_nt(q_s[sl, :], sb_s[c]) + yb_s[sl, :], sls, cs)
        mean = staged(lambda x: head_sum(x) * inv_n, y)
        yc = each(lambda a, b: a - b, y, mean)
        var = staged(lambda x: head_sum(x * x) * inv_n, yc)
        for g in range(GROUP):
            yn = yc[g] * lax.rsqrt(var[g] + LN_X_EPS) * vec(VEC_LNW) + vec(VEC_LNB)
            o_ref[0, sls[g], :] = ((yn + bonus_s[sls[g], :]) * g_s[sls[g], :]).astype(o_ref.dtype)

    tail = [c1_thunk((n_groups - 1) * GROUP + g) for g in range(GROUP)]
    for i in range(n_groups - 1):
        run_with_fillers(functools.partial(phase_c2, i), tail if i == n_groups - 2 else [])
    while tail:
        tail.pop(0)()
    phase_c2(n_groups - 1, lambda: None)


def _rwkv(p_rkv, p_sm, mu_rkv, mu_sm, w0, a0, k_k, k_a, r_k, ln_w, ln_b,
          w2p, a2p, g2p, bsz, seq):
    rw = w0.shape[-1]
    npair = rw // LANES
    lora_w, lora_a, lora_g = w2p.shape[0], a2p.shape[0], g2p.shape[0]
    wsm = lora_w + lora_a + lora_g
    p3 = p_rkv.reshape(bsz, seq, 3 * rw)
    ps = _lora_in(p_sm, mu_sm, lora_w, lora_a, bsz, seq)
    n_chunks = seq // CHUNK

    def col(off):
        return pl.BlockSpec((1, seq, LANES), lambda b, h: (b, 0, off + h))

    rows = [mu_rkv[0, :rw], mu_rkv[0, rw:2 * rw], mu_rkv[0, 2 * rw:],
            w0, a0, k_k, k_a, r_k.reshape(rw), ln_w, ln_b]
    vecs = jnp.concatenate([jnp.stack(rows), jnp.zeros((VEC_ROWS - len(rows), rw), F32)], axis=0)
    lora = jnp.concatenate([w2p, a2p, g2p], axis=0)

    return pl.pallas_call(
        functools.partial(_rwkv_kernel, seq=seq, lora_w=lora_w, lora_a=lora_a),
        grid=(bsz, npair),
        in_specs=[col(0), col(npair), col(2 * npair),
                  pl.BlockSpec((1, seq, wsm), lambda b, h: (b, 0, 0)),
                  pl.BlockSpec((VEC_ROWS, LANES), lambda b, h: (0, h)),
                  pl.BlockSpec((wsm, LANES), lambda b, h: (0, h))],
        out_specs=pl.BlockSpec((1, seq, LANES), lambda b, h: (b, 0, h)),
        out_shape=jax.ShapeDtypeStruct((bsz, seq, rw), BF16),
        scratch_shapes=(
            [pltpu.VMEM((seq, LANES), BF16) for _ in range(7)]
            + [pltpu.VMEM((seq, LANES), F32) for _ in range(2)]
            + [pltpu.VMEM((n_chunks, 8, LANES), F32),
               pltpu.VMEM((n_chunks, LANES, LANES), BF16),
               pltpu.VMEM((n_chunks, LANES, LANES), F32),
               pltpu.VMEM((seq, LANES), BF16),
               pltpu.VMEM((seq, LANES), F32),
               pltpu.VMEM((n_chunks, LANES, LANES), BF16)]),
        compiler_params=_params("parallel", "parallel"),
        name="rwkv7",
    )(p3, p3, p3, ps, vecs, lora)


def _fgate_kernel(f_ref, bf_ref, col_ref, row_ref, *, seq):
    nblk = seq // LANES
    ri = lax.broadcasted_iota(jnp.int32, (LANES, LANES), 0)
    ci = lax.broadcasted_iota(jnp.int32, (LANES, LANES), 1)
    tri = (ci <= ri).astype(BF16)

    def body(i, carry):
        t0 = pl.multiple_of(i * LANES, LANES)
        z = f_ref[0, pl.ds(t0, LANES), :] + bf_ref[...]
        logf = jnp.minimum(z, 0.0) - jnp.log1p(jnp.exp(-jnp.abs(z)))
        hi = logf.astype(BF16)
        r1 = logf - hi.astype(F32)
        mid = r1.astype(BF16)
        lo = (r1 - mid.astype(F32)).astype(BF16)
        cum = carry + (_dot(tri, hi) + _dot(tri, mid) + _dot(tri, lo))
        col_ref[0, pl.ds(t0, LANES), :] = cum
        row_ref[0, :, pl.ds(t0, LANES)] = cum.T
        return cum[LANES - 1:LANES, :]

    lax.fori_loop(0, nblk, body, jnp.zeros((1, LANES), F32))


def _fgate(p_sm, b_f_pad, off, bsz, seq):
    ps = p_sm.reshape(bsz, seq, p_sm.shape[-1])
    return pl.pallas_call(
        functools.partial(_fgate_kernel, seq=seq),
        grid=(bsz,),
        in_specs=[pl.BlockSpec((1, seq, LANES), lambda b: (b, 0, off // LANES)),
                  pl.BlockSpec((1, LANES), lambda b: (0, 0))],
        out_specs=[pl.BlockSpec((1, seq, LANES), lambda b: (b, 0, 0)),
                   pl.BlockSpec((1, LANES, seq), lambda b: (b, 0, 0))],
        out_shape=[jax.ShapeDtypeStruct((bsz, seq, LANES), F32),
                   jax.ShapeDtypeStruct((bsz, LANES, seq), F32)],
        compiler_params=_params("parallel"),
        name="fox_gates",
    )(ps, b_f_pad)


FOX_HEADS_PER_STEP = 2
LOG2E = 1.4426950408889634


def _fox_kernel(q_ref, k_ref, v_ref, ccol_ref, crow_ref, on_ref, o_ref, *, tq, seq, scale):
    hg = pl.program_id(1)
    heads = range(FOX_HEADS_PER_STEP)
    hsl = [slice(g * FOX_HEAD, (g + 1) * FOX_HEAD) for g in heads]
    lane = lax.broadcasted_iota(jnp.int32, (tq, LANES), 1)
    causal = (lax.broadcasted_iota(jnp.int32, (tq, tq), 1)
              <= lax.broadcasted_iota(jnp.int32, (tq, tq), 0))
    ones_col = jnp.where(lane == 0, 1.0, 0.0).astype(BF16)

    for i in range(seq // tq):
        qrows = pl.ds(i * tq, tq)
        q = [q_ref[0, qrows, sl] for sl in hsl]
        ccol = ccol_ref[0, qrows, :]
        cq = [LOG2E * jnp.sum(jnp.where(lane == hg * FOX_HEADS_PER_STEP + g, ccol, 0.0),
                              axis=-1, keepdims=True) for g in heads]

        def scores(j, q=q, cq=cq, i=i):
            krows = pl.ds(j * tq, tq)
            s = [_dot_nt(q[g], k_ref[0, krows, hsl[g]]) * (scale * LOG2E)
                 + cq[g] - LOG2E * crow_ref[0, g, :, krows] for g in heads]
            if j == i:
                s = [jnp.where(causal, x, -jnp.inf) for x in s]
            return s

        m = [jnp.full((tq, 1), -jnp.inf, F32) for _ in heads]
        acc = [jnp.zeros((tq, 2 * FOX_HEAD), F32) for _ in heads]
        s = scores(0)
        for j in range(i + 1):
            s_next = scores(j + 1) if j < i else None
            vrows = pl.ds(j * tq, tq)
            m_new = [jnp.maximum(m[g], jnp.max(s[g], axis=-1, keepdims=True)) for g in heads]
            alpha = [jnp.exp2(m[g] - m_new[g]) for g in heads]
            p = [jnp.exp2(s[g] - m_new[g]).astype(BF16) for g in heads]
            acc = [alpha[g] * acc[g]
                   + _dot(p[g], jnp.concatenate([v_ref[0, vrows, hsl[g]], ones_col], axis=1))
                   for g in heads]
            m, s = m_new, s_next
        for g in heads:
            o = acc[g][:, :FOX_HEAD] / acc[g][:, FOX_HEAD:FOX_HEAD + 1]
            o = o * lax.rsqrt(jnp.mean(o * o, axis=-1, keepdims=True) + NORM_EPS) * on_ref[g]
            o_ref[0, qrows, hsl[g]] = o.astype(o_ref.dtype)


def _fox(p_fox, ccol, crow, out_norm, bsz, seq):
    fw = p_fox.shape[-1] // 3
    nh = fw // FOX_HEAD
    hps = FOX_HEADS_PER_STEP
    ng = nh // hps
    wid = hps * FOX_HEAD
    tq = _pick(seq, (512, 256, 128))
    p3 = p_fox.reshape(bsz, seq, 3 * fw)
    crow4 = crow.reshape(bsz, LANES, 1, seq)
    on = out_norm.reshape(nh, 1, FOX_HEAD)
    return pl.pallas_call(
        functools.partial(_fox_kernel, tq=tq, seq=seq, scale=FOX_HEAD ** -0.5),
        grid=(bsz, ng),
        in_specs=[pl.BlockSpec((1, seq, wid), lambda b, h: (b, 0, h)),
                  pl.BlockSpec((1, seq, wid), lambda b, h: (b, 0, ng + h)),
                  pl.BlockSpec((1, seq, wid), lambda b, h: (b, 0, 2 * ng + h)),
                  pl.BlockSpec((1, seq, LANES), lambda b, h: (b, 0, 0)),
                  pl.BlockSpec((1, hps, 1, seq), lambda b, h: (b, h, 0, 0)),
                  pl.BlockSpec((hps, 1, FOX_HEAD), lambda b, h: (h, 0, 0))],
        out_specs=pl.BlockSpec((1, seq, wid), lambda b, h: (b, 0, h)),
        out_shape=jax.ShapeDtypeStruct((bsz, seq, fw), BF16),
        compiler_params=_params("parallel", "parallel"),
        name="fox_attention",
    )(p3, p3, p3, ccol, crow4, on)


def _pad_rows(w, rows):
    return jnp.pad(w, ((0, rows - w.shape[0]), (0, 0)))


def _cast_pad_kernel(w_ref, o_ref, *, n_valid):
    @pl.when(pl.program_id(0) < n_valid)
    def _():
        o_ref[...] = w_ref[...].astype(o_ref.dtype)

    @pl.when(pl.program_id(0) >= n_valid)
    def _():
        o_ref[...] = jnp.zeros_like(o_ref)


def _cast_pad_rows(w, rows_out):
    rows, cols = w.shape
    tr = _pick(rows, (256, 128, 64, 16))
    assert rows_out % tr == 0
    n_valid = rows // tr
    return pl.pallas_call(
        functools.partial(_cast_pad_kernel, n_valid=n_valid),
        grid=(rows_out // tr,),
        in_specs=[pl.BlockSpec((tr, cols), lambda i: (jnp.minimum(i, n_valid - 1), 0))],
        out_specs=pl.BlockSpec((tr, cols), lambda i: (i, 0)),
        out_shape=jax.ShapeDtypeStruct((rows_out, cols), BF16),
        compiler_params=_params("parallel"),
        name="cast_pad_rows",
    )(w)


def _ffn(x2, h, w_gate, w_up, w_down, gate, seq):
    dff = w_gate.shape[1]
    dffp = _round_up(dff, 512)
    wd = _cast_pad_rows(w_down, dffp)
    act = _swiglu_up(h, w_gate, w_up, dffp)
    return _matmul_residual(act, wd, x2, gate, seq, 0.5)


def kernel(x, c, w_mod, b_mod, norm_ffn1, ffn1_gate, ffn1_up, ffn1_down, norm_mix, w_in, rwkv_mu, rwkv_w0, rwkv_w2, rwkv_a0, rwkv_a2, rwkv_g2, rwkv_k_k, rwkv_k_a, rwkv_r_k, rwkv_ln_w, rwkv_ln_b, fox_b_f, fox_out_norm, w_out, norm_ffn2, ffn2_gate, ffn2_up, ffn2_down, norm_final):
    bsz, seq, d = x.shape
    depth = w_mod.shape[0]
    rw = rwkv_w0.shape[-1]
    fw = fox_out_norm.shape[-1]
    nfh = fox_b_f.shape[-1]
    dl, al, gl = rwkv_w2.shape[1], rwkv_a2.shape[1], rwkv_g2.shape[1]
    dlp, alp, glp = (_round_up(n, LANES) for n in (dl, al, gl))
    c3 = 3 * rw
    c4, c5, c6 = c3 + dl, c3 + dl + al, c3 + dl + al + gl

    x2 = x.reshape(bsz * seq, d)
    for l in range(depth):
        mod = _mod(c, w_mod[l], b_mod[l])
        sh1, sc1, gt1, sh2, sc2, gt2, sh3, sc3, gt3 = (
            m[:, None, :] for m in jnp.split(mod, 9, axis=-1))

        h = _norm_mod(x2, norm_ffn1[l], sc1, sh1, seq)
        x2 = _ffn(x2, h, ffn1_gate[l], ffn1_up[l], ffn1_down[l], gt1, seq)

        h = _norm_mod(x2, norm_mix[l], sc2, sh2, seq)
        wi, mu = w_in[l], rwkv_mu[l]
        wt = jnp.swapaxes(wi, 0, 1).astype(BF16)
        wt_sm = jnp.concatenate(
            [_pad_rows(wt[c3:c4], dlp), _pad_rows(wt[c4:c5], alp),
             _pad_rows(wt[c5:c6], glp), _pad_rows(wt[c6 + 3 * fw:], LANES)], axis=0)
        mu_rkv = mu[:c3].reshape(1, c3)
        mu_sm = jnp.concatenate(
            [jnp.pad(mu[c3:c4], (0, dlp - dl)), jnp.pad(mu[c4:c5], (0, alp - al)),
             jnp.pad(mu[c5:c6], (0, glp - gl))]).reshape(1, dlp + alp + glp)
        p_rkv = _matmul_nt(h, wt, F32, 0, c3)
        p_sm = _matmul_nt(h, wt_sm, F32)
        p_fox = _matmul_nt(h, wt, BF16, c6, 3 * fw)

        y_r = _rwkv(p_rkv, p_sm, mu_rkv, mu_sm,
                    rwkv_w0[l], rwkv_a0[l], rwkv_k_k[l], rwkv_k_a[l], rwkv_r_k[l],
                    rwkv_ln_w[l], rwkv_ln_b[l],
                    _pad_rows(rwkv_w2[l], dlp).astype(BF16),
                    _pad_rows(rwkv_a2[l], alp).astype(BF16),
                    _pad_rows(rwkv_g2[l], glp).astype(BF16), bsz, seq)

        b_f_pad = jnp.pad(fox_b_f[l], (0, LANES - nfh)).reshape(1, LANES)
        ccol, crow = _fgate(p_sm, b_f_pad, dlp + alp + glp, bsz, seq)
        y_f = _fox(p_fox, ccol, crow, fox_out_norm[l], bsz, seq)

        x2 = _matmul_residual2(y_r.reshape(bsz * seq, rw), y_f.reshape(bsz * seq, fw),
                               w_out[l].astype(BF16), x2, gt2, seq, 1.0)

        h = _norm_mod(x2, norm_ffn2[l], sc3, sh3, seq)
        x2 = _ffn(x2, h, ffn2_gate[l], ffn2_up[l], ffn2_down[l], gt3, seq)
    return _final_norm(x2, norm_final).reshape(bsz, seq, d)
```

```python
import functools

import jax
import jax.numpy as jnp
from jax import lax
from jax.experimental import pallas as pl
from jax.experimental.pallas import tpu as pltpu

F32 = jnp.float32
BF16 = jnp.bfloat16

LANES = 128
RWKV_HEAD = 64
FOX_HEAD = 128
CHUNK = 64
INV_BASE = 4
GROUP = 16
NORM_EPS = 1e-6
LN_X_EPS = 64e-5
VMEM_LIMIT = 56 * 1024 * 1024


def _round_up(n, m):
    return (n + m - 1) // m * m


def _pick(n, candidates):
    for c in candidates:
        if n % c == 0:
            return c
    return n


def _dot(a, b):
    return jnp.dot(a, b, preferred_element_type=F32)


def _dot_nt(a, b):
    return lax.dot_general(a, b, (((1,), (1,)), ((), ())), preferred_element_type=F32)


def _dot_tn(a, b):
    return lax.dot_general(a, b, (((0,), (0,)), ((), ())), preferred_element_type=F32)


def _params(*sem):
    return pltpu.CompilerParams(dimension_semantics=sem, vmem_limit_bytes=VMEM_LIMIT)


def _mod_kernel(c_ref, w_ref, b_ref, o_ref):
    c = c_ref[...]
    a = (c * jax.nn.sigmoid(c)).astype(BF16)
    o_ref[...] = _dot(a, w_ref[...].astype(BF16)) + b_ref[...]


def _mod(c, w_mod, b_mod):
    bsz, d = c.shape
    n = w_mod.shape[1]
    rows = _round_up(bsz, 16)
    c_pad = jnp.pad(c, ((0, rows - bsz), (0, 0)))
    tn = _pick(n, (512, 256, 128))
    out = pl.pallas_call(
        _mod_kernel,
        grid=(n // tn,),
        in_specs=[pl.BlockSpec((rows, d), lambda j: (0, 0)),
                  pl.BlockSpec((d, tn), lambda j: (0, j)),
                  pl.BlockSpec((1, tn), lambda j: (0, j))],
        out_specs=pl.BlockSpec((rows, tn), lambda j: (0, j)),
        out_shape=jax.ShapeDtypeStruct((rows, n), F32),
        compiler_params=_params("parallel"),
        name="mod_matmul",
    )(c_pad, w_mod, b_mod.reshape(1, n))
    return out[:bsz]


def _norm_kernel(x_ref, g_ref, sc_ref, sh_ref, o_ref):
    x = x_ref[...]
    y = x * lax.rsqrt(jnp.mean(x * x, axis=-1, keepdims=True) + NORM_EPS) * g_ref[...]
    o_ref[...] = (y * (1.0 + sc_ref[0]) + sh_ref[0]).astype(o_ref.dtype)


def _final_norm_kernel(x_ref, g_ref, o_ref):
    x = x_ref[...]
    o_ref[...] = x * lax.rsqrt(jnp.mean(x * x, axis=-1, keepdims=True) + NORM_EPS) * g_ref[...]


def _norm_mod(x2, g, sc, sh, seq):
    t, d = x2.shape
    ts = _pick(seq, (512, 256, 128, 64, 8))
    return pl.pallas_call(
        _norm_kernel,
        grid=(t // ts,),
        in_specs=[pl.BlockSpec((ts, d), lambda i: (i, 0)),
                  pl.BlockSpec((1, d), lambda i: (0, 0)),
                  pl.BlockSpec((1, 1, d), lambda i: (i * ts // seq, 0, 0)),
                  pl.BlockSpec((1, 1, d), lambda i: (i * ts // seq, 0, 0))],
        out_specs=pl.BlockSpec((ts, d), lambda i: (i, 0)),
        out_shape=jax.ShapeDtypeStruct((t, d), BF16),
        compiler_params=_params("parallel"),
        name="norm_mod",
    )(x2, g.reshape(1, d), sc, sh)


def _final_norm(x2, g):
    t, d = x2.shape
    ts = _pick(t, (512, 256, 128, 64, 8))
    return pl.pallas_call(
        _final_norm_kernel,
        grid=(t // ts,),
        in_specs=[pl.BlockSpec((ts, d), lambda i: (i, 0)),
                  pl.BlockSpec((1, d), lambda i: (0, 0))],
        out_specs=pl.BlockSpec((ts, d), lambda i: (i, 0)),
        out_shape=jax.ShapeDtypeStruct((t, d), F32),
        compiler_params=_params("parallel"),
        name="final_norm",
    )(x2, g.reshape(1, d))


def _mm_nt_kernel(x_ref, wt_ref, o_ref):
    o_ref[...] = _dot_nt(x_ref[...], wt_ref[...]).astype(o_ref.dtype)


def _matmul_nt(x, wt, out_dtype, row0=0, n=None):
    m, kd = x.shape
    n = wt.shape[0] if n is None else n
    tm = _pick(m, (1024, 512, 256, 128))
    tn = _pick(n, (1024, 512, 640, 256, 128))
    return pl.pallas_call(
        _mm_nt_kernel,
        grid=(m // tm, n // tn),
        in_specs=[pl.BlockSpec((tm, kd), lambda i, j: (i, 0)),
                  pl.BlockSpec((pl.Element(tn), pl.Element(kd)),
                               lambda i, j: ((row0 // 16 + j * (tn // 16)) * 16, 0))],
        out_specs=pl.BlockSpec((tm, tn), lambda i, j: (i, j)),
        out_shape=jax.ShapeDtypeStruct((m, n), out_dtype),
        compiler_params=_params("parallel", "parallel"),
        name="matmul_nt",
    )(x, wt)


def _swiglu_kernel(x_ref, wg_hbm, wu_hbm, o_ref, stage_ref, wgb_ref, wub_ref, sem,
                   *, n_full, rem, tn):
    j = pl.program_id(0)

    def full_copies(jj):
        cols = pl.ds(pl.multiple_of(jj * tn, tn), tn)
        return (pltpu.make_async_copy(wg_hbm.at[:, cols], stage_ref.at[0], sem.at[0]),
                pltpu.make_async_copy(wu_hbm.at[:, cols], stage_ref.at[1], sem.at[1]))

    def tail_copies():
        cols = pl.ds(n_full * tn, rem)
        return (pltpu.make_async_copy(wg_hbm.at[:, cols], stage_ref.at[0, :, :rem], sem.at[0]),
                pltpu.make_async_copy(wu_hbm.at[:, cols], stage_ref.at[1, :, :rem], sem.at[1]))

    def start(jj):
        @pl.when(jj < n_full)
        def _():
            for c in full_copies(jj):
                c.start()

        if rem:
            @pl.when(jj == n_full)
            def _():
                for c in tail_copies():
                    c.start()

    @pl.when(pl.program_id(1) == 0)
    def _():
        @pl.when(j == 0)
        def _():
            start(j)

        @pl.when(j < n_full)
        def _():
            for c in full_copies(j):
                c.wait()
            wgb_ref[...] = stage_ref[0].astype(BF16)
            wub_ref[...] = stage_ref[1].astype(BF16)

        if rem:
            @pl.when(j == n_full)
            def _():
                for c in tail_copies():
                    c.wait()
                wgb_ref[:, :rem] = stage_ref[0, :, :rem].astype(BF16)
                wub_ref[:, :rem] = stage_ref[1, :, :rem].astype(BF16)

        start(j + 1)

    def compute(ncols):
        x = x_ref[...]
        g = _dot(x, wgb_ref[:, :ncols])
        u = _dot(x, wub_ref[:, :ncols])
        o_ref[:, :ncols] = (g * (0.5 * jnp.tanh(0.5 * g) + 0.5) * u).astype(o_ref.dtype)
        if ncols < tn:
            o_ref[:, ncols:] = jnp.zeros((o_ref.shape[0], tn - ncols), o_ref.dtype)

    if rem:
        pl.when(j < n_full)(lambda: compute(tn))
        pl.when(j == n_full)(lambda: compute(rem))
    else:
        compute(tn)


def _swiglu_up(h, wg, wu, n_out):
    m, kd = h.shape
    n = wg.shape[1]
    tm = _pick(m, (1024, 512, 256, 128))
    tn = 512
    n_full, rem = divmod(n, tn)
    assert rem % LANES == 0 and n_out == _round_up(n, tn)
    return pl.pallas_call(
        functools.partial(_swiglu_kernel, n_full=n_full, rem=rem, tn=tn),
        grid=(n_out // tn, m // tm),
        in_specs=[pl.BlockSpec((tm, kd), lambda j, i: (i, 0)),
                  pl.BlockSpec(memory_space=pl.ANY),
                  pl.BlockSpec(memory_space=pl.ANY)],
        out_specs=pl.BlockSpec((tm, tn), lambda j, i: (i, j)),
        out_shape=jax.ShapeDtypeStruct((m, n_out), BF16),
        scratch_shapes=[pltpu.VMEM((2, kd, tn), F32),
                        pltpu.VMEM((kd, tn), BF16), pltpu.VMEM((kd, tn), BF16),
                        pltpu.SemaphoreType.DMA((2,))],
        compiler_params=_params("arbitrary", "arbitrary"),
        name="swiglu_up",
    )(h, wg, wu)


def _resid_kernel(a_ref, w_ref, x_ref, gt_ref, o_ref, acc_ref, *, nk, k_last, scale):
    k = pl.program_id(2)

    @pl.when(k == 0)
    def _():
        acc_ref[...] = _dot(a_ref[...], w_ref[...])

    @pl.when((k > 0) & (k < nk - 1))
    def _():
        acc_ref[...] += _dot(a_ref[...], w_ref[...])

    @pl.when(k == nk - 1)
    def _():
        y = acc_ref[...] + _dot(a_ref[:, :k_last], w_ref[:k_last, :])
        o_ref[...] = x_ref[...] + (scale * gt_ref[0]) * y


def _matmul_residual(a, w, x2, gate, seq, scale, k_valid):
    m, kd = a.shape
    n = w.shape[1]
    tm = _pick(seq, (1024, 512, 256, 128))
    tn = _pick(n, (1024, 512, 256, 128))
    tk = _pick(kd, (2816, 1024, 512, 256, 128))
    nk = kd // tk
    k_last = k_valid - (nk - 1) * tk
    assert nk >= 2 and 0 < k_last <= tk and k_last % LANES == 0
    return pl.pallas_call(
        functools.partial(_resid_kernel, nk=nk, k_last=k_last, scale=scale),
        grid=(m // tm, n // tn, nk),
        in_specs=[pl.BlockSpec((tm, tk), lambda i, j, k: (i, k)),
                  pl.BlockSpec((tk, tn), lambda i, j, k: (k, j)),
                  pl.BlockSpec((tm, tn), lambda i, j, k: (i, j)),
                  pl.BlockSpec((1, 1, tn), lambda i, j, k: (i * tm // seq, 0, j))],
        out_specs=pl.BlockSpec((tm, tn), lambda i, j, k: (i, j)),
        out_shape=jax.ShapeDtypeStruct((m, n), F32),
        scratch_shapes=[pltpu.VMEM((tm, tn), F32)],
        compiler_params=_params("parallel", "parallel", "arbitrary"),
        name="matmul_residual",
    )(a, w, x2, gate)


def _resid2_kernel(a1_ref, a2_ref, w1_ref, w2_ref, x_ref, gt_ref, o_ref, *, scale):
    y = _dot(a1_ref[...], w1_ref[...]) + _dot(a2_ref[...], w2_ref[...])
    o_ref[...] = x_ref[...] + (scale * gt_ref[0]) * y


def _matmul_residual2(a1, a2, w, x2, gate, seq, scale):
    m, k1 = a1.shape
    k2 = a2.shape[1]
    n = w.shape[1]
    assert k1 == k2
    tm = _pick(seq, (1024, 512, 256, 128))
    tn = _pick(n, (512, 256, 128))
    return pl.pallas_call(
        functools.partial(_resid2_kernel, scale=scale),
        grid=(m // tm, n // tn),
        in_specs=[pl.BlockSpec((tm, k1), lambda i, j: (i, 0)),
                  pl.BlockSpec((tm, k2), lambda i, j: (i, 0)),
                  pl.BlockSpec((k1, tn), lambda i, j: (0, j)),
                  pl.BlockSpec((k2, tn), lambda i, j: (1, j)),
                  pl.BlockSpec((tm, tn), lambda i, j: (i, j)),
                  pl.BlockSpec((1, 1, tn), lambda i, j: (i * tm // seq, 0, j))],
        out_specs=pl.BlockSpec((tm, tn), lambda i, j: (i, j)),
        out_shape=jax.ShapeDtypeStruct((m, n), F32),
        compiler_params=_params("parallel", "parallel"),
        name="out_proj_residual",
    )(a1, a2, w, w, x2, gate)


ROW_TILE = 512


def _lane_head0(shape):
    return lax.broadcasted_iota(jnp.int32, shape, 1) < RWKV_HEAD


def _stack(x):
    m0 = _lane_head0(x.shape)
    zero = jnp.zeros_like(x)
    return jnp.concatenate([jnp.where(m0, x, zero), jnp.where(m0, zero, x)], axis=0)


def _split_dot(lhs_bf16, x):
    hi = x.astype(BF16)
    lo = (x - hi.astype(F32)).astype(BF16)
    return _dot(lhs_bf16, hi) + _dot(lhs_bf16, lo)


def _lora_in_kernel(ps_ref, prev_ref, mu_ref, o_ref, *, lora_w, lora_a):
    x = ps_ref[0]
    prev_row = jnp.where(pl.program_id(1) == 0, 0.0, prev_ref[0][7:8, :])
    row = lax.broadcasted_iota(jnp.int32, x.shape, 0)
    prev = jnp.where(row == 0, prev_row, pltpu.roll(x, 1, 0))
    sm = x + (prev - x) * mu_ref[...]
    o_ref[0, :, :lora_w] = jnp.tanh(sm[:, :lora_w]).astype(o_ref.dtype)
    o_ref[0, :, lora_w:lora_w + lora_a] = sm[:, lora_w:lora_w + lora_a].astype(o_ref.dtype)
    o_ref[0, :, lora_w + lora_a:] = jax.nn.sigmoid(sm[:, lora_w + lora_a:]).astype(o_ref.dtype)


def _lora_in(p_sm, mu_sm, lora_w, lora_a, bsz, seq):
    wsm = mu_sm.shape[-1]
    ps = p_sm.reshape(bsz, seq, p_sm.shape[-1])
    ts = _pick(seq, (ROW_TILE, 128, 64, 8))
    return pl.pallas_call(
        functools.partial(_lora_in_kernel, lora_w=lora_w, lora_a=lora_a),
        grid=(bsz, seq // ts),
        in_specs=[pl.BlockSpec((1, ts, wsm), lambda b, i: (b, i, 0)),
                  pl.BlockSpec((1, 8, wsm), lambda b, i: (b, jnp.maximum(i * (ts // 8) - 1, 0), 0)),
                  pl.BlockSpec((1, wsm), lambda b, i: (0, 0))],
        out_specs=pl.BlockSpec((1, ts, wsm), lambda b, i: (b, i, 0)),
        out_shape=jax.ShapeDtypeStruct((bsz, seq, wsm), BF16),
        compiler_params=_params("parallel", "parallel"),
        name="rwkv_lora_in",
    )(ps, ps, mu_sm)


(VEC_MU_R, VEC_MU_K, VEC_MU_V, VEC_W0, VEC_A0, VEC_KK, VEC_KA, VEC_RK, VEC_LNW,
 VEC_LNB) = range(10)
VEC_ROWS = 16


def _rwkv_kernel(pr_ref, pk_ref, pv_ref, ps_ref, vec_ref, lora_ref, o_ref,
                 at_s, rt_s, kt_s, bt_s, kg_s, bg_s, v_s, g_s, bonus_s, gc_s,
                 m_s, n_s, q_s, yb_s, sb_s, *, seq, lora_w, lora_a):
    n_chunks = seq // CHUNK
    cpt = ROW_TILE // CHUNK

    def vec(idx):
        return vec_ref[idx:idx + 1, :]

    ri = lax.broadcasted_iota(jnp.int32, (LANES, LANES), 0)
    ci = lax.broadcasted_iota(jnp.int32, (LANES, LANES), 1)
    bd_mask = (ri // RWKV_HEAD) == (ci // RWKV_HEAD)
    bd_ones = bd_mask.astype(BF16)
    tr = lax.broadcasted_iota(jnp.int32, (ROW_TILE, ROW_TILE), 0)
    tc = lax.broadcasted_iota(jnp.int32, (ROW_TILE, ROW_TILE), 1)
    tri = ((tc <= tr) & ((tr // CHUNK) == (tc // CHUNK))).astype(BF16)

    def head_sum(x):
        return _dot(x.astype(BF16), bd_ones)

    def shift_mix(ref, t0, mu):
        x = ref[0, pl.ds(t0, ROW_TILE), :]
        if t0 == 0:
            prev_row = jnp.zeros((1, x.shape[1]), F32)
        else:
            prev_row = ref[0, pl.ds(t0 - 8, 8), :][7:8, :]
        row = lax.broadcasted_iota(jnp.int32, x.shape, 0)
        prev = jnp.where(row == 0, prev_row, pltpu.roll(x, 1, 0))
        return x + (prev - x) * mu

    def phase_a_stages(i):
        t0 = i * ROW_TILE
        rows = pl.ds(t0, ROW_TILE)
        st = {}

        def load_and_lora():
            st["r"] = shift_mix(pr_ref, t0, vec(VEC_MU_R))
            st["k"] = shift_mix(pk_ref, t0, vec(VEC_MU_K))
            st["v"] = shift_mix(pv_ref, t0, vec(VEC_MU_V))
            st["wlin"] = vec(VEC_W0) + _dot(ps_ref[0, rows, :lora_w], lora_ref[:lora_w, :])
            st["alin"] = vec(VEC_A0) + _dot(ps_ref[0, rows, lora_w:lora_w + lora_a],
                                            lora_ref[lora_w:lora_w + lora_a, :])
            st["g"] = _dot(ps_ref[0, rows, lora_w + lora_a:], lora_ref[lora_w + lora_a:, :])

        def decay_and_norm():
            wlin = st.pop("wlin")
            wlog = -(jnp.maximum(-wlin, 0.0) + jnp.log(1.0 + jnp.exp(-jnp.abs(wlin)))) - 0.5
            st["ld"] = -jnp.exp(wlog)
            st["alr"] = jax.nn.sigmoid(st.pop("alin"))
            st["kk"] = st["k"] * vec(VEC_KK)
            st["kk_ss"] = head_sum(st["kk"] * st["kk"])
            st["cum"] = _split_dot(tri, st["ld"])

        def keys_and_bonus():
            st["kk"] = st["kk"] * lax.rsqrt(jnp.maximum(st.pop("kk_ss"), 1e-24))
            st["k"] = st["k"] * (1.0 + (st["alr"] - 1.0) * vec(VEC_KA))
            st["bonus_sum"] = head_sum(st["r"] * st["k"] * vec(VEC_RK))
            st["b"] = st["kk"] * st["alr"]

        def scale_and_store():
            cum, ld = st["cum"], st.pop("ld")
            g_dec = jnp.exp(cum)
            g_inv = jnp.exp(-cum)
            at_s[rows, :] = (-st.pop("kk") * jnp.exp(cum - ld)).astype(BF16)
            rt_s[rows, :] = (st.pop("r") * g_dec).astype(BF16)
            kt_s[rows, :] = (st["k"] * g_inv).astype(BF16)
            bt_s[rows, :] = (st["b"] * g_inv).astype(BF16)
            v_s[rows, :] = st["v"].astype(BF16)
            g_s[rows, :] = st.pop("g")
            bonus_s[rows, :] = st.pop("bonus_sum") * st.pop("v")

        def chunk_tails():
            cum, k, b = st.pop("cum"), st.pop("k"), st.pop("b")
            for c in range(cpt):
                lo, hi = c * CHUNK, (c + 1) * CHUNK
                cum_c = cum[hi - 1:hi, :]
                rem = jnp.exp(cum_c - cum[lo:hi, :])
                kg_s[pl.ds(t0 + lo, CHUNK), :] = (k[lo:hi, :] * rem).astype(BF16)
                bg_s[pl.ds(t0 + lo, CHUNK), :] = (b[lo:hi, :] * rem).astype(BF16)
                gc_s[i * cpt + c] = jnp.broadcast_to(jnp.exp(cum_c), (8, LANES))

        return [load_and_lora, decay_and_norm, keys_and_bonus, scale_and_store, chunk_tails]

    row = lax.broadcasted_iota(jnp.int32, (CHUNK, LANES), 0)
    col = lax.broadcasted_iota(jnp.int32, (CHUNK, LANES), 1) % RWKV_HEAD
    strict = col < row
    incl = col <= row
    eye_p = (col == row).astype(F32)

    def same_block(size):
        return (row // size) == (col // size)

    def each(fn, *lists):
        return [fn(*xs) for xs in zip(*lists)]

    def phase_b(i, tick):
        cs = [i * GROUP + g for g in range(GROUP)]
        sls = [pl.ds(c * CHUNK, CHUNK) for c in cs]
        at = [at_s[sl, :] for sl in sls]
        rt = [rt_s[sl, :] for sl in sls]
        kt = [kt_s[sl, :] for sl in sls]
        bt = [bt_s[sl, :] for sl in sls]
        v = [v_s[sl, :] for sl in sls]
        sc = each(lambda a, r, k, b: _dot_nt(jnp.concatenate([a, r], axis=0),
                                             jnp.concatenate([_stack(k), _stack(b)], axis=0)),
                  at, rt, kt, bt)
        a_ak = [jnp.where(strict, x[:CHUNK, :LANES], 0.0).astype(BF16) for x in sc]
        a_ab = [jnp.where(strict, x[:CHUNK, LANES:], 0.0) for x in sc]
        a_rk = [jnp.where(incl, x[CHUNK:, :LANES], 0.0).astype(BF16) for x in sc]
        a_rb = [jnp.where(incl, x[CHUNK:, LANES:], 0.0).astype(BF16) for x in sc]
        tick()
        d = [jnp.where(same_block(INV_BASE), x, 0.0) for x in a_ab]
        db = [x.astype(BF16) for x in d]
        d2 = [_dot(x, _stack(x)).astype(BF16) for x in db]
        tick()
        p = [eye_p + x for x in d]
        p = each(lambda pp, x2: pp + _dot(pp.astype(BF16), _stack(x2)), p, d2)
        tick()
        s = INV_BASE
        while s < CHUNK:
            lower_left = same_block(2 * s) & ((row // s) % 2 == 1) & ((col // s) % 2 == 0)
            a21 = [jnp.where(lower_left, x, 0.0).astype(BF16) for x in a_ab]
            pb = [x.astype(BF16) for x in p]
            a21_t11 = each(lambda a, t: _dot(a, _stack(t)).astype(BF16), a21, pb)
            tick()
            p = each(lambda pp, t, y: pp + _dot(t, _stack(y)), p, pb, a21_t11)
            tick()
            s *= 2
        tinv = [x.astype(BF16) for x in p]
        vst = [_stack(x) for x in v]
        x1 = each(lambda a, vs: _dot(a, vs).astype(BF16), a_ak, vst)
        tick()
        wu = each(lambda t, a, x: _dot(t, jnp.concatenate([_stack(a), _stack(x)], axis=1)),
                  tinv, at, x1)
        tick()
        wb = [x[:, :LANES].astype(BF16) for x in wu]
        ub = [x[:, LANES:].astype(BF16) for x in wu]
        qy = each(lambda a, w, u: _dot(a, jnp.concatenate([_stack(w), _stack(u)], axis=1)),
                  a_rb, wb, ub)
        tick()
        yb = each(lambda a, vs, x: _dot(a, vs) + x[:, LANES:], a_rk, vst, qy)
        tick()
        for g in range(GROUP):
            q_s[sls[g], :] = (rt[g].astype(F32) + qy[g][:, :LANES]).astype(BF16)
            yb_s[sls[g], :] = yb[g]
        kg = [kg_s[sl, :] for sl in sls]
        bg = [bg_s[sl, :] for sl in sls]
        mm = each(_dot_tn, wb, bg)
        tick()
        nn = each(lambda vv, u, k, b: _dot_tn(jnp.concatenate([vv, u], axis=0),
                                              jnp.concatenate([k, b], axis=0)), v, ub, kg, bg)
        tick()
        for g in range(GROUP):
            m_s[cs[g]] = jnp.where(bd_mask, mm[g], 0.0).astype(BF16)
            n_s[cs[g]] = jnp.where(bd_mask, nn[g], 0.0)

    def c1_step(c, s):
        sb = s.astype(BF16)
        sb_s[c] = sb
        return s * gc_s[c][0:1, :] + _dot(sb, m_s[c]) + n_s[c]

    n_groups = n_chunks // GROUP
    tiles_per_group = GROUP * CHUNK // ROW_TILE
    state = [jnp.zeros((LANES, LANES), F32)]

    def c1_thunk(c):
        def run():
            state[0] = c1_step(c, state[0])
        return run

    def prep_stages(group):
        per_tile = [phase_a_stages(group * tiles_per_group + t) for t in range(tiles_per_group)]
        return [stage for stages in zip(*per_tile) for stage in stages]

    def interleave(xs, ys):
        out = []
        for j in range(max(len(xs), len(ys))):
            out += xs[j:j + 1] + ys[j:j + 1]
        return out

    def run_with_fillers(body, fillers):
        def tick():
            if fillers:
                fillers.pop(0)()
        body(tick)
        while fillers:
            fillers.pop(0)()

    for stage in prep_stages(0):
        stage()
    for i in range(n_groups):
        fillers = interleave(
            [c1_thunk((i - 1) * GROUP + g) for g in range(GROUP)] if i > 0 else [],
            prep_stages(i + 1) if i + 1 < n_groups else [])
        run_with_fillers(functools.partial(phase_b, i), fillers)

    inv_n = 1.0 / RWKV_HEAD

    def phase_c2(i, tick):
        cs = [i * GROUP + g for g in range(GROUP)]
        sls = [pl.ds(c * CHUNK, CHUNK) for c in cs]

        def staged(fn, *lists):
            out = []
            for g, xs in enumerate(zip(*lists)):
                out.append(fn(*xs))
                if g % 4 == 3:
                    tick()
            return out

        y = staged(lambda sl, c: _dot_nt(q_s[sl, :], sb_s[c]) + yb_s[sl, :], sls, cs)
        mean = staged(lambda x: head_sum(x) * inv_n, y)
        yc = each(lambda a, b: a - b, y, mean)
        var = staged(lambda x: head_sum(x * x) * inv_n, yc)
        for g in range(GROUP):
            yn = yc[g] * lax.rsqrt(var[g] + LN_X_EPS) * vec(VEC_LNW) + vec(VEC_LNB)
            o_ref[0, sls[g], :] = ((yn + bonus_s[sls[g], :]) * g_s[sls[g], :]).astype(o_ref.dtype)

    tail = [c1_thunk((n_groups - 1) * GROUP + g) for g in range(GROUP)]
    for i in range(n_groups - 1):
        run_with_fillers(functools.partial(phase_c2, i), tail if i == n_groups - 2 else [])
    while tail:
        tail.pop(0)()
    phase_c2(n_groups - 1, lambda: None)


def _rwkv(p_rkv, p_sm, mu_rkv, mu_sm, w0, a0, k_k, k_a, r_k, ln_w, ln_b,
          w2p, a2p, g2p, bsz, seq):
    rw = w0.shape[-1]
    npair = rw // LANES
    lora_w, lora_a, lora_g = w2p.shape[0], a2p.shape[0], g2p.shape[0]
    wsm = lora_w + lora_a + lora_g
    p3 = p_rkv.reshape(bsz, seq, 3 * rw)
    ps = _lora_in(p_sm, mu_sm, lora_w, lora_a, bsz, seq)
    n_chunks = seq // CHUNK

    def col(off):
        return pl.BlockSpec((1, seq, LANES), lambda b, h: (b, 0, off + h))

    rows = [mu_rkv[0, :rw], mu_rkv[0, rw:2 * rw], mu_rkv[0, 2 * rw:],
            w0, a0, k_k, k_a, r_k.reshape(rw), ln_w, ln_b]
    vecs = jnp.concatenate([jnp.stack(rows), jnp.zeros((VEC_ROWS - len(rows), rw), F32)], axis=0)
    lora = jnp.concatenate([w2p, a2p, g2p], axis=0)

    return pl.pallas_call(
        functools.partial(_rwkv_kernel, seq=seq, lora_w=lora_w, lora_a=lora_a),
        grid=(bsz, npair),
        in_specs=[col(0), col(npair), col(2 * npair),
                  pl.BlockSpec((1, seq, wsm), lambda b, h: (b, 0, 0)),
                  pl.BlockSpec((VEC_ROWS, LANES), lambda b, h: (0, h)),
                  pl.BlockSpec((wsm, LANES), lambda b, h: (0, h))],
        out_specs=pl.BlockSpec((1, seq, LANES), lambda b, h: (b, 0, h)),
        out_shape=jax.ShapeDtypeStruct((bsz, seq, rw), BF16),
        scratch_shapes=(
            [pltpu.VMEM((seq, LANES), BF16) for _ in range(7)]
            + [pltpu.VMEM((seq, LANES), F32) for _ in range(2)]
            + [pltpu.VMEM((n_chunks, 8, LANES), F32),
               pltpu.VMEM((n_chunks, LANES, LANES), BF16),
               pltpu.VMEM((n_chunks, LANES, LANES), F32),
               pltpu.VMEM((seq, LANES), BF16),
               pltpu.VMEM((seq, LANES), F32),
               pltpu.VMEM((n_chunks, LANES, LANES), BF16)]),
        compiler_params=_params("parallel", "parallel"),
        name="rwkv7",
    )(p3, p3, p3, ps, vecs, lora)


def _fgate_kernel(f_ref, bf_ref, col_ref, row_ref, *, seq):
    nblk = seq // LANES
    ri = lax.broadcasted_iota(jnp.int32, (LANES, LANES), 0)
    ci = lax.broadcasted_iota(jnp.int32, (LANES, LANES), 1)
    tri = (ci <= ri).astype(BF16)

    def body(i, carry):
        t0 = pl.multiple_of(i * LANES, LANES)
        z = f_ref[0, pl.ds(t0, LANES), :] + bf_ref[...]
        logf = jnp.minimum(z, 0.0) - jnp.log1p(jnp.exp(-jnp.abs(z)))
        hi = logf.astype(BF16)
        r1 = logf - hi.astype(F32)
        mid = r1.astype(BF16)
        lo = (r1 - mid.astype(F32)).astype(BF16)
        cum = carry + (_dot(tri, hi) + _dot(tri, mid) + _dot(tri, lo))
        col_ref[0, pl.ds(t0, LANES), :] = cum
        row_ref[0, :, pl.ds(t0, LANES)] = cum.T
        return cum[LANES - 1:LANES, :]

    lax.fori_loop(0, nblk, body, jnp.zeros((1, LANES), F32))


def _fgate(p_sm, b_f_pad, off, bsz, seq):
    ps = p_sm.reshape(bsz, seq, p_sm.shape[-1])
    return pl.pallas_call(
        functools.partial(_fgate_kernel, seq=seq),
        grid=(bsz,),
        in_specs=[pl.BlockSpec((1, seq, LANES), lambda b: (b, 0, off // LANES)),
                  pl.BlockSpec((1, LANES), lambda b: (0, 0))],
        out_specs=[pl.BlockSpec((1, seq, LANES), lambda b: (b, 0, 0)),
                   pl.BlockSpec((1, LANES, seq), lambda b: (b, 0, 0))],
        out_shape=[jax.ShapeDtypeStruct((bsz, seq, LANES), F32),
                   jax.ShapeDtypeStruct((bsz, LANES, seq), F32)],
        compiler_params=_params("parallel"),
        name="fox_gates",
    )(ps, b_f_pad)


FOX_HEADS_PER_STEP = 2
LOG2E = 1.4426950408889634


def _fox_kernel(q_ref, k_ref, v_ref, ccol_ref, crow_ref, on_ref, o_ref, *, tq, seq, scale):
    hg = pl.program_id(1)
    heads = range(FOX_HEADS_PER_STEP)
    hsl = [slice(g * FOX_HEAD, (g + 1) * FOX_HEAD) for g in heads]
    lane = lax.broadcasted_iota(jnp.int32, (tq, LANES), 1)
    causal = (lax.broadcasted_iota(jnp.int32, (tq, tq), 1)
              <= lax.broadcasted_iota(jnp.int32, (tq, tq), 0))
    ones_col = jnp.where(lane == 0, 1.0, 0.0).astype(BF16)

    for i in range(seq // tq):
        qrows = pl.ds(i * tq, tq)
        q = [q_ref[0, qrows, sl] for sl in hsl]
        ccol = ccol_ref[0, qrows, :]
        cq = [LOG2E * jnp.sum(jnp.where(lane == hg * FOX_HEADS_PER_STEP + g, ccol, 0.0),
                              axis=-1, keepdims=True) for g in heads]

        def scores(j, q=q, cq=cq, i=i):
            krows = pl.ds(j * tq, tq)
            s = [_dot_nt(q[g], k_ref[0, krows, hsl[g]]) * (scale * LOG2E)
                 + cq[g] - LOG2E * crow_ref[0, g, :, krows] for g in heads]
            if j == i:
                s = [jnp.where(causal, x, -jnp.inf) for x in s]
            return s

        m = [jnp.full((tq, 1), -jnp.inf, F32) for _ in heads]
        acc = [jnp.zeros((tq, 2 * FOX_HEAD), F32) for _ in heads]
        s = scores(0)
        for j in range(i + 1):
            s_next = scores(j + 1) if j < i else None
            vrows = pl.ds(j * tq, tq)
            m_new = [jnp.maximum(m[g], jnp.max(s[g], axis=-1, keepdims=True)) for g in heads]
            alpha = [jnp.exp2(m[g] - m_new[g]) for g in heads]
            p = [jnp.exp2(s[g] - m_new[g]).astype(BF16) for g in heads]
            acc = [alpha[g] * acc[g]
                   + _dot(p[g], jnp.concatenate([v_ref[0, vrows, hsl[g]], ones_col], axis=1))
                   for g in heads]
            m, s = m_new, s_next
        for g in heads:
            o = acc[g][:, :FOX_HEAD] / acc[g][:, FOX_HEAD:FOX_HEAD + 1]
            o = o * lax.rsqrt(jnp.mean(o * o, axis=-1, keepdims=True) + NORM_EPS) * on_ref[g]
            o_ref[0, qrows, hsl[g]] = o.astype(o_ref.dtype)


def _fox(p_fox, ccol, crow, out_norm, bsz, seq):
    fw = p_fox.shape[-1] // 3
    nh = fw // FOX_HEAD
    hps = FOX_HEADS_PER_STEP
    ng = nh // hps
    wid = hps * FOX_HEAD
    tq = _pick(seq, (512, 256, 128))
    p3 = p_fox.reshape(bsz, seq, 3 * fw)
    crow4 = crow.reshape(bsz, LANES, 1, seq)
    on = out_norm.reshape(nh, 1, FOX_HEAD)
    return pl.pallas_call(
        functools.partial(_fox_kernel, tq=tq, seq=seq, scale=FOX_HEAD ** -0.5),
        grid=(bsz, ng),
        in_specs=[pl.BlockSpec((1, seq, wid), lambda b, h: (b, 0, h)),
                  pl.BlockSpec((1, seq, wid), lambda b, h: (b, 0, ng + h)),
                  pl.BlockSpec((1, seq, wid), lambda b, h: (b, 0, 2 * ng + h)),
                  pl.BlockSpec((1, seq, LANES), lambda b, h: (b, 0, 0)),
                  pl.BlockSpec((1, hps, 1, seq), lambda b, h: (b, h, 0, 0)),
                  pl.BlockSpec((hps, 1, FOX_HEAD), lambda b, h: (h, 0, 0))],
        out_specs=pl.BlockSpec((1, seq, wid), lambda b, h: (b, 0, h)),
        out_shape=jax.ShapeDtypeStruct((bsz, seq, fw), BF16),
        compiler_params=_params("parallel", "parallel"),
        name="fox_attention",
    )(p3, p3, p3, ccol, crow4, on)


def _pad_rows(w, rows):
    return jnp.pad(w, ((0, rows - w.shape[0]), (0, 0)))


def _cast_pad_kernel(w_ref, o_ref, *, n_valid):
    @pl.when(pl.program_id(0) < n_valid)
    def _():
        o_ref[...] = w_ref[...].astype(o_ref.dtype)

    @pl.when(pl.program_id(0) >= n_valid)
    def _():
        o_ref[...] = jnp.zeros_like(o_ref)


def _cast_pad_rows(w, rows_out):
    rows, cols = w.shape
    tr = _pick(rows, (256, 128, 64, 16))
    assert rows_out % tr == 0
    n_valid = rows // tr
    return pl.pallas_call(
        functools.partial(_cast_pad_kernel, n_valid=n_valid),
        grid=(rows_out // tr,),
        in_specs=[pl.BlockSpec((tr, cols), lambda i: (jnp.minimum(i, n_valid - 1), 0))],
        out_specs=pl.BlockSpec((tr, cols), lambda i: (i, 0)),
        out_shape=jax.ShapeDtypeStruct((rows_out, cols), BF16),
        compiler_params=_params("parallel"),
        name="cast_pad_rows",
    )(w)


def _ffn(x2, h, w_gate, w_up, w_down, gate, seq):
    dff = w_gate.shape[1]
    dffp = _round_up(dff, 512)
    wd = _cast_pad_rows(w_down, dffp)
    act = _swiglu_up(h, w_gate, w_up, dffp)
    return _matmul_residual(act, wd, x2, gate, seq, 0.5, dff)


def kernel(x, c, w_mod, b_mod, norm_ffn1, ffn1_gate, ffn1_up, ffn1_down, norm_mix, w_in, rwkv_mu, rwkv_w0, rwkv_w2, rwkv_a0, rwkv_a2, rwkv_g2, rwkv_k_k, rwkv_k_a, rwkv_r_k, rwkv_ln_w, rwkv_ln_b, fox_b_f, fox_out_norm, w_out, norm_ffn2, ffn2_gate, ffn2_up, ffn2_down, norm_final):
    bsz, seq, d = x.shape
    depth = w_mod.shape[0]
    rw = rwkv_w0.shape[-1]
    fw = fox_out_norm.shape[-1]
    nfh = fox_b_f.shape[-1]
    dl, al, gl = rwkv_w2.shape[1], rwkv_a2.shape[1], rwkv_g2.shape[1]
    dlp, alp, glp = (_round_up(n, LANES) for n in (dl, al, gl))
    c3 = 3 * rw
    c4, c5, c6 = c3 + dl, c3 + dl + al, c3 + dl + al + gl

    x2 = x.reshape(bsz * seq, d)
    for l in range(depth):
        mod = _mod(c, w_mod[l], b_mod[l])
        sh1, sc1, gt1, sh2, sc2, gt2, sh3, sc3, gt3 = (
            m[:, None, :] for m in jnp.split(mod, 9, axis=-1))

        h = _norm_mod(x2, norm_ffn1[l], sc1, sh1, seq)
        x2 = _ffn(x2, h, ffn1_gate[l], ffn1_up[l], ffn1_down[l], gt1, seq)

        h = _norm_mod(x2, norm_mix[l], sc2, sh2, seq)
        wi, mu = w_in[l], rwkv_mu[l]
        wt = jnp.swapaxes(wi, 0, 1).astype(BF16)
        wt_sm = jnp.concatenate(
            [_pad_rows(wt[c3:c4], dlp), _pad_rows(wt[c4:c5], alp),
             _pad_rows(wt[c5:c6], glp), _pad_rows(wt[c6 + 3 * fw:], LANES)], axis=0)
        mu_rkv = mu[:c3].reshape(1, c3)
        mu_sm = jnp.concatenate(
            [jnp.pad(mu[c3:c4], (0, dlp - dl)), jnp.pad(mu[c4:c5], (0, alp - al)),
             jnp.pad(mu[c5:c6], (0, glp - gl))]).reshape(1, dlp + alp + glp)
        p_rkv = _matmul_nt(h, wt, F32, 0, c3)
        p_sm = _matmul_nt(h, wt_sm, F32)
        p_fox = _matmul_nt(h, wt, BF16, c6, 3 * fw)

        y_r = _rwkv(p_rkv, p_sm, mu_rkv, mu_sm,
                    rwkv_w0[l], rwkv_a0[l], rwkv_k_k[l], rwkv_k_a[l], rwkv_r_k[l],
                    rwkv_ln_w[l], rwkv_ln_b[l],
                    _pad_rows(rwkv_w2[l], dlp).astype(BF16),
                    _pad_rows(rwkv_a2[l], alp).astype(BF16),
                    _pad_rows(rwkv_g2[l], glp).astype(BF16), bsz, seq)

        b_f_pad = jnp.pad(fox_b_f[l], (0, LANES - nfh)).reshape(1, LANES)
        ccol, crow = _fgate(p_sm, b_f_pad, dlp + alp + glp, bsz, seq)
        y_f = _fox(p_fox, ccol, crow, fox_out_norm[l], bsz, seq)

        x2 = _matmul_residual2(y_r.reshape(bsz * seq, rw), y_f.reshape(bsz * seq, fw),
                               w_out[l].astype(BF16), x2, gt2, seq, 1.0)

        h = _norm_mod(x2, norm_ffn2[l], sc3, sh3, seq)
        x2 = _ffn(x2, h, ffn2_gate[l], ffn2_up[l], ffn2_down[l], gt3, seq)
    return _final_norm(x2, norm_final).reshape(bsz, seq, d)
```

```python
import functools

import jax
import jax.numpy as jnp
from jax import lax
from jax.experimental import pallas as pl
from jax.experimental.pallas import tpu as pltpu

F32 = jnp.float32
BF16 = jnp.bfloat16

LANES = 128
RWKV_HEAD = 64
FOX_HEAD = 128
CHUNK = 64
INV_BASE = 4
GROUP = 16
NORM_EPS = 1e-6
LN_X_EPS = 64e-5
VMEM_LIMIT = 56 * 1024 * 1024


def _round_up(n, m):
    return (n + m - 1) // m * m


def _pick(n, candidates):
    for c in candidates:
        if n % c == 0:
            return c
    return n


def _dot(a, b):
    return jnp.dot(a, b, preferred_element_type=F32)


def _dot_nt(a, b):
    return lax.dot_general(a, b, (((1,), (1,)), ((), ())), preferred_element_type=F32)


def _dot_tn(a, b):
    return lax.dot_general(a, b, (((0,), (0,)), ((), ())), preferred_element_type=F32)


def _params(*sem):
    return pltpu.CompilerParams(dimension_semantics=sem, vmem_limit_bytes=VMEM_LIMIT)


def _mod_kernel(c_ref, w_ref, b_ref, o_ref):
    c = c_ref[...]
    a = (c * jax.nn.sigmoid(c)).astype(BF16)
    o_ref[...] = _dot(a, w_ref[...].astype(BF16)) + b_ref[...]


def _mod(c, w_mod, b_mod):
    bsz, d = c.shape
    n = w_mod.shape[1]
    rows = _round_up(bsz, 16)
    c_pad = jnp.pad(c, ((0, rows - bsz), (0, 0)))
    tn = _pick(n, (512, 256, 128))
    out = pl.pallas_call(
        _mod_kernel,
        grid=(n // tn,),
        in_specs=[pl.BlockSpec((rows, d), lambda j: (0, 0)),
                  pl.BlockSpec((d, tn), lambda j: (0, j)),
                  pl.BlockSpec((1, tn), lambda j: (0, j))],
        out_specs=pl.BlockSpec((rows, tn), lambda j: (0, j)),
        out_shape=jax.ShapeDtypeStruct((rows, n), F32),
        compiler_params=_params("parallel"),
        name="mod_matmul",
    )(c_pad, w_mod, b_mod.reshape(1, n))
    return out[:bsz]


def _norm_kernel(x_ref, g_ref, sc_ref, sh_ref, o_ref):
    x = x_ref[...]
    y = x * lax.rsqrt(jnp.mean(x * x, axis=-1, keepdims=True) + NORM_EPS) * g_ref[...]
    o_ref[...] = (y * (1.0 + sc_ref[0]) + sh_ref[0]).astype(o_ref.dtype)


def _final_norm_kernel(x_ref, g_ref, o_ref):
    x = x_ref[...]
    o_ref[...] = x * lax.rsqrt(jnp.mean(x * x, axis=-1, keepdims=True) + NORM_EPS) * g_ref[...]


def _norm_mod(x2, g, sc, sh, seq):
    t, d = x2.shape
    ts = _pick(seq, (512, 256, 128, 64, 8))
    return pl.pallas_call(
        _norm_kernel,
        grid=(t // ts,),
        in_specs=[pl.BlockSpec((ts, d), lambda i: (i, 0)),
                  pl.BlockSpec((1, d), lambda i: (0, 0)),
                  pl.BlockSpec((1, 1, d), lambda i: (i * ts // seq, 0, 0)),
                  pl.BlockSpec((1, 1, d), lambda i: (i * ts // seq, 0, 0))],
        out_specs=pl.BlockSpec((ts, d), lambda i: (i, 0)),
        out_shape=jax.ShapeDtypeStruct((t, d), BF16),
        compiler_params=_params("parallel"),
        name="norm_mod",
    )(x2, g.reshape(1, d), sc, sh)


def _final_norm(x2, g):
    t, d = x2.shape
    ts = _pick(t, (512, 256, 128, 64, 8))
    return pl.pallas_call(
        _final_norm_kernel,
        grid=(t // ts,),
        in_specs=[pl.BlockSpec((ts, d), lambda i: (i, 0)),
                  pl.BlockSpec((1, d), lambda i: (0, 0))],
        out_specs=pl.BlockSpec((ts, d), lambda i: (i, 0)),
        out_shape=jax.ShapeDtypeStruct((t, d), F32),
        compiler_params=_params("parallel"),
        name="final_norm",
    )(x2, g.reshape(1, d))


def _mm_nt_kernel(x_ref, wt_ref, o_ref):
    o_ref[...] = _dot_nt(x_ref[...], wt_ref[...]).astype(o_ref.dtype)


def _matmul_nt(x, wt, out_dtype, row0=0, n=None):
    m, kd = x.shape
    n = wt.shape[0] if n is None else n
    tm = _pick(m, (1024, 512, 256, 128))
    tn = _pick(n, (1024, 512, 640, 256, 128))
    return pl.pallas_call(
        _mm_nt_kernel,
        grid=(m // tm, n // tn),
        in_specs=[pl.BlockSpec((tm, kd), lambda i, j: (i, 0)),
                  pl.BlockSpec((pl.Element(tn), pl.Element(kd)),
                               lambda i, j: ((row0 // 16 + j * (tn // 16)) * 16, 0))],
        out_specs=pl.BlockSpec((tm, tn), lambda i, j: (i, j)),
        out_shape=jax.ShapeDtypeStruct((m, n), out_dtype),
        compiler_params=_params("parallel", "parallel"),
        name="matmul_nt",
    )(x, wt)


def _swiglu_kernel(x_ref, wg_hbm, wu_hbm, wd_ref, o_ref, wdb_ref, stage_ref, wgb_ref, wub_ref,
                   sem, *, n_full, rem, tn, wd_blocks):
    j = pl.program_id(0)
    step = j * pl.num_programs(1) + pl.program_id(1)

    def full_copies(jj):
        cols = pl.ds(pl.multiple_of(jj * tn, tn), tn)
        return (pltpu.make_async_copy(wg_hbm.at[:, cols], stage_ref.at[0], sem.at[0]),
                pltpu.make_async_copy(wu_hbm.at[:, cols], stage_ref.at[1], sem.at[1]))

    def tail_copies():
        cols = pl.ds(n_full * tn, rem)
        return (pltpu.make_async_copy(wg_hbm.at[:, cols], stage_ref.at[0, :, :rem], sem.at[0]),
                pltpu.make_async_copy(wu_hbm.at[:, cols], stage_ref.at[1, :, :rem], sem.at[1]))

    def start(jj):
        @pl.when(jj < n_full)
        def _():
            for c in full_copies(jj):
                c.start()

        if rem:
            @pl.when(jj == n_full)
            def _():
                for c in tail_copies():
                    c.start()

    @pl.when(pl.program_id(1) == 0)
    def _():
        @pl.when(j == 0)
        def _():
            start(j)

        @pl.when(j < n_full)
        def _():
            for c in full_copies(j):
                c.wait()
            wgb_ref[...] = stage_ref[0].astype(BF16)
            wub_ref[...] = stage_ref[1].astype(BF16)

        if rem:
            @pl.when(j == n_full)
            def _():
                for c in tail_copies():
                    c.wait()
                wgb_ref[:, :rem] = stage_ref[0, :, :rem].astype(BF16)
                wub_ref[:, :rem] = stage_ref[1, :, :rem].astype(BF16)

        start(j + 1)

    def compute(ncols):
        wdb_ref[...] = jnp.where(step < wd_blocks, wd_ref[...], 0.0).astype(wdb_ref.dtype)
        x = x_ref[...]
        g = _dot(x, wgb_ref[:, :ncols])
        u = _dot(x, wub_ref[:, :ncols])
        o_ref[:, :ncols] = (g * (0.5 * jnp.tanh(0.5 * g) + 0.5) * u).astype(o_ref.dtype)
        if ncols < tn:
            o_ref[:, ncols:] = jnp.zeros((o_ref.shape[0], tn - ncols), o_ref.dtype)

    if rem:
        pl.when(j < n_full)(lambda: compute(tn))
        pl.when(j == n_full)(lambda: compute(rem))
    else:
        compute(tn)


def _swiglu_up(h, wg, wu, w_down, n_out):
    m, kd = h.shape
    n = wg.shape[1]
    d_out = w_down.shape[1]
    tm = _pick(m, (1024, 512, 256, 128))
    tn = 512
    n_full, rem = divmod(n, tn)
    assert rem % LANES == 0 and n_out == _round_up(n, tn)
    nj, ni = n_out // tn, m // tm
    slab, left = divmod(n_out, nj * ni)
    assert left == 0 and slab % 16 == 0 and n % slab == 0
    wd_blocks = n // slab
    return pl.pallas_call(
        functools.partial(_swiglu_kernel, n_full=n_full, rem=rem, tn=tn, wd_blocks=wd_blocks),
        grid=(nj, ni),
        in_specs=[pl.BlockSpec((tm, kd), lambda j, i: (i, 0)),
                  pl.BlockSpec(memory_space=pl.ANY),
                  pl.BlockSpec(memory_space=pl.ANY),
                  pl.BlockSpec((slab, d_out),
                               lambda j, i: (jnp.minimum(j * ni + i, wd_blocks - 1), 0))],
        out_specs=[pl.BlockSpec((tm, tn), lambda j, i: (i, j)),
                   pl.BlockSpec((slab, d_out), lambda j, i: (j * ni + i, 0))],
        out_shape=[jax.ShapeDtypeStruct((m, n_out), BF16),
                   jax.ShapeDtypeStruct((n_out, d_out), BF16)],
        scratch_shapes=[pltpu.VMEM((2, kd, tn), F32),
                        pltpu.VMEM((kd, tn), BF16), pltpu.VMEM((kd, tn), BF16),
                        pltpu.SemaphoreType.DMA((2,))],
        compiler_params=_params("arbitrary", "arbitrary"),
        name="swiglu_up",
    )(h, wg, wu, w_down)


def _resid_kernel(a_ref, w_ref, x_ref, gt_ref, *rest, nk, k_last, scale):
    side_in, o_ref, side_out, acc_ref = rest if len(rest) == 4 else (None, rest[0], None, rest[1])
    k = pl.program_id(2)

    def side_cast():
        if side_in is not None:
            side_out[...] = side_in[...].astype(side_out.dtype)

    @pl.when(k == 0)
    def _():
        side_cast()
        acc_ref[...] = _dot(a_ref[...], w_ref[...])

    @pl.when((k > 0) & (k < nk - 1))
    def _():
        side_cast()
        acc_ref[...] += _dot(a_ref[...], w_ref[...])

    @pl.when(k == nk - 1)
    def _():
        side_cast()
        y = acc_ref[...] + _dot(a_ref[:, :k_last], w_ref[:k_last, :])
        o_ref[...] = x_ref[...] + (scale * gt_ref[0]) * y


def _matmul_residual(a, w, x2, gate, seq, scale, k_valid, side=None):
    m, kd = a.shape
    n = w.shape[1]
    tm = _pick(seq, (1024, 512, 256, 128))
    tn = _pick(n, (1024, 512, 256, 128))
    tk = _pick(kd, (2816, 1024, 512, 256, 128))
    nk = kd // tk
    k_last = k_valid - (nk - 1) * tk
    assert nk >= 2 and 0 < k_last <= tk and k_last % LANES == 0
    ni, nj = m // tm, n // tn
    in_specs = [pl.BlockSpec((tm, tk), lambda i, j, k: (i, k)),
                pl.BlockSpec((tk, tn), lambda i, j, k: (k, j)),
                pl.BlockSpec((tm, tn), lambda i, j, k: (i, j)),
                pl.BlockSpec((1, 1, tn), lambda i, j, k: (i * tm // seq, 0, j))]
    out_specs = [pl.BlockSpec((tm, tn), lambda i, j, k: (i, j))]
    out_shape = [jax.ShapeDtypeStruct((m, n), F32)]
    operands = [a, w, x2, gate]
    semantics = ("parallel", "parallel", "arbitrary")
    if side is not None:
        rows, cols = side.shape
        slab = _round_up(-(-rows // (ni * nj * nk)), 16)
        last = -(-rows // slab) - 1

        def side_map(i, j, k):
            return (jnp.minimum((i * nj + j) * nk + k, last), 0)

        in_specs.append(pl.BlockSpec((slab, cols), side_map))
        out_specs.append(pl.BlockSpec((slab, cols), side_map))
        out_shape.append(jax.ShapeDtypeStruct((rows, cols), BF16))
        operands.append(side)
        semantics = ("arbitrary", "arbitrary", "arbitrary")
    res = pl.pallas_call(
        functools.partial(_resid_kernel, nk=nk, k_last=k_last, scale=scale),
        grid=(ni, nj, nk),
        in_specs=in_specs,
        out_specs=out_specs,
        out_shape=out_shape,
        scratch_shapes=[pltpu.VMEM((tm, tn), F32)],
        compiler_params=_params(*semantics),
        name="matmul_residual",
    )(*operands)
    return res[0] if side is None else res


def _resid2_kernel(a1_ref, a2_ref, w1_ref, w2_ref, x_ref, gt_ref, o_ref, *, scale):
    y = _dot(a1_ref[...], w1_ref[...]) + _dot(a2_ref[...], w2_ref[...])
    o_ref[...] = x_ref[...] + (scale * gt_ref[0]) * y


def _matmul_residual2(a1, a2, w, x2, gate, seq, scale):
    m, k1 = a1.shape
    k2 = a2.shape[1]
    n = w.shape[1]
    assert k1 == k2
    tm = _pick(seq, (1024, 512, 256, 128))
    tn = _pick(n, (512, 256, 128))
    return pl.pallas_call(
        functools.partial(_resid2_kernel, scale=scale),
        grid=(m // tm, n // tn),
        in_specs=[pl.BlockSpec((tm, k1), lambda i, j: (i, 0)),
                  pl.BlockSpec((tm, k2), lambda i, j: (i, 0)),
                  pl.BlockSpec((k1, tn), lambda i, j: (0, j)),
                  pl.BlockSpec((k2, tn), lambda i, j: (1, j)),
                  pl.BlockSpec((tm, tn), lambda i, j: (i, j)),
                  pl.BlockSpec((1, 1, tn), lambda i, j: (i * tm // seq, 0, j))],
        out_specs=pl.BlockSpec((tm, tn), lambda i, j: (i, j)),
        out_shape=jax.ShapeDtypeStruct((m, n), F32),
        compiler_params=_params("parallel", "parallel"),
        name="out_proj_residual",
    )(a1, a2, w, w, x2, gate)


ROW_TILE = 512


def _lane_head0(shape):
    return lax.broadcasted_iota(jnp.int32, shape, 1) < RWKV_HEAD


def _stack(x):
    m0 = _lane_head0(x.shape)
    zero = jnp.zeros_like(x)
    return jnp.concatenate([jnp.where(m0, x, zero), jnp.where(m0, zero, x)], axis=0)


def _split_dot(lhs_bf16, x):
    hi = x.astype(BF16)
    lo = (x - hi.astype(F32)).astype(BF16)
    return _dot(lhs_bf16, hi) + _dot(lhs_bf16, lo)


def _lora_in_kernel(ps_ref, prev_ref, mu_ref, o_ref, *, lora_w, lora_a):
    x = ps_ref[0]
    prev_row = jnp.where(pl.program_id(1) == 0, 0.0, prev_ref[0][7:8, :])
    row = lax.broadcasted_iota(jnp.int32, x.shape, 0)
    prev = jnp.where(row == 0, prev_row, pltpu.roll(x, 1, 0))
    sm = x + (prev - x) * mu_ref[...]
    o_ref[0, :, :lora_w] = jnp.tanh(sm[:, :lora_w]).astype(o_ref.dtype)
    o_ref[0, :, lora_w:lora_w + lora_a] = sm[:, lora_w:lora_w + lora_a].astype(o_ref.dtype)
    o_ref[0, :, lora_w + lora_a:] = jax.nn.sigmoid(sm[:, lora_w + lora_a:]).astype(o_ref.dtype)


def _lora_in(p_sm, mu_sm, lora_w, lora_a, bsz, seq):
    wsm = mu_sm.shape[-1]
    ps = p_sm.reshape(bsz, seq, p_sm.shape[-1])
    ts = _pick(seq, (ROW_TILE, 128, 64, 8))
    return pl.pallas_call(
        functools.partial(_lora_in_kernel, lora_w=lora_w, lora_a=lora_a),
        grid=(bsz, seq // ts),
        in_specs=[pl.BlockSpec((1, ts, wsm), lambda b, i: (b, i, 0)),
                  pl.BlockSpec((1, 8, wsm), lambda b, i: (b, jnp.maximum(i * (ts // 8) - 1, 0), 0)),
                  pl.BlockSpec((1, wsm), lambda b, i: (0, 0))],
        out_specs=pl.BlockSpec((1, ts, wsm), lambda b, i: (b, i, 0)),
        out_shape=jax.ShapeDtypeStruct((bsz, seq, wsm), BF16),
        compiler_params=_params("parallel", "parallel"),
        name="rwkv_lora_in",
    )(ps, ps, mu_sm)


(VEC_MU_R, VEC_MU_K, VEC_MU_V, VEC_W0, VEC_A0, VEC_KK, VEC_KA, VEC_RK, VEC_LNW,
 VEC_LNB) = range(10)
VEC_ROWS = 16


def _rwkv_kernel(pr_ref, pk_ref, pv_ref, ps_ref, vec_ref, lora_ref, o_ref,
                 at_s, rt_s, kt_s, bt_s, kg_s, bg_s, v_s, g_s, bonus_s, gc_s,
                 m_s, n_s, q_s, yb_s, sb_s, *, seq, lora_w, lora_a):
    n_chunks = seq // CHUNK
    cpt = ROW_TILE // CHUNK

    def vec(idx):
        return vec_ref[idx:idx + 1, :]

    ri = lax.broadcasted_iota(jnp.int32, (LANES, LANES), 0)
    ci = lax.broadcasted_iota(jnp.int32, (LANES, LANES), 1)
    bd_mask = (ri // RWKV_HEAD) == (ci // RWKV_HEAD)
    bd_ones = bd_mask.astype(BF16)
    tr = lax.broadcasted_iota(jnp.int32, (ROW_TILE, ROW_TILE), 0)
    tc = lax.broadcasted_iota(jnp.int32, (ROW_TILE, ROW_TILE), 1)
    tri = ((tc <= tr) & ((tr // CHUNK) == (tc // CHUNK))).astype(BF16)

    def head_sum(x):
        return _dot(x.astype(BF16), bd_ones)

    def shift_mix(ref, t0, mu):
        x = ref[0, pl.ds(t0, ROW_TILE), :]
        if t0 == 0:
            prev_row = jnp.zeros((1, x.shape[1]), F32)
        else:
            prev_row = ref[0, pl.ds(t0 - 8, 8), :][7:8, :]
        row = lax.broadcasted_iota(jnp.int32, x.shape, 0)
        prev = jnp.where(row == 0, prev_row, pltpu.roll(x, 1, 0))
        return x + (prev - x) * mu

    def phase_a_stages(i):
        t0 = i * ROW_TILE
        rows = pl.ds(t0, ROW_TILE)
        st = {}

        def load_and_lora():
            st["r"] = shift_mix(pr_ref, t0, vec(VEC_MU_R))
            st["k"] = shift_mix(pk_ref, t0, vec(VEC_MU_K))
            st["v"] = shift_mix(pv_ref, t0, vec(VEC_MU_V))
            st["wlin"] = vec(VEC_W0) + _dot(ps_ref[0, rows, :lora_w], lora_ref[:lora_w, :])
            st["alin"] = vec(VEC_A0) + _dot(ps_ref[0, rows, lora_w:lora_w + lora_a],
                                            lora_ref[lora_w:lora_w + lora_a, :])
            st["g"] = _dot(ps_ref[0, rows, lora_w + lora_a:], lora_ref[lora_w + lora_a:, :])

        def decay_and_norm():
            wlin = st.pop("wlin")
            wlog = -(jnp.maximum(-wlin, 0.0) + jnp.log(1.0 + jnp.exp(-jnp.abs(wlin)))) - 0.5
            st["ld"] = -jnp.exp(wlog)
            st["alr"] = jax.nn.sigmoid(st.pop("alin"))
            st["kk"] = st["k"] * vec(VEC_KK)
            st["kk_ss"] = head_sum(st["kk"] * st["kk"])
            st["cum"] = _split_dot(tri, st["ld"])

        def keys_and_bonus():
            st["kk"] = st["kk"] * lax.rsqrt(jnp.maximum(st.pop("kk_ss"), 1e-24))
            st["k"] = st["k"] * (1.0 + (st["alr"] - 1.0) * vec(VEC_KA))
            st["bonus_sum"] = head_sum(st["r"] * st["k"] * vec(VEC_RK))
            st["b"] = st["kk"] * st["alr"]

        def scale_and_store():
            cum, ld = st["cum"], st.pop("ld")
            g_dec = jnp.exp(cum)
            g_inv = jnp.exp(-cum)
            at_s[rows, :] = (-st.pop("kk") * jnp.exp(cum - ld)).astype(BF16)
            rt_s[rows, :] = (st.pop("r") * g_dec).astype(BF16)
            kt_s[rows, :] = (st["k"] * g_inv).astype(BF16)
            bt_s[rows, :] = (st["b"] * g_inv).astype(BF16)
            v_s[rows, :] = st["v"].astype(BF16)
            g_s[rows, :] = st.pop("g")
            bonus_s[rows, :] = st.pop("bonus_sum") * st.pop("v")

        def chunk_tails():
            cum, k, b = st.pop("cum"), st.pop("k"), st.pop("b")
            for c in range(cpt):
                lo, hi = c * CHUNK, (c + 1) * CHUNK
                cum_c = cum[hi - 1:hi, :]
                rem = jnp.exp(cum_c - cum[lo:hi, :])
                kg_s[pl.ds(t0 + lo, CHUNK), :] = (k[lo:hi, :] * rem).astype(BF16)
                bg_s[pl.ds(t0 + lo, CHUNK), :] = (b[lo:hi, :] * rem).astype(BF16)
                gc_s[i * cpt + c] = jnp.broadcast_to(jnp.exp(cum_c), (8, LANES))

        return [load_and_lora, decay_and_norm, keys_and_bonus, scale_and_store, chunk_tails]

    row = lax.broadcasted_iota(jnp.int32, (CHUNK, LANES), 0)
    col = lax.broadcasted_iota(jnp.int32, (CHUNK, LANES), 1) % RWKV_HEAD
    strict = col < row
    incl = col <= row
    eye_p = (col == row).astype(F32)

    def same_block(size):
        return (row // size) == (col // size)

    def each(fn, *lists):
        return [fn(*xs) for xs in zip(*lists)]

    def phase_b(i, tick):
        cs = [i * GROUP + g for g in range(GROUP)]
        sls = [pl.ds(c * CHUNK, CHUNK) for c in cs]
        at = [at_s[sl, :] for sl in sls]
        rt = [rt_s[sl, :] for sl in sls]
        kt = [kt_s[sl, :] for sl in sls]
        bt = [bt_s[sl, :] for sl in sls]
        v = [v_s[sl, :] for sl in sls]
        sc = each(lambda a, r, k, b: _dot_nt(jnp.concatenate([a, r], axis=0),
                                             jnp.concatenate([_stack(k), _stack(b)], axis=0)),
                  at, rt, kt, bt)
        a_ak = [jnp.where(strict, x[:CHUNK, :LANES], 0.0).astype(BF16) for x in sc]
        a_ab = [jnp.where(strict, x[:CHUNK, LANES:], 0.0) for x in sc]
        a_rk = [jnp.where(incl, x[CHUNK:, :LANES], 0.0).astype(BF16) for x in sc]
        a_rb = [jnp.where(incl, x[CHUNK:, LANES:], 0.0).astype(BF16) for x in sc]
        tick()
        d = [jnp.where(same_block(INV_BASE), x, 0.0) for x in a_ab]
        db = [x.astype(BF16) for x in d]
        d2 = [_dot(x, _stack(x)).astype(BF16) for x in db]
        tick()
        p = [eye_p + x for x in d]
        p = each(lambda pp, x2: pp + _dot(pp.astype(BF16), _stack(x2)), p, d2)
        tick()
        s = INV_BASE
        while s < CHUNK:
            lower_left = same_block(2 * s) & ((row // s) % 2 == 1) & ((col // s) % 2 == 0)
            a21 = [jnp.where(lower_left, x, 0.0).astype(BF16) for x in a_ab]
            pb = [x.astype(BF16) for x in p]
            a21_t11 = each(lambda a, t: _dot(a, _stack(t)).astype(BF16), a21, pb)
            tick()
            p = each(lambda pp, t, y: pp + _dot(t, _stack(y)), p, pb, a21_t11)
            tick()
            s *= 2
        tinv = [x.astype(BF16) for x in p]
        vst = [_stack(x) for x in v]
        x1 = each(lambda a, vs: _dot(a, vs).astype(BF16), a_ak, vst)
        tick()
        wu = each(lambda t, a, x: _dot(t, jnp.concatenate([_stack(a), _stack(x)], axis=1)),
                  tinv, at, x1)
        tick()
        wb = [x[:, :LANES].astype(BF16) for x in wu]
        ub = [x[:, LANES:].astype(BF16) for x in wu]
        qy = each(lambda a, w, u: _dot(a, jnp.concatenate([_stack(w), _stack(u)], axis=1)),
                  a_rb, wb, ub)
        tick()
        yb = each(lambda a, vs, x: _dot(a, vs) + x[:, LANES:], a_rk, vst, qy)
        tick()
        for g in range(GROUP):
            q_s[sls[g], :] = (rt[g].astype(F32) + qy[g][:, :LANES]).astype(BF16)
            yb_s[sls[g], :] = yb[g]
        kg = [kg_s[sl, :] for sl in sls]
        bg = [bg_s[sl, :] for sl in sls]
        mm = each(_dot_tn, wb, bg)
        tick()
        nn = each(lambda vv, u, k, b: _dot_tn(jnp.concatenate([vv, u], axis=0),
                                              jnp.concatenate([k, b], axis=0)), v, ub, kg, bg)
        tick()
        for g in range(GROUP):
            m_s[cs[g]] = jnp.where(bd_mask, mm[g], 0.0).astype(BF16)
            n_s[cs[g]] = jnp.where(bd_mask, nn[g], 0.0)

    def c1_step(c, s):
        sb = s.astype(BF16)
        sb_s[c] = sb
        return s * gc_s[c][0:1, :] + _dot(sb, m_s[c]) + n_s[c]

    n_groups = n_chunks // GROUP
    tiles_per_group = GROUP * CHUNK // ROW_TILE
    state = [jnp.zeros((LANES, LANES), F32)]

    def c1_thunk(c):
        def run():
            state[0] = c1_step(c, state[0])
        return run

    def prep_stages(group):
        per_tile = [phase_a_stages(group * tiles_per_group + t) for t in range(tiles_per_group)]
        return [stage for stages in zip(*per_tile) for stage in stages]

    def interleave(xs, ys):
        out = []
        for j in range(max(len(xs), len(ys))):
            out += xs[j:j + 1] + ys[j:j + 1]
        return out

    def run_with_fillers(body, fillers):
        def tick():
            if fillers:
                fillers.pop(0)()
        body(tick)
        while fillers:
            fillers.pop(0)()

    for stage in prep_stages(0):
        stage()
    for i in range(n_groups):
        fillers = interleave(
            [c1_thunk((i - 1) * GROUP + g) for g in range(GROUP)] if i > 0 else [],
            prep_stages(i + 1) if i + 1 < n_groups else [])
        run_with_fillers(functools.partial(phase_b, i), fillers)

    inv_n = 1.0 / RWKV_HEAD

    def phase_c2(i, tick):
        cs = [i * GROUP + g for g in range(GROUP)]
        sls = [pl.ds(c * CHUNK, CHUNK) for c in cs]

        def staged(fn, *lists):
            out = []
            for g, xs in enumerate(zip(*lists)):
                out.append(fn(*xs))
                if g % 4 == 3:
                    tick()
            return out

        y = staged(lambda sl, c: _dot_nt(q_s[sl, :], sb_s[c]) + yb_s[sl, :], sls, cs)
        mean = staged(lambda x: head_sum(x) * inv_n, y)
        yc = each(lambda a, b: a - b, y, mean)
        var = staged(lambda x: head_sum(x * x) * inv_n, yc)
        for g in range(GROUP):
            yn = yc[g] * lax.rsqrt(var[g] + LN_X_EPS) * vec(VEC_LNW) + vec(VEC_LNB)
            o_ref[0, sls[g], :] = ((yn + bonus_s[sls[g], :]) * g_s[sls[g], :]).astype(o_ref.dtype)

    tail = [c1_thunk((n_groups - 1) * GROUP + g) for g in range(GROUP)]
    for i in range(n_groups - 1):
        run_with_fillers(functools.partial(phase_c2, i), tail if i == n_groups - 2 else [])
    while tail:
        tail.pop(0)()
    phase_c2(n_groups - 1, lambda: None)


def _rwkv(p_rkv, p_sm, mu_rkv, mu_sm, w0, a0, k_k, k_a, r_k, ln_w, ln_b,
          w2p, a2p, g2p, bsz, seq):
    rw = w0.shape[-1]
    npair = rw // LANES
    lora_w, lora_a, lora_g = w2p.shape[0], a2p.shape[0], g2p.shape[0]
    wsm = lora_w + lora_a + lora_g
    p3 = p_rkv.reshape(bsz, seq, 3 * rw)
    ps = _lora_in(p_sm, mu_sm, lora_w, lora_a, bsz, seq)
    n_chunks = seq // CHUNK

    def col(off):
        return pl.BlockSpec((1, seq, LANES), lambda b, h: (b, 0, off + h))

    rows = [mu_rkv[0, :rw], mu_rkv[0, rw:2 * rw], mu_rkv[0, 2 * rw:],
            w0, a0, k_k, k_a, r_k.reshape(rw), ln_w, ln_b]
    vecs = jnp.concatenate([jnp.stack(rows), jnp.zeros((VEC_ROWS - len(rows), rw), F32)], axis=0)
    lora = jnp.concatenate([w2p, a2p, g2p], axis=0)

    return pl.pallas_call(
        functools.partial(_rwkv_kernel, seq=seq, lora_w=lora_w, lora_a=lora_a),
        grid=(bsz, npair),
        in_specs=[col(0), col(npair), col(2 * npair),
                  pl.BlockSpec((1, seq, wsm), lambda b, h: (b, 0, 0)),
                  pl.BlockSpec((VEC_ROWS, LANES), lambda b, h: (0, h)),
                  pl.BlockSpec((wsm, LANES), lambda b, h: (0, h))],
        out_specs=pl.BlockSpec((1, seq, LANES), lambda b, h: (b, 0, h)),
        out_shape=jax.ShapeDtypeStruct((bsz, seq, rw), BF16),
        scratch_shapes=(
            [pltpu.VMEM((seq, LANES), BF16) for _ in range(7)]
            + [pltpu.VMEM((seq, LANES), F32) for _ in range(2)]
            + [pltpu.VMEM((n_chunks, 8, LANES), F32),
               pltpu.VMEM((n_chunks, LANES, LANES), BF16),
               pltpu.VMEM((n_chunks, LANES, LANES), F32),
               pltpu.VMEM((seq, LANES), BF16),
               pltpu.VMEM((seq, LANES), F32),
               pltpu.VMEM((n_chunks, LANES, LANES), BF16)]),
        compiler_params=_params("parallel", "parallel"),
        name="rwkv7",
    )(p3, p3, p3, ps, vecs, lora)


def _fgate_kernel(f_ref, bf_ref, col_ref, row_ref, *, seq):
    nblk = seq // LANES
    ri = lax.broadcasted_iota(jnp.int32, (LANES, LANES), 0)
    ci = lax.broadcasted_iota(jnp.int32, (LANES, LANES), 1)
    tri = (ci <= ri).astype(BF16)

    def body(i, carry):
        t0 = pl.multiple_of(i * LANES, LANES)
        z = f_ref[0, pl.ds(t0, LANES), :] + bf_ref[...]
        logf = jnp.minimum(z, 0.0) - jnp.log1p(jnp.exp(-jnp.abs(z)))
        hi = logf.astype(BF16)
        r1 = logf - hi.astype(F32)
        mid = r1.astype(BF16)
        lo = (r1 - mid.astype(F32)).astype(BF16)
        cum = carry + (_dot(tri, hi) + _dot(tri, mid) + _dot(tri, lo))
        col_ref[0, pl.ds(t0, LANES), :] = cum
        row_ref[0, :, pl.ds(t0, LANES)] = cum.T
        return cum[LANES - 1:LANES, :]

    lax.fori_loop(0, nblk, body, jnp.zeros((1, LANES), F32))


def _fgate(p_sm, b_f_pad, off, bsz, seq):
    ps = p_sm.reshape(bsz, seq, p_sm.shape[-1])
    return pl.pallas_call(
        functools.partial(_fgate_kernel, seq=seq),
        grid=(bsz,),
        in_specs=[pl.BlockSpec((1, seq, LANES), lambda b: (b, 0, off // LANES)),
                  pl.BlockSpec((1, LANES), lambda b: (0, 0))],
        out_specs=[pl.BlockSpec((1, seq, LANES), lambda b: (b, 0, 0)),
                   pl.BlockSpec((1, LANES, seq), lambda b: (b, 0, 0))],
        out_shape=[jax.ShapeDtypeStruct((bsz, seq, LANES), F32),
                   jax.ShapeDtypeStruct((bsz, LANES, seq), F32)],
        compiler_params=_params("parallel"),
        name="fox_gates",
    )(ps, b_f_pad)


FOX_HEADS_PER_STEP = 2
LOG2E = 1.4426950408889634


def _fox_kernel(q_ref, k_ref, v_ref, ccol_ref, crow_ref, on_ref, o_ref, *, tq, seq, scale):
    hg = pl.program_id(1)
    heads = range(FOX_HEADS_PER_STEP)
    hsl = [slice(g * FOX_HEAD, (g + 1) * FOX_HEAD) for g in heads]
    lane = lax.broadcasted_iota(jnp.int32, (tq, LANES), 1)
    causal = (lax.broadcasted_iota(jnp.int32, (tq, tq), 1)
              <= lax.broadcasted_iota(jnp.int32, (tq, tq), 0))
    ones_col = jnp.where(lane == 0, 1.0, 0.0).astype(BF16)

    for i in range(seq // tq):
        qrows = pl.ds(i * tq, tq)
        q = [q_ref[0, qrows, sl] for sl in hsl]
        ccol = ccol_ref[0, qrows, :]
        cq = [LOG2E * jnp.sum(jnp.where(lane == hg * FOX_HEADS_PER_STEP + g, ccol, 0.0),
                              axis=-1, keepdims=True) for g in heads]

        def scores(j, q=q, cq=cq, i=i):
            krows = pl.ds(j * tq, tq)
            s = [_dot_nt(q[g], k_ref[0, krows, hsl[g]]) * (scale * LOG2E)
                 + cq[g] - LOG2E * crow_ref[0, g, :, krows] for g in heads]
            if j == i:
                s = [jnp.where(causal, x, -jnp.inf) for x in s]
            return s

        m = [jnp.full((tq, 1), -jnp.inf, F32) for _ in heads]
        acc = [jnp.zeros((tq, 2 * FOX_HEAD), F32) for _ in heads]
        s = scores(0)
        for j in range(i + 1):
            s_next = scores(j + 1) if j < i else None
            vrows = pl.ds(j * tq, tq)
            m_new = [jnp.maximum(m[g], jnp.max(s[g], axis=-1, keepdims=True)) for g in heads]
            alpha = [jnp.exp2(m[g] - m_new[g]) for g in heads]
            p = [jnp.exp2(s[g] - m_new[g]).astype(BF16) for g in heads]
            acc = [alpha[g] * acc[g]
                   + _dot(p[g], jnp.concatenate([v_ref[0, vrows, hsl[g]], ones_col], axis=1))
                   for g in heads]
            m, s = m_new, s_next
        for g in heads:
            o = acc[g][:, :FOX_HEAD] / acc[g][:, FOX_HEAD:FOX_HEAD + 1]
            o = o * lax.rsqrt(jnp.mean(o * o, axis=-1, keepdims=True) + NORM_EPS) * on_ref[g]
            o_ref[0, qrows, hsl[g]] = o.astype(o_ref.dtype)


def _fox(p_fox, ccol, crow, out_norm, bsz, seq):
    fw = p_fox.shape[-1] // 3
    nh = fw // FOX_HEAD
    hps = FOX_HEADS_PER_STEP
    ng = nh // hps
    wid = hps * FOX_HEAD
    tq = _pick(seq, (512, 256, 128))
    p3 = p_fox.reshape(bsz, seq, 3 * fw)
    crow4 = crow.reshape(bsz, LANES, 1, seq)
    on = out_norm.reshape(nh, 1, FOX_HEAD)
    return pl.pallas_call(
        functools.partial(_fox_kernel, tq=tq, seq=seq, scale=FOX_HEAD ** -0.5),
        grid=(bsz, ng),
        in_specs=[pl.BlockSpec((1, seq, wid), lambda b, h: (b, 0, h)),
                  pl.BlockSpec((1, seq, wid), lambda b, h: (b, 0, ng + h)),
                  pl.BlockSpec((1, seq, wid), lambda b, h: (b, 0, 2 * ng + h)),
                  pl.BlockSpec((1, seq, LANES), lambda b, h: (b, 0, 0)),
                  pl.BlockSpec((1, hps, 1, seq), lambda b, h: (b, h, 0, 0)),
                  pl.BlockSpec((hps, 1, FOX_HEAD), lambda b, h: (h, 0, 0))],
        out_specs=pl.BlockSpec((1, seq, wid), lambda b, h: (b, 0, h)),
        out_shape=jax.ShapeDtypeStruct((bsz, seq, fw), BF16),
        compiler_params=_params("parallel", "parallel"),
        name="fox_attention",
    )(p3, p3, p3, ccol, crow4, on)


def _pad_rows(w, rows):
    return jnp.pad(w, ((0, rows - w.shape[0]), (0, 0)))


def _ffn(x2, h, w_gate, w_up, w_down, gate, seq, side=None):
    dff = w_gate.shape[1]
    dffp = _round_up(dff, 512)
    act, wd = _swiglu_up(h, w_gate, w_up, w_down, dffp)
    return _matmul_residual(act, wd, x2, gate, seq, 0.5, dff, side)


def kernel(x, c, w_mod, b_mod, norm_ffn1, ffn1_gate, ffn1_up, ffn1_down, norm_mix, w_in, rwkv_mu, rwkv_w0, rwkv_w2, rwkv_a0, rwkv_a2, rwkv_g2, rwkv_k_k, rwkv_k_a, rwkv_r_k, rwkv_ln_w, rwkv_ln_b, fox_b_f, fox_out_norm, w_out, norm_ffn2, ffn2_gate, ffn2_up, ffn2_down, norm_final):
    bsz, seq, d = x.shape
    depth = w_mod.shape[0]
    rw = rwkv_w0.shape[-1]
    fw = fox_out_norm.shape[-1]
    nfh = fox_b_f.shape[-1]
    dl, al, gl = rwkv_w2.shape[1], rwkv_a2.shape[1], rwkv_g2.shape[1]
    dlp, alp, glp = (_round_up(n, LANES) for n in (dl, al, gl))
    c3 = 3 * rw
    c4, c5, c6 = c3 + dl, c3 + dl + al, c3 + dl + al + gl

    x2 = x.reshape(bsz * seq, d)
    for l in range(depth):
        mod = _mod(c, w_mod[l], b_mod[l])
        sh1, sc1, gt1, sh2, sc2, gt2, sh3, sc3, gt3 = (
            m[:, None, :] for m in jnp.split(mod, 9, axis=-1))

        h = _norm_mod(x2, norm_ffn1[l], sc1, sh1, seq)
        x2, wt = _ffn(x2, h, ffn1_gate[l], ffn1_up[l], ffn1_down[l], gt1, seq,
                      side=jnp.swapaxes(w_in[l], 0, 1))

        h = _norm_mod(x2, norm_mix[l], sc2, sh2, seq)
        mu = rwkv_mu[l]
        wt_sm = jnp.concatenate(
            [_pad_rows(wt[c3:c4], dlp), _pad_rows(wt[c4:c5], alp),
             _pad_rows(wt[c5:c6], glp), _pad_rows(wt[c6 + 3 * fw:], LANES)], axis=0)
        mu_rkv = mu[:c3].reshape(1, c3)
        mu_sm = jnp.concatenate(
            [jnp.pad(mu[c3:c4], (0, dlp - dl)), jnp.pad(mu[c4:c5], (0, alp - al)),
             jnp.pad(mu[c5:c6], (0, glp - gl))]).reshape(1, dlp + alp + glp)
        p_rkv = _matmul_nt(h, wt, F32, 0, c3)
        p_sm = _matmul_nt(h, wt_sm, F32)
        p_fox = _matmul_nt(h, wt, BF16, c6, 3 * fw)

        y_r = _rwkv(p_rkv, p_sm, mu_rkv, mu_sm,
                    rwkv_w0[l], rwkv_a0[l], rwkv_k_k[l], rwkv_k_a[l], rwkv_r_k[l],
                    rwkv_ln_w[l], rwkv_ln_b[l],
                    _pad_rows(rwkv_w2[l], dlp).astype(BF16),
                    _pad_rows(rwkv_a2[l], alp).astype(BF16),
                    _pad_rows(rwkv_g2[l], glp).astype(BF16), bsz, seq)

        b_f_pad = jnp.pad(fox_b_f[l], (0, LANES - nfh)).reshape(1, LANES)
        ccol, crow = _fgate(p_sm, b_f_pad, dlp + alp + glp, bsz, seq)
        y_f = _fox(p_fox, ccol, crow, fox_out_norm[l], bsz, seq)

        x2 = _matmul_residual2(y_r.reshape(bsz * seq, rw), y_f.reshape(bsz * seq, fw),
                               w_out[l].astype(BF16), x2, gt2, seq, 1.0)

        h = _norm_mod(x2, norm_ffn2[l], sc3, sh3, seq)
        x2 = _ffn(x2, h, ffn2_gate[l], ffn2_up[l], ffn2_down[l], gt3, seq)
    return _final_norm(x2, norm_final).reshape(bsz, seq, d)
```

```python
import functools

import jax
import jax.numpy as jnp
from jax import lax
from jax.experimental import pallas as pl
from jax.experimental.pallas import tpu as pltpu

F32 = jnp.float32
BF16 = jnp.bfloat16

LANES = 128
RWKV_HEAD = 64
FOX_HEAD = 128
CHUNK = 64
INV_BASE = 4
GROUP = 16
NORM_EPS = 1e-6
LN_X_EPS = 64e-5
VMEM_LIMIT = 56 * 1024 * 1024


def _round_up(n, m):
    return (n + m - 1) // m * m


def _pick(n, candidates):
    for c in candidates:
        if n % c == 0:
            return c
    return n


def _dot(a, b):
    return jnp.dot(a, b, preferred_element_type=F32)


def _dot_nt(a, b):
    return lax.dot_general(a, b, (((1,), (1,)), ((), ())), preferred_element_type=F32)


def _dot_tn(a, b):
    return lax.dot_general(a, b, (((0,), (0,)), ((), ())), preferred_element_type=F32)


def _params(*sem):
    return pltpu.CompilerParams(dimension_semantics=sem, vmem_limit_bytes=VMEM_LIMIT)


def _mod_kernel(c_ref, w_ref, b_ref, o_ref):
    c = c_ref[...]
    a = (c * jax.nn.sigmoid(c)).astype(BF16)
    o_ref[...] = _dot(a, w_ref[...].astype(BF16)) + b_ref[...]


def _mod(c, w_mod, b_mod):
    bsz, d = c.shape
    n = w_mod.shape[1]
    rows = _round_up(bsz, 16)
    c_pad = jnp.pad(c, ((0, rows - bsz), (0, 0)))
    tn = _pick(n, (1024, 512, 256, 128))
    out = pl.pallas_call(
        _mod_kernel,
        grid=(n // tn,),
        in_specs=[pl.BlockSpec((rows, d), lambda j: (0, 0)),
                  pl.BlockSpec((d, tn), lambda j: (0, j)),
                  pl.BlockSpec((1, tn), lambda j: (0, j))],
        out_specs=pl.BlockSpec((rows, tn), lambda j: (0, j)),
        out_shape=jax.ShapeDtypeStruct((rows, n), F32),
        compiler_params=_params("parallel"),
        name="mod_matmul",
    )(c_pad, w_mod, b_mod.reshape(1, n))
    return out[:bsz]


def _norm_kernel(x_ref, g_ref, sc_ref, sh_ref, o_ref):
    x = x_ref[...]
    y = x * lax.rsqrt(jnp.mean(x * x, axis=-1, keepdims=True) + NORM_EPS) * g_ref[...]
    o_ref[...] = (y * (1.0 + sc_ref[0]) + sh_ref[0]).astype(o_ref.dtype)


def _final_norm_kernel(x_ref, g_ref, o_ref):
    x = x_ref[...]
    o_ref[...] = x * lax.rsqrt(jnp.mean(x * x, axis=-1, keepdims=True) + NORM_EPS) * g_ref[...]


def _norm_mod(x2, g, sc, sh, seq):
    t, d = x2.shape
    ts = _pick(seq, (512, 256, 128, 64, 8))
    return pl.pallas_call(
        _norm_kernel,
        grid=(t // ts,),
        in_specs=[pl.BlockSpec((ts, d), lambda i: (i, 0)),
                  pl.BlockSpec((1, d), lambda i: (0, 0)),
                  pl.BlockSpec((1, 1, d), lambda i: (i * ts // seq, 0, 0)),
                  pl.BlockSpec((1, 1, d), lambda i: (i * ts // seq, 0, 0))],
        out_specs=pl.BlockSpec((ts, d), lambda i: (i, 0)),
        out_shape=jax.ShapeDtypeStruct((t, d), BF16),
        compiler_params=_params("parallel"),
        name="norm_mod",
    )(x2, g.reshape(1, d), sc, sh)


def _final_norm(x2, g):
    t, d = x2.shape
    ts = _pick(t, (512, 256, 128, 64, 8))
    return pl.pallas_call(
        _final_norm_kernel,
        grid=(t // ts,),
        in_specs=[pl.BlockSpec((ts, d), lambda i: (i, 0)),
                  pl.BlockSpec((1, d), lambda i: (0, 0))],
        out_specs=pl.BlockSpec((ts, d), lambda i: (i, 0)),
        out_shape=jax.ShapeDtypeStruct((t, d), F32),
        compiler_params=_params("parallel"),
        name="final_norm",
    )(x2, g.reshape(1, d))


def _mm_nt_kernel(x_ref, wt_ref, o_ref):
    o_ref[...] = _dot_nt(x_ref[...], wt_ref[...]).astype(o_ref.dtype)


def _matmul_nt(x, wt, out_dtype, row0=0, n=None):
    m, kd = x.shape
    n = wt.shape[0] if n is None else n
    tm = _pick(m, (1024, 512, 256, 128))
    tn = _pick(n, (1024, 512, 640, 256, 128))
    return pl.pallas_call(
        _mm_nt_kernel,
        grid=(m // tm, n // tn),
        in_specs=[pl.BlockSpec((tm, kd), lambda i, j: (i, 0)),
                  pl.BlockSpec((pl.Element(tn), pl.Element(kd)),
                               lambda i, j: ((row0 // 16 + j * (tn // 16)) * 16, 0))],
        out_specs=pl.BlockSpec((tm, tn), lambda i, j: (i, j)),
        out_shape=jax.ShapeDtypeStruct((m, n), out_dtype),
        compiler_params=_params("parallel", "parallel"),
        name="matmul_nt",
    )(x, wt)


def _swiglu_kernel(x_ref, wg_hbm, wu_hbm, wd_ref, o_ref, wdb_ref, stage_ref, wgb_ref, wub_ref,
                   sem, *, n_full, rem, tn, wd_blocks):
    j = pl.program_id(0)
    step = j * pl.num_programs(1) + pl.program_id(1)

    def full_copies(jj):
        cols = pl.ds(pl.multiple_of(jj * tn, tn), tn)
        return (pltpu.make_async_copy(wg_hbm.at[:, cols], stage_ref.at[0], sem.at[0]),
                pltpu.make_async_copy(wu_hbm.at[:, cols], stage_ref.at[1], sem.at[1]))

    def tail_copies():
        cols = pl.ds(n_full * tn, rem)
        return (pltpu.make_async_copy(wg_hbm.at[:, cols], stage_ref.at[0, :, :rem], sem.at[0]),
                pltpu.make_async_copy(wu_hbm.at[:, cols], stage_ref.at[1, :, :rem], sem.at[1]))

    def start(jj):
        @pl.when(jj < n_full)
        def _():
            for c in full_copies(jj):
                c.start()

        if rem:
            @pl.when(jj == n_full)
            def _():
                for c in tail_copies():
                    c.start()

    @pl.when(pl.program_id(1) == 0)
    def _():
        @pl.when(j == 0)
        def _():
            start(j)

        @pl.when(j < n_full)
        def _():
            for c in full_copies(j):
                c.wait()
            wgb_ref[...] = stage_ref[0].astype(BF16)
            wub_ref[...] = stage_ref[1].astype(BF16)

        if rem:
            @pl.when(j == n_full)
            def _():
                for c in tail_copies():
                    c.wait()
                wgb_ref[:, :rem] = stage_ref[0, :, :rem].astype(BF16)
                wub_ref[:, :rem] = stage_ref[1, :, :rem].astype(BF16)

        start(j + 1)

    def compute(ncols):
        wdb_ref[...] = jnp.where(step < wd_blocks, wd_ref[...], 0.0).astype(wdb_ref.dtype)
        x = x_ref[...]
        g = _dot(x, wgb_ref[:, :ncols])
        u = _dot(x, wub_ref[:, :ncols])
        o_ref[:, :ncols] = (g * (0.5 * jnp.tanh(0.5 * g) + 0.5) * u).astype(o_ref.dtype)
        if ncols < tn:
            o_ref[:, ncols:] = jnp.zeros((o_ref.shape[0], tn - ncols), o_ref.dtype)

    if rem:
        pl.when(j < n_full)(lambda: compute(tn))
        pl.when(j == n_full)(lambda: compute(rem))
    else:
        compute(tn)


def _swiglu_up(h, wg, wu, w_down, n_out):
    m, kd = h.shape
    n = wg.shape[1]
    d_out = w_down.shape[1]
    tm = _pick(m, (1024, 512, 256, 128))
    tn = 512
    n_full, rem = divmod(n, tn)
    assert rem % LANES == 0 and n_out == _round_up(n, tn)
    nj, ni = n_out // tn, m // tm
    slab, left = divmod(n_out, nj * ni)
    assert left == 0 and slab % 16 == 0 and n % slab == 0
    wd_blocks = n // slab
    return pl.pallas_call(
        functools.partial(_swiglu_kernel, n_full=n_full, rem=rem, tn=tn, wd_blocks=wd_blocks),
        grid=(nj, ni),
        in_specs=[pl.BlockSpec((tm, kd), lambda j, i: (i, 0)),
                  pl.BlockSpec(memory_space=pl.ANY),
                  pl.BlockSpec(memory_space=pl.ANY),
                  pl.BlockSpec((slab, d_out),
                               lambda j, i: (jnp.minimum(j * ni + i, wd_blocks - 1), 0))],
        out_specs=[pl.BlockSpec((tm, tn), lambda j, i: (i, j)),
                   pl.BlockSpec((slab, d_out), lambda j, i: (j * ni + i, 0))],
        out_shape=[jax.ShapeDtypeStruct((m, n_out), BF16),
                   jax.ShapeDtypeStruct((n_out, d_out), BF16)],
        scratch_shapes=[pltpu.VMEM((2, kd, tn), F32),
                        pltpu.VMEM((kd, tn), BF16), pltpu.VMEM((kd, tn), BF16),
                        pltpu.SemaphoreType.DMA((2,))],
        compiler_params=_params("arbitrary", "arbitrary"),
        name="swiglu_up",
    )(h, wg, wu, w_down)


def _resid_kernel(a_ref, w_ref, x_ref, gt_ref, *rest, nk, k_last, scale):
    side_in, o_ref, side_out, acc_ref = rest if len(rest) == 4 else (None, rest[0], None, rest[1])
    k = pl.program_id(2)

    def side_cast():
        if side_in is not None:
            side_out[...] = side_in[...].astype(side_out.dtype)

    @pl.when(k == 0)
    def _():
        side_cast()
        acc_ref[...] = _dot(a_ref[...], w_ref[...])

    @pl.when((k > 0) & (k < nk - 1))
    def _():
        side_cast()
        acc_ref[...] += _dot(a_ref[...], w_ref[...])

    @pl.when(k == nk - 1)
    def _():
        side_cast()
        y = acc_ref[...] + _dot(a_ref[:, :k_last], w_ref[:k_last, :])
        o_ref[...] = x_ref[...] + (scale * gt_ref[0]) * y


def _matmul_residual(a, w, x2, gate, seq, scale, k_valid, side=None):
    m, kd = a.shape
    n = w.shape[1]
    tm = _pick(seq, (1024, 512, 256, 128))
    tn = _pick(n, (1024, 512, 256, 128))
    tk = _pick(kd, (2816, 1024, 512, 256, 128))
    nk = kd // tk
    k_last = k_valid - (nk - 1) * tk
    assert nk >= 2 and 0 < k_last <= tk and k_last % LANES == 0
    ni, nj = m // tm, n // tn
    in_specs = [pl.BlockSpec((tm, tk), lambda i, j, k: (i, k)),
                pl.BlockSpec((tk, tn), lambda i, j, k: (k, j)),
                pl.BlockSpec((tm, tn), lambda i, j, k: (i, j)),
                pl.BlockSpec((1, 1, tn), lambda i, j, k: (i * tm // seq, 0, j))]
    out_specs = [pl.BlockSpec((tm, tn), lambda i, j, k: (i, j))]
    out_shape = [jax.ShapeDtypeStruct((m, n), F32)]
    operands = [a, w, x2, gate]
    semantics = ("parallel", "parallel", "arbitrary")
    if side is not None:
        rows, cols = side.shape
        slab = _round_up(-(-rows // (ni * nj * nk)), 16)
        last = -(-rows // slab) - 1

        def side_map(i, j, k):
            return (jnp.minimum((i * nj + j) * nk + k, last), 0)

        in_specs.append(pl.BlockSpec((slab, cols), side_map))
        out_specs.append(pl.BlockSpec((slab, cols), side_map))
        out_shape.append(jax.ShapeDtypeStruct((rows, cols), BF16))
        operands.append(side)
        semantics = ("arbitrary", "arbitrary", "arbitrary")
    res = pl.pallas_call(
        functools.partial(_resid_kernel, nk=nk, k_last=k_last, scale=scale),
        grid=(ni, nj, nk),
        in_specs=in_specs,
        out_specs=out_specs,
        out_shape=out_shape,
        scratch_shapes=[pltpu.VMEM((tm, tn), F32)],
        compiler_params=_params(*semantics),
        name="matmul_residual",
    )(*operands)
    return res[0] if side is None else res


def _resid2_kernel(a1_ref, a2_ref, w1_ref, w2_ref, x_ref, gt_ref, o_ref, *, scale):
    y = _dot(a1_ref[...], w1_ref[...]) + _dot(a2_ref[...], w2_ref[...])
    o_ref[...] = x_ref[...] + (scale * gt_ref[0]) * y


def _matmul_residual2(a1, a2, w, x2, gate, seq, scale):
    m, k1 = a1.shape
    k2 = a2.shape[1]
    n = w.shape[1]
    assert k1 == k2
    tm = _pick(seq, (1024, 512, 256, 128))
    tn = _pick(n, (512, 256, 128))
    return pl.pallas_call(
        functools.partial(_resid2_kernel, scale=scale),
        grid=(m // tm, n // tn),
        in_specs=[pl.BlockSpec((tm, k1), lambda i, j: (i, 0)),
                  pl.BlockSpec((tm, k2), lambda i, j: (i, 0)),
                  pl.BlockSpec((k1, tn), lambda i, j: (0, j)),
                  pl.BlockSpec((k2, tn), lambda i, j: (1, j)),
                  pl.BlockSpec((tm, tn), lambda i, j: (i, j)),
                  pl.BlockSpec((1, 1, tn), lambda i, j: (i * tm // seq, 0, j))],
        out_specs=pl.BlockSpec((tm, tn), lambda i, j: (i, j)),
        out_shape=jax.ShapeDtypeStruct((m, n), F32),
        compiler_params=_params("parallel", "parallel"),
        name="out_proj_residual",
    )(a1, a2, w, w, x2, gate)


ROW_TILE = 512


def _lane_head0(shape):
    return lax.broadcasted_iota(jnp.int32, shape, 1) < RWKV_HEAD


def _stack(x):
    m0 = _lane_head0(x.shape)
    zero = jnp.zeros_like(x)
    return jnp.concatenate([jnp.where(m0, x, zero), jnp.where(m0, zero, x)], axis=0)


def _split_dot(lhs_bf16, x):
    hi = x.astype(BF16)
    lo = (x - hi.astype(F32)).astype(BF16)
    return _dot(lhs_bf16, hi) + _dot(lhs_bf16, lo)


def _lora_in_kernel(ps_ref, prev_ref, mu_ref, o_ref, *, lora_w, lora_a):
    x = ps_ref[0]
    prev_row = jnp.where(pl.program_id(1) == 0, 0.0, prev_ref[0][7:8, :])
    row = lax.broadcasted_iota(jnp.int32, x.shape, 0)
    prev = jnp.where(row == 0, prev_row, pltpu.roll(x, 1, 0))
    sm = x + (prev - x) * mu_ref[...]
    o_ref[0, :, :lora_w] = jnp.tanh(sm[:, :lora_w]).astype(o_ref.dtype)
    o_ref[0, :, lora_w:lora_w + lora_a] = sm[:, lora_w:lora_w + lora_a].astype(o_ref.dtype)
    o_ref[0, :, lora_w + lora_a:] = jax.nn.sigmoid(sm[:, lora_w + lora_a:]).astype(o_ref.dtype)


def _lora_in(p_sm, mu_sm, lora_w, lora_a, bsz, seq):
    wsm = mu_sm.shape[-1]
    ps = p_sm.reshape(bsz, seq, p_sm.shape[-1])
    ts = _pick(seq, (ROW_TILE, 128, 64, 8))
    return pl.pallas_call(
        functools.partial(_lora_in_kernel, lora_w=lora_w, lora_a=lora_a),
        grid=(bsz, seq // ts),
        in_specs=[pl.BlockSpec((1, ts, wsm), lambda b, i: (b, i, 0)),
                  pl.BlockSpec((1, 8, wsm), lambda b, i: (b, jnp.maximum(i * (ts // 8) - 1, 0), 0)),
                  pl.BlockSpec((1, wsm), lambda b, i: (0, 0))],
        out_specs=pl.BlockSpec((1, ts, wsm), lambda b, i: (b, i, 0)),
        out_shape=jax.ShapeDtypeStruct((bsz, seq, wsm), BF16),
        compiler_params=_params("parallel", "parallel"),
        name="rwkv_lora_in",
    )(ps, ps, mu_sm)


(VEC_MU_R, VEC_MU_K, VEC_MU_V, VEC_W0, VEC_A0, VEC_KK, VEC_KA, VEC_RK, VEC_LNW,
 VEC_LNB) = range(10)
VEC_ROWS = 16


def _rwkv_kernel(pr_ref, pk_ref, pv_ref, ps_ref, vec_ref, lora_ref, o_ref,
                 at_s, rt_s, kt_s, bt_s, kg_s, bg_s, v_s, g_s, bonus_s, gc_s,
                 m_s, n_s, q_s, yb_s, sb_s, *, seq, lora_w, lora_a):
    n_chunks = seq // CHUNK
    cpt = ROW_TILE // CHUNK

    def vec(idx):
        return vec_ref[idx:idx + 1, :]

    ri = lax.broadcasted_iota(jnp.int32, (LANES, LANES), 0)
    ci = lax.broadcasted_iota(jnp.int32, (LANES, LANES), 1)
    bd_mask = (ri // RWKV_HEAD) == (ci // RWKV_HEAD)
    bd_ones = bd_mask.astype(BF16)
    tri = ((ci <= ri) & bd_mask).astype(BF16)

    def head_sum(x):
        return _dot(x.astype(BF16), bd_ones)

    def shift_mix(ref, t0, mu):
        x = ref[0, pl.ds(t0, ROW_TILE), :]
        if t0 == 0:
            prev_row = jnp.zeros((1, x.shape[1]), F32)
        else:
            prev_row = ref[0, pl.ds(t0 - 8, 8), :][7:8, :]
        row = lax.broadcasted_iota(jnp.int32, x.shape, 0)
        prev = jnp.where(row == 0, prev_row, pltpu.roll(x, 1, 0))
        return x + (prev - x) * mu

    def phase_a_stages(i):
        t0 = i * ROW_TILE
        rows = pl.ds(t0, ROW_TILE)
        st = {}

        def load_and_lora():
            st["r"] = shift_mix(pr_ref, t0, vec(VEC_MU_R))
            st["k"] = shift_mix(pk_ref, t0, vec(VEC_MU_K))
            st["v"] = shift_mix(pv_ref, t0, vec(VEC_MU_V))
            st["wlin"] = vec(VEC_W0) + _dot(ps_ref[0, rows, :lora_w], lora_ref[:lora_w, :])
            st["alin"] = vec(VEC_A0) + _dot(ps_ref[0, rows, lora_w:lora_w + lora_a],
                                            lora_ref[lora_w:lora_w + lora_a, :])
            st["g"] = _dot(ps_ref[0, rows, lora_w + lora_a:], lora_ref[lora_w + lora_a:, :])

        def decay_and_norm():
            wlin = st.pop("wlin")
            wlog = -(jnp.maximum(-wlin, 0.0) + jnp.log(1.0 + jnp.exp(-jnp.abs(wlin)))) - 0.5
            st["ld"] = -jnp.exp(wlog)
            st["alr"] = jax.nn.sigmoid(st.pop("alin"))
            st["kk"] = st["k"] * vec(VEC_KK)
            st["kk_ss"] = head_sum(st["kk"] * st["kk"])
            st["cum"] = jnp.concatenate(
                [_split_dot(tri, st["ld"][r:r + LANES]) for r in range(0, ROW_TILE, LANES)], axis=0)

        def keys_and_bonus():
            st["kk"] = st["kk"] * lax.rsqrt(jnp.maximum(st.pop("kk_ss"), 1e-24))
            st["k"] = st["k"] * (1.0 + (st["alr"] - 1.0) * vec(VEC_KA))
            st["bonus_sum"] = head_sum(st["r"] * st["k"] * vec(VEC_RK))
            st["b"] = st["kk"] * st["alr"]

        def scale_and_store():
            cum, ld = st["cum"], st.pop("ld")
            g_dec = jnp.exp(cum)
            g_inv = jnp.exp(-cum)
            at_s[rows, :] = (-st.pop("kk") * jnp.exp(cum - ld)).astype(BF16)
            rt_s[rows, :] = (st.pop("r") * g_dec).astype(BF16)
            kt_s[rows, :] = (st["k"] * g_inv).astype(BF16)
            bt_s[rows, :] = (st["b"] * g_inv).astype(BF16)
            v_s[rows, :] = st["v"].astype(BF16)
            g_s[rows, :] = st.pop("g")
            bonus_s[rows, :] = st.pop("bonus_sum") * st.pop("v")

        def chunk_tails():
            cum, k, b = st.pop("cum"), st.pop("k"), st.pop("b")
            for c in range(cpt):
                lo, hi = c * CHUNK, (c + 1) * CHUNK
                cum_c = cum[hi - 1:hi, :]
                rem = jnp.exp(cum_c - cum[lo:hi, :])
                kg_s[pl.ds(t0 + lo, CHUNK), :] = (k[lo:hi, :] * rem).astype(BF16)
                bg_s[pl.ds(t0 + lo, CHUNK), :] = (b[lo:hi, :] * rem).astype(BF16)
                gc_s[i * cpt + c] = jnp.broadcast_to(jnp.exp(cum_c), (8, LANES))

        return [load_and_lora, decay_and_norm, keys_and_bonus, scale_and_store, chunk_tails]

    row = lax.broadcasted_iota(jnp.int32, (CHUNK, LANES), 0)
    col = lax.broadcasted_iota(jnp.int32, (CHUNK, LANES), 1) % RWKV_HEAD
    strict = col < row
    incl = col <= row
    eye_p = (col == row).astype(F32)

    def same_block(size):
        return (row // size) == (col // size)

    def each(fn, *lists):
        return [fn(*xs) for xs in zip(*lists)]

    def phase_b(i, tick):
        cs = [i * GROUP + g for g in range(GROUP)]
        sls = [pl.ds(c * CHUNK, CHUNK) for c in cs]
        at = [at_s[sl, :] for sl in sls]
        rt = [rt_s[sl, :] for sl in sls]
        kt = [kt_s[sl, :] for sl in sls]
        bt = [bt_s[sl, :] for sl in sls]
        v = [v_s[sl, :] for sl in sls]
        sc = each(lambda a, r, k, b: _dot_nt(jnp.concatenate([a, r], axis=0),
                                             jnp.concatenate([_stack(k), _stack(b)], axis=0)),
                  at, rt, kt, bt)
        a_ak = [jnp.where(strict, x[:CHUNK, :LANES], 0.0).astype(BF16) for x in sc]
        a_ab = [jnp.where(strict, x[:CHUNK, LANES:], 0.0) for x in sc]
        a_rk = [jnp.where(incl, x[CHUNK:, :LANES], 0.0).astype(BF16) for x in sc]
        a_rb = [jnp.where(incl, x[CHUNK:, LANES:], 0.0).astype(BF16) for x in sc]
        tick()
        d = [jnp.where(same_block(INV_BASE), x, 0.0) for x in a_ab]
        db = [x.astype(BF16) for x in d]
        d2 = [_dot(x, _stack(x)).astype(BF16) for x in db]
        tick()
        p = [eye_p + x for x in d]
        p = each(lambda pp, x2: pp + _dot(pp.astype(BF16), _stack(x2)), p, d2)
        tick()
        s = INV_BASE
        while s < CHUNK:
            lower_left = same_block(2 * s) & ((row // s) % 2 == 1) & ((col // s) % 2 == 0)
            a21 = [jnp.where(lower_left, x, 0.0).astype(BF16) for x in a_ab]
            pb = [x.astype(BF16) for x in p]
            a21_t11 = each(lambda a, t: _dot(a, _stack(t)).astype(BF16), a21, pb)
            tick()
            p = each(lambda pp, t, y: pp + _dot(t, _stack(y)), p, pb, a21_t11)
            tick()
            s *= 2
        tinv = [x.astype(BF16) for x in p]
        vst = [_stack(x) for x in v]
        x1 = each(lambda a, vs: _dot(a, vs).astype(BF16), a_ak, vst)
        tick()
        wu = each(lambda t, a, x: _dot(t, jnp.concatenate([_stack(a), _stack(x)], axis=1)),
                  tinv, at, x1)
        tick()
        wb = [x[:, :LANES].astype(BF16) for x in wu]
        ub = [x[:, LANES:].astype(BF16) for x in wu]
        qy = each(lambda a, w, u: _dot(a, jnp.concatenate([_stack(w), _stack(u)], axis=1)),
                  a_rb, wb, ub)
        tick()
        yb = each(lambda a, vs, x: _dot(a, vs) + x[:, LANES:], a_rk, vst, qy)
        tick()
        for g in range(GROUP):
            q_s[sls[g], :] = (rt[g].astype(F32) + qy[g][:, :LANES]).astype(BF16)
            yb_s[sls[g], :] = yb[g]
        kg = [kg_s[sl, :] for sl in sls]
        bg = [bg_s[sl, :] for sl in sls]
        mm = each(_dot_tn, wb, bg)
        tick()
        nn = each(lambda vv, u, k, b: _dot_tn(jnp.concatenate([vv, u], axis=0),
                                              jnp.concatenate([k, b], axis=0)), v, ub, kg, bg)
        tick()
        for g in range(GROUP):
            m_s[cs[g]] = jnp.where(bd_mask, mm[g], 0.0).astype(BF16)
            n_s[cs[g]] = jnp.where(bd_mask, nn[g], 0.0)

    def c1_step(c, s):
        sb = s.astype(BF16)
        sb_s[c] = sb
        return s * gc_s[c][0:1, :] + _dot(sb, m_s[c]) + n_s[c]

    n_groups = n_chunks // GROUP
    tiles_per_group = GROUP * CHUNK // ROW_TILE
    state = [jnp.zeros((LANES, LANES), F32)]

    def c1_thunk(c):
        def run():
            state[0] = c1_step(c, state[0])
        return run

    def prep_stages(group):
        per_tile = [phase_a_stages(group * tiles_per_group + t) for t in range(tiles_per_group)]
        return [stage for stages in zip(*per_tile) for stage in stages]

    def interleave(xs, ys):
        out = []
        for j in range(max(len(xs), len(ys))):
            out += xs[j:j + 1] + ys[j:j + 1]
        return out

    def run_with_fillers(body, fillers):
        def tick():
            if fillers:
                fillers.pop(0)()
        body(tick)
        while fillers:
            fillers.pop(0)()

    for stage in prep_stages(0):
        stage()
    for i in range(n_groups):
        fillers = interleave(
            [c1_thunk((i - 1) * GROUP + g) for g in range(GROUP)] if i > 0 else [],
            prep_stages(i + 1) if i + 1 < n_groups else [])
        run_with_fillers(functools.partial(phase_b, i), fillers)

    inv_n = 1.0 / RWKV_HEAD

    def phase_c2(i, tick):
        cs = [i * GROUP + g for g in range(GROUP)]
        sls = [pl.ds(c * CHUNK, CHUNK) for c in cs]

        def staged(fn, *lists):
            out = []
            for g, xs in enumerate(zip(*lists)):
                out.append(fn(*xs))
                if g % 4 == 3:
                    tick()
            return out

        y = staged(lambda sl, c: _dot_nt(q_s[sl, :], sb_s[c]) + yb_s[sl, :], sls, cs)
        mean = staged(lambda x: head_sum(x) * inv_n, y)
        yc = each(lambda a, b: a - b, y, mean)
        var = staged(lambda x: head_sum(x * x) * inv_n, yc)
        for g in range(GROUP):
            yn = yc[g] * lax.rsqrt(var[g] + LN_X_EPS) * vec(VEC_LNW) + vec(VEC_LNB)
            o_ref[0, sls[g], :] = ((yn + bonus_s[sls[g], :]) * g_s[sls[g], :]).astype(o_ref.dtype)

    tail = [c1_thunk((n_groups - 1) * GROUP + g) for g in range(GROUP)]
    for i in range(n_groups - 1):
        run_with_fillers(functools.partial(phase_c2, i), tail if i == n_groups - 2 else [])
    while tail:
        tail.pop(0)()
    phase_c2(n_groups - 1, lambda: None)


def _rwkv(p_rkv, p_sm, mu_rkv, mu_sm, w0, a0, k_k, k_a, r_k, ln_w, ln_b,
          w2p, a2p, g2p, bsz, seq):
    rw = w0.shape[-1]
    npair = rw // LANES
    lora_w, lora_a, lora_g = w2p.shape[0], a2p.shape[0], g2p.shape[0]
    wsm = lora_w + lora_a + lora_g
    p3 = p_rkv.reshape(bsz, seq, 3 * rw)
    ps = _lora_in(p_sm, mu_sm, lora_w, lora_a, bsz, seq)
    n_chunks = seq // CHUNK

    def col(off):
        return pl.BlockSpec((1, seq, LANES), lambda b, h: (b, 0, off + h))

    rows = [mu_rkv[0, :rw], mu_rkv[0, rw:2 * rw], mu_rkv[0, 2 * rw:],
            w0, a0, k_k, k_a, r_k.reshape(rw), ln_w, ln_b]
    vecs = jnp.concatenate([jnp.stack(rows), jnp.zeros((VEC_ROWS - len(rows), rw), F32)], axis=0)
    lora = jnp.concatenate([w2p, a2p, g2p], axis=0)

    return pl.pallas_call(
        functools.partial(_rwkv_kernel, seq=seq, lora_w=lora_w, lora_a=lora_a),
        grid=(bsz, npair),
        in_specs=[col(0), col(npair), col(2 * npair),
                  pl.BlockSpec((1, seq, wsm), lambda b, h: (b, 0, 0)),
                  pl.BlockSpec((VEC_ROWS, LANES), lambda b, h: (0, h)),
                  pl.BlockSpec((wsm, LANES), lambda b, h: (0, h))],
        out_specs=pl.BlockSpec((1, seq, LANES), lambda b, h: (b, 0, h)),
        out_shape=jax.ShapeDtypeStruct((bsz, seq, rw), BF16),
        scratch_shapes=(
            [pltpu.VMEM((seq, LANES), BF16) for _ in range(7)]
            + [pltpu.VMEM((seq, LANES), F32) for _ in range(2)]
            + [pltpu.VMEM((n_chunks, 8, LANES), F32),
               pltpu.VMEM((n_chunks, LANES, LANES), BF16),
               pltpu.VMEM((n_chunks, LANES, LANES), F32),
               pltpu.VMEM((seq, LANES), BF16),
               pltpu.VMEM((seq, LANES), F32),
               pltpu.VMEM((n_chunks, LANES, LANES), BF16)]),
        compiler_params=_params("parallel", "parallel"),
        name="rwkv7",
    )(p3, p3, p3, ps, vecs, lora)


def _fgate_kernel(f_ref, bf_ref, col_ref, row_ref, *, seq):
    nblk = seq // LANES
    ri = lax.broadcasted_iota(jnp.int32, (LANES, LANES), 0)
    ci = lax.broadcasted_iota(jnp.int32, (LANES, LANES), 1)
    tri = (ci <= ri).astype(BF16)

    def body(i, carry):
        t0 = pl.multiple_of(i * LANES, LANES)
        z = f_ref[0, pl.ds(t0, LANES), :] + bf_ref[...]
        logf = jnp.minimum(z, 0.0) - jnp.log1p(jnp.exp(-jnp.abs(z)))
        hi = logf.astype(BF16)
        r1 = logf - hi.astype(F32)
        mid = r1.astype(BF16)
        lo = (r1 - mid.astype(F32)).astype(BF16)
        cum = carry + (_dot(tri, hi) + _dot(tri, mid) + _dot(tri, lo))
        col_ref[0, pl.ds(t0, LANES), :] = cum
        row_ref[0, :, pl.ds(t0, LANES)] = cum.T
        return cum[LANES - 1:LANES, :]

    lax.fori_loop(0, nblk, body, jnp.zeros((1, LANES), F32))


def _fgate(p_sm, b_f_pad, off, bsz, seq):
    ps = p_sm.reshape(bsz, seq, p_sm.shape[-1])
    return pl.pallas_call(
        functools.partial(_fgate_kernel, seq=seq),
        grid=(bsz,),
        in_specs=[pl.BlockSpec((1, seq, LANES), lambda b: (b, 0, off // LANES)),
                  pl.BlockSpec((1, LANES), lambda b: (0, 0))],
        out_specs=[pl.BlockSpec((1, seq, LANES), lambda b: (b, 0, 0)),
                   pl.BlockSpec((1, LANES, seq), lambda b: (b, 0, 0))],
        out_shape=[jax.ShapeDtypeStruct((bsz, seq, LANES), F32),
                   jax.ShapeDtypeStruct((bsz, LANES, seq), F32)],
        compiler_params=_params("parallel"),
        name="fox_gates",
    )(ps, b_f_pad)


FOX_HEADS_PER_STEP = 2
LOG2E = 1.4426950408889634


def _fox_kernel(q_ref, k_ref, v_ref, ccol_ref, crow_ref, on_ref, o_ref, *, tq, seq, scale):
    hg = pl.program_id(1)
    heads = range(FOX_HEADS_PER_STEP)
    hsl = [slice(g * FOX_HEAD, (g + 1) * FOX_HEAD) for g in heads]
    lane = lax.broadcasted_iota(jnp.int32, (tq, LANES), 1)
    causal = (lax.broadcasted_iota(jnp.int32, (tq, tq), 1)
              <= lax.broadcasted_iota(jnp.int32, (tq, tq), 0))
    ones_col = jnp.where(lane == 0, 1.0, 0.0).astype(BF16)

    for i in range(seq // tq):
        qrows = pl.ds(i * tq, tq)
        q = [q_ref[0, qrows, sl] for sl in hsl]
        ccol = ccol_ref[0, qrows, :]
        cq = [LOG2E * jnp.sum(jnp.where(lane == hg * FOX_HEADS_PER_STEP + g, ccol, 0.0),
                              axis=-1, keepdims=True) for g in heads]

        def scores(j, q=q, cq=cq, i=i):
            krows = pl.ds(j * tq, tq)
            s = [_dot_nt(q[g], k_ref[0, krows, hsl[g]]) * (scale * LOG2E)
                 + cq[g] - LOG2E * crow_ref[0, g, :, krows] for g in heads]
            if j == i:
                s = [jnp.where(causal, x, -jnp.inf) for x in s]
            return s

        m = [jnp.full((tq, 1), -jnp.inf, F32) for _ in heads]
        acc = [jnp.zeros((tq, 2 * FOX_HEAD), F32) for _ in heads]
        s = scores(0)
        for j in range(i + 1):
            s_next = scores(j + 1) if j < i else None
            vrows = pl.ds(j * tq, tq)
            m_new = [jnp.maximum(m[g], jnp.max(s[g], axis=-1, keepdims=True)) for g in heads]
            alpha = [jnp.exp2(m[g] - m_new[g]) for g in heads]
            p = [jnp.exp2(s[g] - m_new[g]).astype(BF16) for g in heads]
            acc = [alpha[g] * acc[g]
                   + _dot(p[g], jnp.concatenate([v_ref[0, vrows, hsl[g]], ones_col], axis=1))
                   for g in heads]
            m, s = m_new, s_next
        for g in heads:
            o = acc[g][:, :FOX_HEAD] / acc[g][:, FOX_HEAD:FOX_HEAD + 1]
            o = o * lax.rsqrt(jnp.mean(o * o, axis=-1, keepdims=True) + NORM_EPS) * on_ref[g]
            o_ref[0, qrows, hsl[g]] = o.astype(o_ref.dtype)


def _fox(p_fox, ccol, crow, out_norm, bsz, seq):
    fw = p_fox.shape[-1] // 3
    nh = fw // FOX_HEAD
    hps = FOX_HEADS_PER_STEP
    ng = nh // hps
    wid = hps * FOX_HEAD
    tq = _pick(seq, (512, 256, 128))
    p3 = p_fox.reshape(bsz, seq, 3 * fw)
    crow4 = crow.reshape(bsz, LANES, 1, seq)
    on = out_norm.reshape(nh, 1, FOX_HEAD)
    return pl.pallas_call(
        functools.partial(_fox_kernel, tq=tq, seq=seq, scale=FOX_HEAD ** -0.5),
        grid=(bsz, ng),
        in_specs=[pl.BlockSpec((1, seq, wid), lambda b, h: (b, 0, h)),
                  pl.BlockSpec((1, seq, wid), lambda b, h: (b, 0, ng + h)),
                  pl.BlockSpec((1, seq, wid), lambda b, h: (b, 0, 2 * ng + h)),
                  pl.BlockSpec((1, seq, LANES), lambda b, h: (b, 0, 0)),
                  pl.BlockSpec((1, hps, 1, seq), lambda b, h: (b, h, 0, 0)),
                  pl.BlockSpec((hps, 1, FOX_HEAD), lambda b, h: (h, 0, 0))],
        out_specs=pl.BlockSpec((1, seq, wid), lambda b, h: (b, 0, h)),
        out_shape=jax.ShapeDtypeStruct((bsz, seq, fw), BF16),
        compiler_params=_params("parallel", "parallel"),
        name="fox_attention",
    )(p3, p3, p3, ccol, crow4, on)


def _pad_rows(w, rows):
    return jnp.pad(w, ((0, rows - w.shape[0]), (0, 0)))


def _ffn(x2, h, w_gate, w_up, w_down, gate, seq, side=None):
    dff = w_gate.shape[1]
    dffp = _round_up(dff, 512)
    act, wd = _swiglu_up(h, w_gate, w_up, w_down, dffp)
    return _matmul_residual(act, wd, x2, gate, seq, 0.5, dff, side)


def kernel(x, c, w_mod, b_mod, norm_ffn1, ffn1_gate, ffn1_up, ffn1_down, norm_mix, w_in, rwkv_mu, rwkv_w0, rwkv_w2, rwkv_a0, rwkv_a2, rwkv_g2, rwkv_k_k, rwkv_k_a, rwkv_r_k, rwkv_ln_w, rwkv_ln_b, fox_b_f, fox_out_norm, w_out, norm_ffn2, ffn2_gate, ffn2_up, ffn2_down, norm_final):
    bsz, seq, d = x.shape
    depth = w_mod.shape[0]
    rw = rwkv_w0.shape[-1]
    fw = fox_out_norm.shape[-1]
    nfh = fox_b_f.shape[-1]
    dl, al, gl = rwkv_w2.shape[1], rwkv_a2.shape[1], rwkv_g2.shape[1]
    dlp, alp, glp = (_round_up(n, LANES) for n in (dl, al, gl))
    c3 = 3 * rw
    c4, c5, c6 = c3 + dl, c3 + dl + al, c3 + dl + al + gl

    x2 = x.reshape(bsz * seq, d)
    for l in range(depth):
        mod = _mod(c, w_mod[l], b_mod[l])
        sh1, sc1, gt1, sh2, sc2, gt2, sh3, sc3, gt3 = (
            m[:, None, :] for m in jnp.split(mod, 9, axis=-1))

        h = _norm_mod(x2, norm_ffn1[l], sc1, sh1, seq)
        x2, wt = _ffn(x2, h, ffn1_gate[l], ffn1_up[l], ffn1_down[l], gt1, seq,
                      side=jnp.swapaxes(w_in[l], 0, 1))

        h = _norm_mod(x2, norm_mix[l], sc2, sh2, seq)
        mu = rwkv_mu[l]
        wt_sm = jnp.concatenate(
            [_pad_rows(wt[c3:c4], dlp), _pad_rows(wt[c4:c5], alp),
             _pad_rows(wt[c5:c6], glp), _pad_rows(wt[c6 + 3 * fw:], LANES)], axis=0)
        mu_rkv = mu[:c3].reshape(1, c3)
        mu_sm = jnp.concatenate(
            [jnp.pad(mu[c3:c4], (0, dlp - dl)), jnp.pad(mu[c4:c5], (0, alp - al)),
             jnp.pad(mu[c5:c6], (0, glp - gl))]).reshape(1, dlp + alp + glp)
        p_rkv = _matmul_nt(h, wt, F32, 0, c3)
        p_sm = _matmul_nt(h, wt_sm, F32)
        p_fox = _matmul_nt(h, wt, BF16, c6, 3 * fw)

        y_r = _rwkv(p_rkv, p_sm, mu_rkv, mu_sm,
                    rwkv_w0[l], rwkv_a0[l], rwkv_k_k[l], rwkv_k_a[l], rwkv_r_k[l],
                    rwkv_ln_w[l], rwkv_ln_b[l],
                    _pad_rows(rwkv_w2[l], dlp).astype(BF16),
                    _pad_rows(rwkv_a2[l], alp).astype(BF16),
                    _pad_rows(rwkv_g2[l], glp).astype(BF16), bsz, seq)

        b_f_pad = jnp.pad(fox_b_f[l], (0, LANES - nfh)).reshape(1, LANES)
        ccol, crow = _fgate(p_sm, b_f_pad, dlp + alp + glp, bsz, seq)
        y_f = _fox(p_fox, ccol, crow, fox_out_norm[l], bsz, seq)

        x2 = _matmul_residual2(y_r.reshape(bsz * seq, rw), y_f.reshape(bsz * seq, fw),
                               w_out[l].astype(BF16), x2, gt2, seq, 1.0)

        h = _norm_mod(x2, norm_ffn2[l], sc3, sh3, seq)
        x2 = _ffn(x2, h, ffn2_gate[l], ffn2_up[l], ffn2_down[l], gt3, seq)
    return _final_norm(x2, norm_final).reshape(bsz, seq, d)
```

```python
import functools

import jax
import jax.numpy as jnp
from jax import lax
from jax.experimental import pallas as pl
from jax.experimental.pallas import tpu as pltpu

F32 = jnp.float32
BF16 = jnp.bfloat16

LANES = 128
RWKV_HEAD = 64
FOX_HEAD = 128
CHUNK = 64
INV_BASE = 4
GROUP = 16
NORM_EPS = 1e-6
LN_X_EPS = 64e-5
VMEM_LIMIT = 56 * 1024 * 1024


def _round_up(n, m):
    return (n + m - 1) // m * m


def _pick(n, candidates):
    for c in candidates:
        if n % c == 0:
            return c
    return n


def _dot(a, b):
    return jnp.dot(a, b, preferred_element_type=F32)


def _dot_nt(a, b):
    return lax.dot_general(a, b, (((1,), (1,)), ((), ())), preferred_element_type=F32)


def _dot_tn(a, b):
    return lax.dot_general(a, b, (((0,), (0,)), ((), ())), preferred_element_type=F32)


def _params(*sem):
    return pltpu.CompilerParams(dimension_semantics=sem, vmem_limit_bytes=VMEM_LIMIT)


def _mod_kernel(c_ref, w_ref, b_ref, o_ref):
    c = c_ref[...]
    a = (c * jax.nn.sigmoid(c)).astype(BF16)
    o_ref[...] = _dot(a, w_ref[...].astype(BF16)) + b_ref[...]


def _mod(c, w_mod, b_mod):
    bsz, d = c.shape
    n = w_mod.shape[1]
    rows = _round_up(bsz, 16)
    c_pad = jnp.pad(c, ((0, rows - bsz), (0, 0)))
    tn = _pick(n, (1024, 512, 256, 128))
    out = pl.pallas_call(
        _mod_kernel,
        grid=(n // tn,),
        in_specs=[pl.BlockSpec((rows, d), lambda j: (0, 0)),
                  pl.BlockSpec((d, tn), lambda j: (0, j)),
                  pl.BlockSpec((1, tn), lambda j: (0, j))],
        out_specs=pl.BlockSpec((rows, tn), lambda j: (0, j)),
        out_shape=jax.ShapeDtypeStruct((rows, n), F32),
        compiler_params=_params("parallel"),
        name="mod_matmul",
    )(c_pad, w_mod, b_mod.reshape(1, n))
    return out[:bsz]


def _norm_kernel(x_ref, g_ref, sc_ref, sh_ref, o_ref):
    x = x_ref[...]
    y = x * lax.rsqrt(jnp.mean(x * x, axis=-1, keepdims=True) + NORM_EPS) * g_ref[...]
    o_ref[...] = (y * (1.0 + sc_ref[0]) + sh_ref[0]).astype(o_ref.dtype)


def _final_norm_kernel(x_ref, g_ref, o_ref):
    x = x_ref[...]
    o_ref[...] = x * lax.rsqrt(jnp.mean(x * x, axis=-1, keepdims=True) + NORM_EPS) * g_ref[...]


def _norm_mod(x2, g, sc, sh, seq):
    t, d = x2.shape
    ts = _pick(seq, (512, 256, 128, 64, 8))
    return pl.pallas_call(
        _norm_kernel,
        grid=(t // ts,),
        in_specs=[pl.BlockSpec((ts, d), lambda i: (i, 0)),
                  pl.BlockSpec((1, d), lambda i: (0, 0)),
                  pl.BlockSpec((1, 1, d), lambda i: (i * ts // seq, 0, 0)),
                  pl.BlockSpec((1, 1, d), lambda i: (i * ts // seq, 0, 0))],
        out_specs=pl.BlockSpec((ts, d), lambda i: (i, 0)),
        out_shape=jax.ShapeDtypeStruct((t, d), BF16),
        compiler_params=_params("parallel"),
        name="norm_mod",
    )(x2, g.reshape(1, d), sc, sh)


def _final_norm(x2, g):
    t, d = x2.shape
    ts = _pick(t, (512, 256, 128, 64, 8))
    return pl.pallas_call(
        _final_norm_kernel,
        grid=(t // ts,),
        in_specs=[pl.BlockSpec((ts, d), lambda i: (i, 0)),
                  pl.BlockSpec((1, d), lambda i: (0, 0))],
        out_specs=pl.BlockSpec((ts, d), lambda i: (i, 0)),
        out_shape=jax.ShapeDtypeStruct((t, d), F32),
        compiler_params=_params("parallel"),
        name="final_norm",
    )(x2, g.reshape(1, d))


def _mm_nt_kernel(x_ref, wt_ref, o_ref):
    o_ref[...] = _dot_nt(x_ref[...], wt_ref[...]).astype(o_ref.dtype)


def _matmul_nt(x, wt, out_dtype, row0=0, n=None):
    m, kd = x.shape
    n = wt.shape[0] if n is None else n
    tm = _pick(m, (1024, 512, 256, 128))
    tn = _pick(n, (1024, 512, 640, 256, 128))
    return pl.pallas_call(
        _mm_nt_kernel,
        grid=(m // tm, n // tn),
        in_specs=[pl.BlockSpec((tm, kd), lambda i, j: (i, 0)),
                  pl.BlockSpec((pl.Element(tn), pl.Element(kd)),
                               lambda i, j: ((row0 // 16 + j * (tn // 16)) * 16, 0))],
        out_specs=pl.BlockSpec((tm, tn), lambda i, j: (i, j)),
        out_shape=jax.ShapeDtypeStruct((m, n), out_dtype),
        compiler_params=_params("parallel", "parallel"),
        name="matmul_nt",
    )(x, wt)


def _swiglu_kernel(x_ref, wg_hbm, wu_hbm, wd_ref, o_ref, wdb_ref, stage_ref, wgb_ref, wub_ref,
                   sem, *, n_full, rem, tn, wd_blocks):
    j = pl.program_id(0)
    step = j * pl.num_programs(1) + pl.program_id(1)

    def full_copies(jj):
        cols = pl.ds(pl.multiple_of(jj * tn, tn), tn)
        return (pltpu.make_async_copy(wg_hbm.at[:, cols], stage_ref.at[0], sem.at[0]),
                pltpu.make_async_copy(wu_hbm.at[:, cols], stage_ref.at[1], sem.at[1]))

    def tail_copies():
        cols = pl.ds(n_full * tn, rem)
        return (pltpu.make_async_copy(wg_hbm.at[:, cols], stage_ref.at[0, :, :rem], sem.at[0]),
                pltpu.make_async_copy(wu_hbm.at[:, cols], stage_ref.at[1, :, :rem], sem.at[1]))

    def start(jj):
        @pl.when(jj < n_full)
        def _():
            for c in full_copies(jj):
                c.start()

        if rem:
            @pl.when(jj == n_full)
            def _():
                for c in tail_copies():
                    c.start()

    @pl.when(pl.program_id(1) == 0)
    def _():
        @pl.when(j == 0)
        def _():
            start(j)

        @pl.when(j < n_full)
        def _():
            for c in full_copies(j):
                c.wait()
            wgb_ref[...] = stage_ref[0].astype(BF16)
            wub_ref[...] = stage_ref[1].astype(BF16)

        if rem:
            @pl.when(j == n_full)
            def _():
                for c in tail_copies():
                    c.wait()
                wgb_ref[:, :rem] = stage_ref[0, :, :rem].astype(BF16)
                wub_ref[:, :rem] = stage_ref[1, :, :rem].astype(BF16)

        start(j + 1)

    def compute(ncols):
        wdb_ref[...] = jnp.where(step < wd_blocks, wd_ref[...], 0.0).astype(wdb_ref.dtype)
        x = x_ref[...]
        g = _dot(x, wgb_ref[:, :ncols])
        u = _dot(x, wub_ref[:, :ncols])
        o_ref[:, :ncols] = (g * (0.5 * jnp.tanh(0.5 * g) + 0.5) * u).astype(o_ref.dtype)
        if ncols < tn:
            o_ref[:, ncols:] = jnp.zeros((o_ref.shape[0], tn - ncols), o_ref.dtype)

    if rem:
        pl.when(j < n_full)(lambda: compute(tn))
        pl.when(j == n_full)(lambda: compute(rem))
    else:
        compute(tn)


def _swiglu_up(h, wg, wu, w_down, n_out):
    m, kd = h.shape
    n = wg.shape[1]
    d_out = w_down.shape[1]
    tm = _pick(m, (1024, 512, 256, 128))
    tn = 512
    n_full, rem = divmod(n, tn)
    assert rem % LANES == 0 and n_out == _round_up(n, tn)
    nj, ni = n_out // tn, m // tm
    slab, left = divmod(n_out, nj * ni)
    assert left == 0 and slab % 16 == 0 and n % slab == 0
    wd_blocks = n // slab
    return pl.pallas_call(
        functools.partial(_swiglu_kernel, n_full=n_full, rem=rem, tn=tn, wd_blocks=wd_blocks),
        grid=(nj, ni),
        in_specs=[pl.BlockSpec((tm, kd), lambda j, i: (i, 0)),
                  pl.BlockSpec(memory_space=pl.ANY),
                  pl.BlockSpec(memory_space=pl.ANY),
                  pl.BlockSpec((slab, d_out),
                               lambda j, i: (jnp.minimum(j * ni + i, wd_blocks - 1), 0))],
        out_specs=[pl.BlockSpec((tm, tn), lambda j, i: (i, j)),
                   pl.BlockSpec((slab, d_out), lambda j, i: (j * ni + i, 0))],
        out_shape=[jax.ShapeDtypeStruct((m, n_out), BF16),
                   jax.ShapeDtypeStruct((n_out, d_out), BF16)],
        scratch_shapes=[pltpu.VMEM((2, kd, tn), F32),
                        pltpu.VMEM((kd, tn), BF16), pltpu.VMEM((kd, tn), BF16),
                        pltpu.SemaphoreType.DMA((2,))],
        compiler_params=_params("arbitrary", "arbitrary"),
        name="swiglu_up",
    )(h, wg, wu, w_down)


def _resid_kernel(a_ref, w_ref, x_ref, gt_ref, *rest, nk, k_last, scale):
    side_in, o_ref, side_out, acc_ref = rest if len(rest) == 4 else (None, rest[0], None, rest[1])
    k = pl.program_id(2)

    def side_cast():
        if side_in is not None:
            side_out[...] = side_in[...].astype(side_out.dtype)

    @pl.when(k == 0)
    def _():
        side_cast()
        acc_ref[...] = _dot(a_ref[...], w_ref[...])

    @pl.when((k > 0) & (k < nk - 1))
    def _():
        side_cast()
        acc_ref[...] += _dot(a_ref[...], w_ref[...])

    @pl.when(k == nk - 1)
    def _():
        side_cast()
        y = acc_ref[...] + _dot(a_ref[:, :k_last], w_ref[:k_last, :])
        o_ref[...] = x_ref[...] + (scale * gt_ref[0]) * y


def _matmul_residual(a, w, x2, gate, seq, scale, k_valid, side=None):
    m, kd = a.shape
    n = w.shape[1]
    tm = _pick(seq, (1024, 512, 256, 128))
    tn = _pick(n, (1024, 512, 256, 128))
    tk = _pick(kd, (2816, 1024, 512, 256, 128))
    nk = kd // tk
    k_last = k_valid - (nk - 1) * tk
    assert nk >= 2 and 0 < k_last <= tk and k_last % LANES == 0
    ni, nj = m // tm, n // tn
    in_specs = [pl.BlockSpec((tm, tk), lambda i, j, k: (i, k)),
                pl.BlockSpec((tk, tn), lambda i, j, k: (k, j)),
                pl.BlockSpec((tm, tn), lambda i, j, k: (i, j)),
                pl.BlockSpec((1, 1, tn), lambda i, j, k: (i * tm // seq, 0, j))]
    out_specs = [pl.BlockSpec((tm, tn), lambda i, j, k: (i, j))]
    out_shape = [jax.ShapeDtypeStruct((m, n), F32)]
    operands = [a, w, x2, gate]
    semantics = ("parallel", "parallel", "arbitrary")
    if side is not None:
        rows, cols = side.shape
        slab = _round_up(-(-rows // (ni * nj * nk)), 16)
        last = -(-rows // slab) - 1

        def side_map(i, j, k):
            return (jnp.minimum((i * nj + j) * nk + k, last), 0)

        in_specs.append(pl.BlockSpec((slab, cols), side_map))
        out_specs.append(pl.BlockSpec((slab, cols), side_map))
        out_shape.append(jax.ShapeDtypeStruct((rows, cols), BF16))
        operands.append(side)
        semantics = ("arbitrary", "arbitrary", "arbitrary")
    res = pl.pallas_call(
        functools.partial(_resid_kernel, nk=nk, k_last=k_last, scale=scale),
        grid=(ni, nj, nk),
        in_specs=in_specs,
        out_specs=out_specs,
        out_shape=out_shape,
        scratch_shapes=[pltpu.VMEM((tm, tn), F32)],
        compiler_params=_params(*semantics),
        name="matmul_residual",
    )(*operands)
    return res[0] if side is None else res


def _resid2_kernel(a1_ref, a2_ref, w_ref, x_ref, gt_ref, o_ref, *, scale):
    k1 = a1_ref.shape[1]
    y = _dot(a1_ref[...], w_ref[:k1, :]) + _dot(a2_ref[...], w_ref[k1:, :])
    o_ref[...] = x_ref[...] + (scale * gt_ref[0]) * y


def _matmul_residual2(a1, a2, w, x2, gate, seq, scale):
    m, k1 = a1.shape
    k2 = a2.shape[1]
    n = w.shape[1]
    assert w.shape[0] == k1 + k2
    tm = _pick(seq, (1024, 512, 256, 128))
    tn = _pick(n, (512, 256, 128))
    return pl.pallas_call(
        functools.partial(_resid2_kernel, scale=scale),
        grid=(m // tm, n // tn),
        in_specs=[pl.BlockSpec((tm, k1), lambda i, j: (i, 0)),
                  pl.BlockSpec((tm, k2), lambda i, j: (i, 0)),
                  pl.BlockSpec((k1 + k2, tn), lambda i, j: (0, j)),
                  pl.BlockSpec((tm, tn), lambda i, j: (i, j)),
                  pl.BlockSpec((1, 1, tn), lambda i, j: (i * tm // seq, 0, j))],
        out_specs=pl.BlockSpec((tm, tn), lambda i, j: (i, j)),
        out_shape=jax.ShapeDtypeStruct((m, n), F32),
        compiler_params=_params("parallel", "parallel"),
        name="out_proj_residual",
    )(a1, a2, w, x2, gate)


ROW_TILE = 512


def _lane_head0(shape):
    return lax.broadcasted_iota(jnp.int32, shape, 1) < RWKV_HEAD


def _stack(x):
    m0 = _lane_head0(x.shape)
    zero = jnp.zeros_like(x)
    return jnp.concatenate([jnp.where(m0, x, zero), jnp.where(m0, zero, x)], axis=0)


def _split_dot(lhs_bf16, x):
    hi = x.astype(BF16)
    lo = (x - hi.astype(F32)).astype(BF16)
    return _dot(lhs_bf16, hi) + _dot(lhs_bf16, lo)


def _lora_in_kernel(ps_ref, prev_ref, mu_ref, o_ref, *, lora_w, lora_a):
    x = ps_ref[0]
    prev_row = jnp.where(pl.program_id(1) == 0, 0.0, prev_ref[0][7:8, :])
    row = lax.broadcasted_iota(jnp.int32, x.shape, 0)
    prev = jnp.where(row == 0, prev_row, pltpu.roll(x, 1, 0))
    sm = x + (prev - x) * mu_ref[...]
    o_ref[0, :, :lora_w] = jnp.tanh(sm[:, :lora_w]).astype(o_ref.dtype)
    o_ref[0, :, lora_w:lora_w + lora_a] = sm[:, lora_w:lora_w + lora_a].astype(o_ref.dtype)
    o_ref[0, :, lora_w + lora_a:] = jax.nn.sigmoid(sm[:, lora_w + lora_a:]).astype(o_ref.dtype)


def _lora_in(p_sm, mu_sm, lora_w, lora_a, bsz, seq):
    wsm = mu_sm.shape[-1]
    ps = p_sm.reshape(bsz, seq, p_sm.shape[-1])
    ts = _pick(seq, (ROW_TILE, 128, 64, 8))
    return pl.pallas_call(
        functools.partial(_lora_in_kernel, lora_w=lora_w, lora_a=lora_a),
        grid=(bsz, seq // ts),
        in_specs=[pl.BlockSpec((1, ts, wsm), lambda b, i: (b, i, 0)),
                  pl.BlockSpec((1, 8, wsm), lambda b, i: (b, jnp.maximum(i * (ts // 8) - 1, 0), 0)),
                  pl.BlockSpec((1, wsm), lambda b, i: (0, 0))],
        out_specs=pl.BlockSpec((1, ts, wsm), lambda b, i: (b, i, 0)),
        out_shape=jax.ShapeDtypeStruct((bsz, seq, wsm), BF16),
        compiler_params=_params("parallel", "parallel"),
        name="rwkv_lora_in",
    )(ps, ps, mu_sm)


(VEC_MU_R, VEC_MU_K, VEC_MU_V, VEC_W0, VEC_A0, VEC_KK, VEC_KA, VEC_RK, VEC_LNW,
 VEC_LNB) = range(10)
VEC_ROWS = 16


def _rwkv_kernel(pr_ref, pk_ref, pv_ref, ps_ref, vec_ref, lora_ref, o_ref,
                 at_s, rt_s, kt_s, bt_s, kg_s, bg_s, v_s, g_s, bonus_s, gc_s,
                 m_s, n_s, q_s, yb_s, sb_s, *, seq, lora_w, lora_a):
    n_chunks = seq // CHUNK
    cpt = ROW_TILE // CHUNK

    def vec(idx):
        return vec_ref[idx:idx + 1, :]

    ri = lax.broadcasted_iota(jnp.int32, (LANES, LANES), 0)
    ci = lax.broadcasted_iota(jnp.int32, (LANES, LANES), 1)
    bd_mask = (ri // RWKV_HEAD) == (ci // RWKV_HEAD)
    bd_ones = bd_mask.astype(BF16)
    tri = ((ci <= ri) & ((ri // CHUNK) == (ci // CHUNK))).astype(BF16)

    def head_sum(x):
        return _dot(x.astype(BF16), bd_ones)

    def shift_mix(ref, t0, mu):
        x = ref[0, pl.ds(t0, ROW_TILE), :]
        if t0 == 0:
            prev_row = jnp.zeros((1, x.shape[1]), F32)
        else:
            prev_row = ref[0, pl.ds(t0 - 8, 8), :][7:8, :]
        row = lax.broadcasted_iota(jnp.int32, x.shape, 0)
        prev = jnp.where(row == 0, prev_row, pltpu.roll(x, 1, 0))
        return x + (prev - x) * mu

    def phase_a_stages(i):
        t0 = i * ROW_TILE
        rows = pl.ds(t0, ROW_TILE)
        st = {}

        def load_and_lora():
            st["r"] = shift_mix(pr_ref, t0, vec(VEC_MU_R))
            st["k"] = shift_mix(pk_ref, t0, vec(VEC_MU_K))
            st["v"] = shift_mix(pv_ref, t0, vec(VEC_MU_V))
            st["wlin"] = vec(VEC_W0) + _dot(ps_ref[0, rows, :lora_w], lora_ref[:lora_w, :])
            st["alin"] = vec(VEC_A0) + _dot(ps_ref[0, rows, lora_w:lora_w + lora_a],
                                            lora_ref[lora_w:lora_w + lora_a, :])
            st["g"] = _dot(ps_ref[0, rows, lora_w + lora_a:], lora_ref[lora_w + lora_a:, :])

        def decay_and_norm():
            wlin = st.pop("wlin")
            wlog = -(jnp.maximum(-wlin, 0.0) + jnp.log(1.0 + jnp.exp(-jnp.abs(wlin)))) - 0.5
            st["ld"] = -jnp.exp(wlog)
            st["alr"] = jax.nn.sigmoid(st.pop("alin"))
            st["kk"] = st["k"] * vec(VEC_KK)
            st["kk_ss"] = head_sum(st["kk"] * st["kk"])
            st["cum"] = jnp.concatenate(
                [_split_dot(tri, st["ld"][r:r + LANES]) for r in range(0, ROW_TILE, LANES)], axis=0)

        def keys_and_bonus():
            st["kk"] = st["kk"] * lax.rsqrt(jnp.maximum(st.pop("kk_ss"), 1e-24))
            st["k"] = st["k"] * (1.0 + (st["alr"] - 1.0) * vec(VEC_KA))
            st["bonus_sum"] = head_sum(st["r"] * st["k"] * vec(VEC_RK))
            st["b"] = st["kk"] * st["alr"]

        def scale_and_store():
            cum, ld = st["cum"], st.pop("ld")
            g_dec = jnp.exp(cum)
            g_inv = jnp.exp(-cum)
            at_s[rows, :] = (-st.pop("kk") * jnp.exp(cum - ld)).astype(BF16)
            rt_s[rows, :] = (st.pop("r") * g_dec).astype(BF16)
            kt_s[rows, :] = (st["k"] * g_inv).astype(BF16)
            bt_s[rows, :] = (st["b"] * g_inv).astype(BF16)
            v_s[rows, :] = st["v"].astype(BF16)
            g_s[rows, :] = st.pop("g")
            bonus_s[rows, :] = st.pop("bonus_sum") * st.pop("v")

        def chunk_tails():
            cum, k, b = st.pop("cum"), st.pop("k"), st.pop("b")
            for c in range(cpt):
                lo, hi = c * CHUNK, (c + 1) * CHUNK
                cum_c = cum[hi - 1:hi, :]
                rem = jnp.exp(cum_c - cum[lo:hi, :])
                kg_s[pl.ds(t0 + lo, CHUNK), :] = (k[lo:hi, :] * rem).astype(BF16)
                bg_s[pl.ds(t0 + lo, CHUNK), :] = (b[lo:hi, :] * rem).astype(BF16)
                gc_s[i * cpt + c] = jnp.broadcast_to(jnp.exp(cum_c), (8, LANES))

        return [load_and_lora, decay_and_norm, keys_and_bonus, scale_and_store, chunk_tails]

    row = lax.broadcasted_iota(jnp.int32, (CHUNK, LANES), 0)
    col = lax.broadcasted_iota(jnp.int32, (CHUNK, LANES), 1) % RWKV_HEAD
    strict = col < row
    incl = col <= row
    eye_p = (col == row).astype(F32)

    def same_block(size):
        return (row // size) == (col // size)

    def each(fn, *lists):
        return [fn(*xs) for xs in zip(*lists)]

    def phase_b(i, tick):
        cs = [i * GROUP + g for g in range(GROUP)]
        sls = [pl.ds(c * CHUNK, CHUNK) for c in cs]
        at = [at_s[sl, :] for sl in sls]
        rt = [rt_s[sl, :] for sl in sls]
        kt = [kt_s[sl, :] for sl in sls]
        bt = [bt_s[sl, :] for sl in sls]
        v = [v_s[sl, :] for sl in sls]
        sc = each(lambda a, r, k, b: _dot_nt(jnp.concatenate([a, r], axis=0),
                                             jnp.concatenate([_stack(k), _stack(b)], axis=0)),
                  at, rt, kt, bt)
        a_ak = [jnp.where(strict, x[:CHUNK, :LANES], 0.0).astype(BF16) for x in sc]
        a_ab = [jnp.where(strict, x[:CHUNK, LANES:], 0.0) for x in sc]
        a_rk = [jnp.where(incl, x[CHUNK:, :LANES], 0.0).astype(BF16) for x in sc]
        a_rb = [jnp.where(incl, x[CHUNK:, LANES:], 0.0).astype(BF16) for x in sc]
        tick()
        d = [jnp.where(same_block(INV_BASE), x, 0.0) for x in a_ab]
        db = [x.astype(BF16) for x in d]
        d2 = [_dot(x, _stack(x)).astype(BF16) for x in db]
        tick()
        p = [eye_p + x for x in d]
        p = each(lambda pp, x2: pp + _dot(pp.astype(BF16), _stack(x2)), p, d2)
        tick()
        s = INV_BASE
        while s < CHUNK:
            lower_left = same_block(2 * s) & ((row // s) % 2 == 1) & ((col // s) % 2 == 0)
            a21 = [jnp.where(lower_left, x, 0.0).astype(BF16) for x in a_ab]
            pb = [x.astype(BF16) for x in p]
            a21_t11 = each(lambda a, t: _dot(a, _stack(t)).astype(BF16), a21, pb)
            tick()
            p = each(lambda pp, t, y: pp + _dot(t, _stack(y)), p, pb, a21_t11)
            tick()
            s *= 2
        tinv = [x.astype(BF16) for x in p]
        vst = [_stack(x) for x in v]
        x1 = each(lambda a, vs: _dot(a, vs).astype(BF16), a_ak, vst)
        tick()
        wu = each(lambda t, a, x: _dot(t, jnp.concatenate([_stack(a), _stack(x)], axis=1)),
                  tinv, at, x1)
        tick()
        wb = [x[:, :LANES].astype(BF16) for x in wu]
        ub = [x[:, LANES:].astype(BF16) for x in wu]
        qy = each(lambda a, w, u: _dot(a, jnp.concatenate([_stack(w), _stack(u)], axis=1)),
                  a_rb, wb, ub)
        tick()
        yb = each(lambda a, vs, x: _dot(a, vs) + x[:, LANES:], a_rk, vst, qy)
        tick()
        for g in range(GROUP):
            q_s[sls[g], :] = (rt[g].astype(F32) + qy[g][:, :LANES]).astype(BF16)
            yb_s[sls[g], :] = yb[g]
        kg = [kg_s[sl, :] for sl in sls]
        bg = [bg_s[sl, :] for sl in sls]
        mm = each(_dot_tn, wb, bg)
        tick()
        nn = each(lambda vv, u, k, b: _dot_tn(jnp.concatenate([vv, u], axis=0),
                                              jnp.concatenate([k, b], axis=0)), v, ub, kg, bg)
        tick()
        for g in range(GROUP):
            m_s[cs[g]] = jnp.where(bd_mask, mm[g], 0.0).astype(BF16)
            n_s[cs[g]] = jnp.where(bd_mask, nn[g], 0.0)

    def c1_step(c, s):
        sb = s.astype(BF16)
        sb_s[c] = sb
        return s * gc_s[c][0:1, :] + _dot(sb, m_s[c]) + n_s[c]

    n_groups = n_chunks // GROUP
    tiles_per_group = GROUP * CHUNK // ROW_TILE
    state = [jnp.zeros((LANES, LANES), F32)]

    def c1_thunk(c):
        def run():
            state[0] = c1_step(c, state[0])
        return run

    def prep_stages(group):
        per_tile = [phase_a_stages(group * tiles_per_group + t) for t in range(tiles_per_group)]
        return [stage for stages in zip(*per_tile) for stage in stages]

    def interleave(xs, ys):
        out = []
        for j in range(max(len(xs), len(ys))):
            out += xs[j:j + 1] + ys[j:j + 1]
        return out

    def run_with_fillers(body, fillers):
        def tick():
            if fillers:
                fillers.pop(0)()
        body(tick)
        while fillers:
            fillers.pop(0)()

    for stage in prep_stages(0):
        stage()
    for i in range(n_groups):
        fillers = interleave(
            [c1_thunk((i - 1) * GROUP + g) for g in range(GROUP)] if i > 0 else [],
            prep_stages(i + 1) if i + 1 < n_groups else [])
        run_with_fillers(functools.partial(phase_b, i), fillers)

    inv_n = 1.0 / RWKV_HEAD

    def phase_c2(i, tick):
        cs = [i * GROUP + g for g in range(GROUP)]
        sls = [pl.ds(c * CHUNK, CHUNK) for c in cs]

        def staged(fn, *lists):
            out = []
            for g, xs in enumerate(zip(*lists)):
                out.append(fn(*xs))
                if g % 4 == 3:
                    tick()
            return out

        y = staged(lambda sl, c: _dot_nt(q_s[sl, :], sb_s[c]) + yb_s[sl, :], sls, cs)
        mean = staged(lambda x: head_sum(x) * inv_n, y)
        yc = each(lambda a, b: a - b, y, mean)
        var = staged(lambda x: head_sum(x * x) * inv_n, yc)
        for g in range(GROUP):
            yn = yc[g] * lax.rsqrt(var[g] + LN_X_EPS) * vec(VEC_LNW) + vec(VEC_LNB)
            o_ref[0, sls[g], :] = ((yn + bonus_s[sls[g], :]) * g_s[sls[g], :]).astype(o_ref.dtype)

    tail = [c1_thunk((n_groups - 1) * GROUP + g) for g in range(GROUP)]
    for i in range(n_groups - 1):
        run_with_fillers(functools.partial(phase_c2, i), tail if i == n_groups - 2 else [])
    while tail:
        tail.pop(0)()
    phase_c2(n_groups - 1, lambda: None)


def _rwkv(p_rkv, p_sm, mu_rkv, mu_sm, w0, a0, k_k, k_a, r_k, ln_w, ln_b,
          w2p, a2p, g2p, bsz, seq):
    rw = w0.shape[-1]
    npair = rw // LANES
    lora_w, lora_a, lora_g = w2p.shape[0], a2p.shape[0], g2p.shape[0]
    wsm = lora_w + lora_a + lora_g
    p3 = p_rkv.reshape(bsz, seq, 3 * rw)
    ps = _lora_in(p_sm, mu_sm, lora_w, lora_a, bsz, seq)
    n_chunks = seq // CHUNK

    def col(off):
        return pl.BlockSpec((1, seq, LANES), lambda b, h: (b, 0, off + h))

    rows = [mu_rkv[0, :rw], mu_rkv[0, rw:2 * rw], mu_rkv[0, 2 * rw:],
            w0, a0, k_k, k_a, r_k.reshape(rw), ln_w, ln_b]
    vecs = jnp.concatenate([jnp.stack(rows), jnp.zeros((VEC_ROWS - len(rows), rw), F32)], axis=0)
    lora = jnp.concatenate([w2p, a2p, g2p], axis=0)

    return pl.pallas_call(
        functools.partial(_rwkv_kernel, seq=seq, lora_w=lora_w, lora_a=lora_a),
        grid=(bsz, npair),
        in_specs=[col(0), col(npair), col(2 * npair),
                  pl.BlockSpec((1, seq, wsm), lambda b, h: (b, 0, 0)),
                  pl.BlockSpec((VEC_ROWS, LANES), lambda b, h: (0, h)),
                  pl.BlockSpec((wsm, LANES), lambda b, h: (0, h))],
        out_specs=pl.BlockSpec((1, seq, LANES), lambda b, h: (b, 0, h)),
        out_shape=jax.ShapeDtypeStruct((bsz, seq, rw), BF16),
        scratch_shapes=(
            [pltpu.VMEM((seq, LANES), BF16) for _ in range(7)]
            + [pltpu.VMEM((seq, LANES), F32) for _ in range(2)]
            + [pltpu.VMEM((n_chunks, 8, LANES), F32),
               pltpu.VMEM((n_chunks, LANES, LANES), BF16),
               pltpu.VMEM((n_chunks, LANES, LANES), F32),
               pltpu.VMEM((seq, LANES), BF16),
               pltpu.VMEM((seq, LANES), F32),
               pltpu.VMEM((n_chunks, LANES, LANES), BF16)]),
        compiler_params=_params("parallel", "parallel"),
        name="rwkv7",
    )(p3, p3, p3, ps, vecs, lora)


def _fgate_kernel(f_ref, bf_ref, col_ref, row_ref, *, seq):
    nblk = seq // LANES
    ri = lax.broadcasted_iota(jnp.int32, (LANES, LANES), 0)
    ci = lax.broadcasted_iota(jnp.int32, (LANES, LANES), 1)
    tri = (ci <= ri).astype(BF16)

    def body(i, carry):
        t0 = pl.multiple_of(i * LANES, LANES)
        z = f_ref[0, pl.ds(t0, LANES), :] + bf_ref[...]
        logf = jnp.minimum(z, 0.0) - jnp.log1p(jnp.exp(-jnp.abs(z)))
        hi = logf.astype(BF16)
        r1 = logf - hi.astype(F32)
        mid = r1.astype(BF16)
        lo = (r1 - mid.astype(F32)).astype(BF16)
        cum = carry + (_dot(tri, hi) + _dot(tri, mid) + _dot(tri, lo))
        col_ref[0, pl.ds(t0, LANES), :] = cum
        row_ref[0, :, pl.ds(t0, LANES)] = cum.T
        return cum[LANES - 1:LANES, :]

    lax.fori_loop(0, nblk, body, jnp.zeros((1, LANES), F32))


def _fgate(p_sm, b_f_pad, off, bsz, seq):
    ps = p_sm.reshape(bsz, seq, p_sm.shape[-1])
    return pl.pallas_call(
        functools.partial(_fgate_kernel, seq=seq),
        grid=(bsz,),
        in_specs=[pl.BlockSpec((1, seq, LANES), lambda b: (b, 0, off // LANES)),
                  pl.BlockSpec((1, LANES), lambda b: (0, 0))],
        out_specs=[pl.BlockSpec((1, seq, LANES), lambda b: (b, 0, 0)),
                   pl.BlockSpec((1, LANES, seq), lambda b: (b, 0, 0))],
        out_shape=[jax.ShapeDtypeStruct((bsz, seq, LANES), F32),
                   jax.ShapeDtypeStruct((bsz, LANES, seq), F32)],
        compiler_params=_params("parallel"),
        name="fox_gates",
    )(ps, b_f_pad)


FOX_HEADS_PER_STEP = 4
LOG2E = 1.4426950408889634


def _fox_kernel(q_ref, k_ref, v_ref, ccol_ref, crow_ref, on_ref, o_ref, *, tq, seq, scale):
    hg = pl.program_id(1)
    heads = range(FOX_HEADS_PER_STEP)
    hsl = [slice(g * FOX_HEAD, (g + 1) * FOX_HEAD) for g in heads]
    lane = lax.broadcasted_iota(jnp.int32, (tq, LANES), 1)
    causal = (lax.broadcasted_iota(jnp.int32, (tq, tq), 1)
              <= lax.broadcasted_iota(jnp.int32, (tq, tq), 0))
    ones_col = jnp.where(lane == 0, 1.0, 0.0).astype(BF16)

    for i in range(seq // tq):
        qrows = pl.ds(i * tq, tq)
        q = [q_ref[0, qrows, sl] for sl in hsl]
        ccol = ccol_ref[0, qrows, :]
        cq = [LOG2E * jnp.sum(jnp.where(lane == hg * FOX_HEADS_PER_STEP + g, ccol, 0.0),
                              axis=-1, keepdims=True) for g in heads]

        def scores(j, q=q, cq=cq, i=i):
            krows = pl.ds(j * tq, tq)
            s = [_dot_nt(q[g], k_ref[0, krows, hsl[g]]) * (scale * LOG2E)
                 + cq[g] - LOG2E * crow_ref[0, g, :, krows] for g in heads]
            if j == i:
                s = [jnp.where(causal, x, -jnp.inf) for x in s]
            return s

        m = [jnp.full((tq, 1), -jnp.inf, F32) for _ in heads]
        acc = [jnp.zeros((tq, 2 * FOX_HEAD), F32) for _ in heads]
        s = scores(0)
        for j in range(i + 1):
            s_next = scores(j + 1) if j < i else None
            vrows = pl.ds(j * tq, tq)
            m_new = [jnp.maximum(m[g], jnp.max(s[g], axis=-1, keepdims=True)) for g in heads]
            alpha = [jnp.exp2(m[g] - m_new[g]) for g in heads]
            p = [jnp.exp2(s[g] - m_new[g]).astype(BF16) for g in heads]
            acc = [alpha[g] * acc[g]
                   + _dot(p[g], jnp.concatenate([v_ref[0, vrows, hsl[g]], ones_col], axis=1))
                   for g in heads]
            m, s = m_new, s_next
        for g in heads:
            o = acc[g][:, :FOX_HEAD] / acc[g][:, FOX_HEAD:FOX_HEAD + 1]
            o = o * lax.rsqrt(jnp.mean(o * o, axis=-1, keepdims=True) + NORM_EPS) * on_ref[g]
            o_ref[0, qrows, hsl[g]] = o.astype(o_ref.dtype)


def _fox(p_fox, ccol, crow, out_norm, bsz, seq):
    fw = p_fox.shape[-1] // 3
    nh = fw // FOX_HEAD
    hps = FOX_HEADS_PER_STEP
    ng = nh // hps
    wid = hps * FOX_HEAD
    tq = _pick(seq, (512, 256, 128))
    p3 = p_fox.reshape(bsz, seq, 3 * fw)
    crow4 = crow.reshape(bsz, LANES, 1, seq)
    on = out_norm.reshape(nh, 1, FOX_HEAD)
    return pl.pallas_call(
        functools.partial(_fox_kernel, tq=tq, seq=seq, scale=FOX_HEAD ** -0.5),
        grid=(bsz, ng),
        in_specs=[pl.BlockSpec((1, seq, wid), lambda b, h: (b, 0, h)),
                  pl.BlockSpec((1, seq, wid), lambda b, h: (b, 0, ng + h)),
                  pl.BlockSpec((1, seq, wid), lambda b, h: (b, 0, 2 * ng + h)),
                  pl.BlockSpec((1, seq, LANES), lambda b, h: (b, 0, 0)),
                  pl.BlockSpec((1, hps, 1, seq), lambda b, h: (b, h, 0, 0)),
                  pl.BlockSpec((hps, 1, FOX_HEAD), lambda b, h: (h, 0, 0))],
        out_specs=pl.BlockSpec((1, seq, wid), lambda b, h: (b, 0, h)),
        out_shape=jax.ShapeDtypeStruct((bsz, seq, fw), BF16),
        compiler_params=_params("parallel", "parallel"),
        name="fox_attention",
    )(p3, p3, p3, ccol, crow4, on)


def _pad_rows(w, rows):
    return jnp.pad(w, ((0, rows - w.shape[0]), (0, 0)))


def _ffn(x2, h, w_gate, w_up, w_down, gate, seq, side=None):
    dff = w_gate.shape[1]
    dffp = _round_up(dff, 512)
    act, wd = _swiglu_up(h, w_gate, w_up, w_down, dffp)
    return _matmul_residual(act, wd, x2, gate, seq, 0.5, dff, side)


def kernel(x, c, w_mod, b_mod, norm_ffn1, ffn1_gate, ffn1_up, ffn1_down, norm_mix, w_in, rwkv_mu, rwkv_w0, rwkv_w2, rwkv_a0, rwkv_a2, rwkv_g2, rwkv_k_k, rwkv_k_a, rwkv_r_k, rwkv_ln_w, rwkv_ln_b, fox_b_f, fox_out_norm, w_out, norm_ffn2, ffn2_gate, ffn2_up, ffn2_down, norm_final):
    bsz, seq, d = x.shape
    depth = w_mod.shape[0]
    rw = rwkv_w0.shape[-1]
    fw = fox_out_norm.shape[-1]
    nfh = fox_b_f.shape[-1]
    dl, al, gl = rwkv_w2.shape[1], rwkv_a2.shape[1], rwkv_g2.shape[1]
    dlp, alp, glp = (_round_up(n, LANES) for n in (dl, al, gl))
    c3 = 3 * rw
    c4, c5, c6 = c3 + dl, c3 + dl + al, c3 + dl + al + gl

    x2 = x.reshape(bsz * seq, d)
    for l in range(depth):
        mod = _mod(c, w_mod[l], b_mod[l])
        sh1, sc1, gt1, sh2, sc2, gt2, sh3, sc3, gt3 = (
            m[:, None, :] for m in jnp.split(mod, 9, axis=-1))

        h = _norm_mod(x2, norm_ffn1[l], sc1, sh1, seq)
        x2, wt = _ffn(x2, h, ffn1_gate[l], ffn1_up[l], ffn1_down[l], gt1, seq,
                      side=jnp.swapaxes(w_in[l], 0, 1))

        h = _norm_mod(x2, norm_mix[l], sc2, sh2, seq)
        mu = rwkv_mu[l]
        wt_sm = jnp.concatenate(
            [_pad_rows(wt[c3:c4], dlp), _pad_rows(wt[c4:c5], alp),
             _pad_rows(wt[c5:c6], glp), _pad_rows(wt[c6 + 3 * fw:], LANES)], axis=0)
        mu_rkv = mu[:c3].reshape(1, c3)
        mu_sm = jnp.concatenate(
            [jnp.pad(mu[c3:c4], (0, dlp - dl)), jnp.pad(mu[c4:c5], (0, alp - al)),
             jnp.pad(mu[c5:c6], (0, glp - gl))]).reshape(1, dlp + alp + glp)
        p_rkv = _matmul_nt(h, wt, F32, 0, c3)
        p_sm = _matmul_nt(h, wt_sm, F32)
        p_fox = _matmul_nt(h, wt, BF16, c6, 3 * fw)

        y_r = _rwkv(p_rkv, p_sm, mu_rkv, mu_sm,
                    rwkv_w0[l], rwkv_a0[l], rwkv_k_k[l], rwkv_k_a[l], rwkv_r_k[l],
                    rwkv_ln_w[l], rwkv_ln_b[l],
                    _pad_rows(rwkv_w2[l], dlp).astype(BF16),
                    _pad_rows(rwkv_a2[l], alp).astype(BF16),
                    _pad_rows(rwkv_g2[l], glp).astype(BF16), bsz, seq)

        b_f_pad = jnp.pad(fox_b_f[l], (0, LANES - nfh)).reshape(1, LANES)
        ccol, crow = _fgate(p_sm, b_f_pad, dlp + alp + glp, bsz, seq)
        y_f = _fox(p_fox, ccol, crow, fox_out_norm[l], bsz, seq)

        x2 = _matmul_residual2(y_r.reshape(bsz * seq, rw), y_f.reshape(bsz * seq, fw),
                               w_out[l].astype(BF16), x2, gt2, seq, 1.0)

        h = _norm_mod(x2, norm_ffn2[l], sc3, sh3, seq)
        x2 = _ffn(x2, h, ffn2_gate[l], ffn2_up[l], ffn2_down[l], gt3, seq)
    return _final_norm(x2, norm_final).reshape(bsz, seq, d)
```

```python
import functools

import jax
import jax.numpy as jnp
from jax import lax
from jax.experimental import pallas as pl
from jax.experimental.pallas import tpu as pltpu

F32 = jnp.float32
BF16 = jnp.bfloat16

LANES = 128
RWKV_HEAD = 64
FOX_HEAD = 128
CHUNK = 64
INV_BASE = 4
GROUP = 16
NORM_EPS = 1e-6
LN_X_EPS = 64e-5
VMEM_LIMIT = 56 * 1024 * 1024


def _round_up(n, m):
    return (n + m - 1) // m * m


def _pick(n, candidates):
    for c in candidates:
        if n % c == 0:
            return c
    return n


def _dot(a, b):
    return jnp.dot(a, b, preferred_element_type=F32)


def _dot_nt(a, b):
    return lax.dot_general(a, b, (((1,), (1,)), ((), ())), preferred_element_type=F32)


def _dot_tn(a, b):
    return lax.dot_general(a, b, (((0,), (0,)), ((), ())), preferred_element_type=F32)


def _params(*sem):
    return pltpu.CompilerParams(dimension_semantics=sem, vmem_limit_bytes=VMEM_LIMIT)


def _mod_kernel(c_ref, w_ref, b_ref, o_ref):
    c = c_ref[...]
    a = (c * jax.nn.sigmoid(c)).astype(BF16)
    o_ref[...] = _dot(a, w_ref[...].astype(BF16)) + b_ref[...]


def _mod(c, w_mod, b_mod):
    bsz, d = c.shape
    n = w_mod.shape[1]
    rows = _round_up(bsz, 16)
    c_pad = jnp.pad(c, ((0, rows - bsz), (0, 0)))
    tn = _pick(n, (1024, 512, 256, 128))
    out = pl.pallas_call(
        _mod_kernel,
        grid=(n // tn,),
        in_specs=[pl.BlockSpec((rows, d), lambda j: (0, 0)),
                  pl.BlockSpec((d, tn), lambda j: (0, j)),
                  pl.BlockSpec((1, tn), lambda j: (0, j))],
        out_specs=pl.BlockSpec((rows, tn), lambda j: (0, j)),
        out_shape=jax.ShapeDtypeStruct((rows, n), F32),
        compiler_params=_params("parallel"),
        name="mod_matmul",
    )(c_pad, w_mod, b_mod.reshape(1, n))
    return out[:bsz]


def _norm_kernel(x_ref, g_ref, sc_ref, sh_ref, o_ref):
    x = x_ref[...]
    y = x * lax.rsqrt(jnp.mean(x * x, axis=-1, keepdims=True) + NORM_EPS) * g_ref[...]
    o_ref[...] = (y * (1.0 + sc_ref[0]) + sh_ref[0]).astype(o_ref.dtype)


def _final_norm_kernel(x_ref, g_ref, o_ref):
    x = x_ref[...]
    o_ref[...] = x * lax.rsqrt(jnp.mean(x * x, axis=-1, keepdims=True) + NORM_EPS) * g_ref[...]


def _norm_mod(x2, g, sc, sh, seq):
    t, d = x2.shape
    ts = _pick(seq, (512, 256, 128, 64, 8))
    return pl.pallas_call(
        _norm_kernel,
        grid=(t // ts,),
        in_specs=[pl.BlockSpec((ts, d), lambda i: (i, 0)),
                  pl.BlockSpec((1, d), lambda i: (0, 0)),
                  pl.BlockSpec((1, 1, d), lambda i: (i * ts // seq, 0, 0)),
                  pl.BlockSpec((1, 1, d), lambda i: (i * ts // seq, 0, 0))],
        out_specs=pl.BlockSpec((ts, d), lambda i: (i, 0)),
        out_shape=jax.ShapeDtypeStruct((t, d), BF16),
        compiler_params=_params("parallel"),
        name="norm_mod",
    )(x2, g.reshape(1, d), sc, sh)


def _final_norm(x2, g):
    t, d = x2.shape
    ts = _pick(t, (512, 256, 128, 64, 8))
    return pl.pallas_call(
        _final_norm_kernel,
        grid=(t // ts,),
        in_specs=[pl.BlockSpec((ts, d), lambda i: (i, 0)),
                  pl.BlockSpec((1, d), lambda i: (0, 0))],
        out_specs=pl.BlockSpec((ts, d), lambda i: (i, 0)),
        out_shape=jax.ShapeDtypeStruct((t, d), F32),
        compiler_params=_params("parallel"),
        name="final_norm",
    )(x2, g.reshape(1, d))


def _mm_nt_kernel(x_ref, wt_ref, o_ref):
    o_ref[...] = _dot_nt(x_ref[...], wt_ref[...]).astype(o_ref.dtype)


def _matmul_nt(x, wt, out_dtype, row0=0, n=None):
    m, kd = x.shape
    n = wt.shape[0] if n is None else n
    tm = _pick(m, (1024, 512, 256, 128))
    tn = _pick(n, (1024, 512, 640, 256, 128))
    return pl.pallas_call(
        _mm_nt_kernel,
        grid=(m // tm, n // tn),
        in_specs=[pl.BlockSpec((tm, kd), lambda i, j: (i, 0)),
                  pl.BlockSpec((pl.Element(tn), pl.Element(kd)),
                               lambda i, j: ((row0 // 16 + j * (tn // 16)) * 16, 0))],
        out_specs=pl.BlockSpec((tm, tn), lambda i, j: (i, j)),
        out_shape=jax.ShapeDtypeStruct((m, n), out_dtype),
        compiler_params=_params("parallel", "parallel"),
        name="matmul_nt",
    )(x, wt)


def _swiglu_kernel(x_ref, wg_hbm, wu_hbm, wd_ref, o_ref, wdb_ref, stage_ref, wgb_ref, wub_ref,
                   sem, *, n_full, rem, tn, wd_blocks):
    j = pl.program_id(0)
    step = j * pl.num_programs(1) + pl.program_id(1)

    def full_copies(jj):
        cols = pl.ds(pl.multiple_of(jj * tn, tn), tn)
        return (pltpu.make_async_copy(wg_hbm.at[:, cols], stage_ref.at[0], sem.at[0]),
                pltpu.make_async_copy(wu_hbm.at[:, cols], stage_ref.at[1], sem.at[1]))

    def tail_copies():
        cols = pl.ds(n_full * tn, rem)
        return (pltpu.make_async_copy(wg_hbm.at[:, cols], stage_ref.at[0, :, :rem], sem.at[0]),
                pltpu.make_async_copy(wu_hbm.at[:, cols], stage_ref.at[1, :, :rem], sem.at[1]))

    def start(jj):
        @pl.when(jj < n_full)
        def _():
            for c in full_copies(jj):
                c.start()

        if rem:
            @pl.when(jj == n_full)
            def _():
                for c in tail_copies():
                    c.start()

    @pl.when(pl.program_id(1) == 0)
    def _():
        @pl.when(j == 0)
        def _():
            start(j)

        @pl.when(j < n_full)
        def _():
            for c in full_copies(j):
                c.wait()
            wgb_ref[...] = stage_ref[0].astype(BF16)
            wub_ref[...] = stage_ref[1].astype(BF16)

        if rem:
            @pl.when(j == n_full)
            def _():
                for c in tail_copies():
                    c.wait()
                wgb_ref[:, :rem] = stage_ref[0, :, :rem].astype(BF16)
                wub_ref[:, :rem] = stage_ref[1, :, :rem].astype(BF16)

        start(j + 1)

    def compute(ncols):
        wdb_ref[...] = jnp.where(step < wd_blocks, wd_ref[...], 0.0).astype(wdb_ref.dtype)
        x = x_ref[...]
        g = _dot(x, wgb_ref[:, :ncols])
        u = _dot(x, wub_ref[:, :ncols])
        o_ref[:, :ncols] = (g * (0.5 * jnp.tanh(0.5 * g) + 0.5) * u).astype(o_ref.dtype)
        if ncols < tn:
            o_ref[:, ncols:] = jnp.zeros((o_ref.shape[0], tn - ncols), o_ref.dtype)

    if rem:
        pl.when(j < n_full)(lambda: compute(tn))
        pl.when(j == n_full)(lambda: compute(rem))
    else:
        compute(tn)


def _swiglu_up(h, wg, wu, w_down, n_out):
    m, kd = h.shape
    n = wg.shape[1]
    d_out = w_down.shape[1]
    tm = _pick(m, (1024, 512, 256, 128))
    tn = 512
    n_full, rem = divmod(n, tn)
    assert rem % LANES == 0 and n_out == _round_up(n, tn)
    nj, ni = n_out // tn, m // tm
    slab, left = divmod(n_out, nj * ni)
    assert left == 0 and slab % 16 == 0 and n % slab == 0
    wd_blocks = n // slab
    return pl.pallas_call(
        functools.partial(_swiglu_kernel, n_full=n_full, rem=rem, tn=tn, wd_blocks=wd_blocks),
        grid=(nj, ni),
        in_specs=[pl.BlockSpec((tm, kd), lambda j, i: (i, 0)),
                  pl.BlockSpec(memory_space=pl.ANY),
                  pl.BlockSpec(memory_space=pl.ANY),
                  pl.BlockSpec((slab, d_out),
                               lambda j, i: (jnp.minimum(j * ni + i, wd_blocks - 1), 0))],
        out_specs=[pl.BlockSpec((tm, tn), lambda j, i: (i, j)),
                   pl.BlockSpec((slab, d_out), lambda j, i: (j * ni + i, 0))],
        out_shape=[jax.ShapeDtypeStruct((m, n_out), BF16),
                   jax.ShapeDtypeStruct((n_out, d_out), BF16)],
        scratch_shapes=[pltpu.VMEM((2, kd, tn), F32),
                        pltpu.VMEM((kd, tn), BF16), pltpu.VMEM((kd, tn), BF16),
                        pltpu.SemaphoreType.DMA((2,))],
        compiler_params=_params("arbitrary", "arbitrary"),
        name="swiglu_up",
    )(h, wg, wu, w_down)


def _resid_kernel(a_ref, w_ref, x_ref, gt_ref, *rest, nk, k_last, scale):
    side_in, o_ref, side_out, acc_ref = rest if len(rest) == 4 else (None, rest[0], None, rest[1])
    k = pl.program_id(2)

    def side_cast():
        if side_in is not None:
            side_out[...] = side_in[...].astype(side_out.dtype)

    @pl.when(k == 0)
    def _():
        side_cast()
        acc_ref[...] = _dot(a_ref[...], w_ref[...])

    @pl.when((k > 0) & (k < nk - 1))
    def _():
        side_cast()
        acc_ref[...] += _dot(a_ref[...], w_ref[...])

    @pl.when(k == nk - 1)
    def _():
        side_cast()
        y = acc_ref[...] + _dot(a_ref[:, :k_last], w_ref[:k_last, :])
        o_ref[...] = x_ref[...] + (scale * gt_ref[0]) * y


def _matmul_residual(a, w, x2, gate, seq, scale, k_valid, side=None):
    m, kd = a.shape
    n = w.shape[1]
    tm = _pick(seq, (1024, 512, 256, 128))
    tn = _pick(n, (1024, 512, 256, 128))
    tk = _pick(kd, (2816, 1024, 512, 256, 128))
    nk = kd // tk
    k_last = k_valid - (nk - 1) * tk
    assert nk >= 2 and 0 < k_last <= tk and k_last % LANES == 0
    ni, nj = m // tm, n // tn
    in_specs = [pl.BlockSpec((tm, tk), lambda i, j, k: (i, k)),
                pl.BlockSpec((tk, tn), lambda i, j, k: (k, j)),
                pl.BlockSpec((tm, tn), lambda i, j, k: (i, j)),
                pl.BlockSpec((1, 1, tn), lambda i, j, k: (i * tm // seq, 0, j))]
    out_specs = [pl.BlockSpec((tm, tn), lambda i, j, k: (i, j))]
    out_shape = [jax.ShapeDtypeStruct((m, n), F32)]
    operands = [a, w, x2, gate]
    semantics = ("parallel", "parallel", "arbitrary")
    if side is not None:
        rows, cols = side.shape
        slab = _round_up(-(-rows // (ni * nj * nk)), 16)
        last = -(-rows // slab) - 1

        def side_map(i, j, k):
            return (jnp.minimum((i * nj + j) * nk + k, last), 0)

        in_specs.append(pl.BlockSpec((slab, cols), side_map))
        out_specs.append(pl.BlockSpec((slab, cols), side_map))
        out_shape.append(jax.ShapeDtypeStruct((rows, cols), BF16))
        operands.append(side)
        semantics = ("arbitrary", "arbitrary", "arbitrary")
    res = pl.pallas_call(
        functools.partial(_resid_kernel, nk=nk, k_last=k_last, scale=scale),
        grid=(ni, nj, nk),
        in_specs=in_specs,
        out_specs=out_specs,
        out_shape=out_shape,
        scratch_shapes=[pltpu.VMEM((tm, tn), F32)],
        compiler_params=_params(*semantics),
        name="matmul_residual",
    )(*operands)
    return res[0] if side is None else res


def _resid2_kernel(a1_ref, a2_ref, w_ref, x_ref, gt_ref, o_ref, *, scale):
    k1 = a1_ref.shape[1]
    y = _dot(a1_ref[...], w_ref[:k1, :]) + _dot(a2_ref[...], w_ref[k1:, :])
    o_ref[...] = x_ref[...] + (scale * gt_ref[0]) * y


def _matmul_residual2(a1, a2, w, x2, gate, seq, scale):
    m, k1 = a1.shape
    k2 = a2.shape[1]
    n = w.shape[1]
    assert w.shape[0] == k1 + k2
    tm = _pick(seq, (1024, 512, 256, 128))
    tn = _pick(n, (512, 256, 128))
    return pl.pallas_call(
        functools.partial(_resid2_kernel, scale=scale),
        grid=(m // tm, n // tn),
        in_specs=[pl.BlockSpec((tm, k1), lambda i, j: (i, 0)),
                  pl.BlockSpec((tm, k2), lambda i, j: (i, 0)),
                  pl.BlockSpec((k1 + k2, tn), lambda i, j: (0, j)),
                  pl.BlockSpec((tm, tn), lambda i, j: (i, j)),
                  pl.BlockSpec((1, 1, tn), lambda i, j: (i * tm // seq, 0, j))],
        out_specs=pl.BlockSpec((tm, tn), lambda i, j: (i, j)),
        out_shape=jax.ShapeDtypeStruct((m, n), F32),
        compiler_params=_params("parallel", "parallel"),
        name="out_proj_residual",
    )(a1, a2, w, x2, gate)


ROW_TILE = 512


def _lane_head0(shape):
    return lax.broadcasted_iota(jnp.int32, shape, 1) < RWKV_HEAD


def _stack(x):
    m0 = _lane_head0(x.shape)
    zero = jnp.zeros_like(x)
    return jnp.concatenate([jnp.where(m0, x, zero), jnp.where(m0, zero, x)], axis=0)


def _split_dot(lhs_bf16, x):
    hi = x.astype(BF16)
    lo = (x - hi.astype(F32)).astype(BF16)
    return _dot(lhs_bf16, hi) + _dot(lhs_bf16, lo)


def _lora_in_kernel(ps_ref, prev_ref, mu_ref, o_ref, *, lora_w, lora_a):
    x = ps_ref[0]
    prev_row = jnp.where(pl.program_id(1) == 0, 0.0, prev_ref[0][7:8, :])
    row = lax.broadcasted_iota(jnp.int32, x.shape, 0)
    prev = jnp.where(row == 0, prev_row, pltpu.roll(x, 1, 0))
    sm = x + (prev - x) * mu_ref[...]
    o_ref[0, :, :lora_w] = jnp.tanh(sm[:, :lora_w]).astype(o_ref.dtype)
    o_ref[0, :, lora_w:lora_w + lora_a] = sm[:, lora_w:lora_w + lora_a].astype(o_ref.dtype)
    o_ref[0, :, lora_w + lora_a:] = jax.nn.sigmoid(sm[:, lora_w + lora_a:]).astype(o_ref.dtype)


def _lora_in(p_sm, mu_sm, lora_w, lora_a, bsz, seq):
    wsm = mu_sm.shape[-1]
    ps = p_sm.reshape(bsz, seq, p_sm.shape[-1])
    ts = _pick(seq, (ROW_TILE, 128, 64, 8))
    return pl.pallas_call(
        functools.partial(_lora_in_kernel, lora_w=lora_w, lora_a=lora_a),
        grid=(bsz, seq // ts),
        in_specs=[pl.BlockSpec((1, ts, wsm), lambda b, i: (b, i, 0)),
                  pl.BlockSpec((1, 8, wsm), lambda b, i: (b, jnp.maximum(i * (ts // 8) - 1, 0), 0)),
                  pl.BlockSpec((1, wsm), lambda b, i: (0, 0))],
        out_specs=pl.BlockSpec((1, ts, wsm), lambda b, i: (b, i, 0)),
        out_shape=jax.ShapeDtypeStruct((bsz, seq, wsm), BF16),
        compiler_params=_params("parallel", "parallel"),
        name="rwkv_lora_in",
    )(ps, ps, mu_sm)


(VEC_MU_R, VEC_MU_K, VEC_MU_V, VEC_W0, VEC_A0, VEC_KK, VEC_KA, VEC_RK, VEC_LNW,
 VEC_LNB) = range(10)
VEC_ROWS = 16


def _rwkv_kernel(pr_ref, pk_ref, pv_ref, ps_ref, vec_ref, lora_ref, o_ref,
                 at_s, rt_s, kt_s, bt_s, kg_s, bg_s, v_s, g_s, bonus_s, gc_s,
                 m_s, n_s, q_s, yb_s, sb_s, *, seq, lora_w, lora_a):
    n_chunks = seq // CHUNK
    cpt = ROW_TILE // CHUNK

    def vec(idx):
        return vec_ref[idx:idx + 1, :]

    ri = lax.broadcasted_iota(jnp.int32, (LANES, LANES), 0)
    ci = lax.broadcasted_iota(jnp.int32, (LANES, LANES), 1)
    bd_mask = (ri // RWKV_HEAD) == (ci // RWKV_HEAD)
    bd_ones = bd_mask.astype(BF16)
    tri = ((ci <= ri) & ((ri // CHUNK) == (ci // CHUNK))).astype(BF16)

    def head_sum(x):
        return _dot(x.astype(BF16), bd_ones)

    def shift_mix(ref, t0, mu):
        x = ref[0, pl.ds(t0, ROW_TILE), :].astype(F32)
        if t0 == 0:
            prev_row = jnp.zeros((1, x.shape[1]), F32)
        else:
            prev_row = ref[0, pl.ds(t0 - 16, 16), :].astype(F32)[15:16, :]
        row = lax.broadcasted_iota(jnp.int32, x.shape, 0)
        prev = jnp.where(row == 0, prev_row, pltpu.roll(x, 1, 0))
        return x + (prev - x) * mu

    def phase_a_stages(i):
        t0 = i * ROW_TILE
        rows = pl.ds(t0, ROW_TILE)
        st = {}

        def load_and_lora():
            st["r"] = shift_mix(pr_ref, t0, vec(VEC_MU_R))
            st["k"] = shift_mix(pk_ref, t0, vec(VEC_MU_K))
            st["v"] = shift_mix(pv_ref, t0, vec(VEC_MU_V))
            st["wlin"] = vec(VEC_W0) + _dot(ps_ref[0, rows, :lora_w], lora_ref[:lora_w, :])
            st["alin"] = vec(VEC_A0) + _dot(ps_ref[0, rows, lora_w:lora_w + lora_a],
                                            lora_ref[lora_w:lora_w + lora_a, :])
            st["g"] = _dot(ps_ref[0, rows, lora_w + lora_a:], lora_ref[lora_w + lora_a:, :])

        def decay_and_norm():
            wlin = st.pop("wlin")
            wlog = -(jnp.maximum(-wlin, 0.0) + jnp.log(1.0 + jnp.exp(-jnp.abs(wlin)))) - 0.5
            st["ld"] = -jnp.exp(wlog)
            st["alr"] = jax.nn.sigmoid(st.pop("alin"))
            st["kk"] = st["k"] * vec(VEC_KK)
            st["kk_ss"] = head_sum(st["kk"] * st["kk"])
            st["cum"] = jnp.concatenate(
                [_split_dot(tri, st["ld"][r:r + LANES]) for r in range(0, ROW_TILE, LANES)], axis=0)

        def keys_and_bonus():
            st["kk"] = st["kk"] * lax.rsqrt(jnp.maximum(st.pop("kk_ss"), 1e-24))
            st["k"] = st["k"] * (1.0 + (st["alr"] - 1.0) * vec(VEC_KA))
            st["bonus_sum"] = head_sum(st["r"] * st["k"] * vec(VEC_RK))
            st["b"] = st["kk"] * st["alr"]

        def scale_and_store():
            cum, ld = st["cum"], st.pop("ld")
            g_dec = jnp.exp(cum)
            g_inv = jnp.exp(-cum)
            at_s[rows, :] = (-st.pop("kk") * jnp.exp(cum - ld)).astype(BF16)
            rt_s[rows, :] = (st.pop("r") * g_dec).astype(BF16)
            kt_s[rows, :] = (st["k"] * g_inv).astype(BF16)
            bt_s[rows, :] = (st["b"] * g_inv).astype(BF16)
            v_s[rows, :] = st["v"].astype(BF16)
            g_s[rows, :] = st.pop("g")
            bonus_s[rows, :] = st.pop("bonus_sum") * st.pop("v")

        def chunk_tails():
            cum, k, b = st.pop("cum"), st.pop("k"), st.pop("b")
            for c in range(cpt):
                lo, hi = c * CHUNK, (c + 1) * CHUNK
                cum_c = cum[hi - 1:hi, :]
                rem = jnp.exp(cum_c - cum[lo:hi, :])
                kg_s[pl.ds(t0 + lo, CHUNK), :] = (k[lo:hi, :] * rem).astype(BF16)
                bg_s[pl.ds(t0 + lo, CHUNK), :] = (b[lo:hi, :] * rem).astype(BF16)
                gc_s[i * cpt + c] = jnp.broadcast_to(jnp.exp(cum_c), (8, LANES))

        return [load_and_lora, decay_and_norm, keys_and_bonus, scale_and_store, chunk_tails]

    row = lax.broadcasted_iota(jnp.int32, (CHUNK, LANES), 0)
    col = lax.broadcasted_iota(jnp.int32, (CHUNK, LANES), 1) % RWKV_HEAD
    strict = col < row
    incl = col <= row
    eye_p = (col == row).astype(F32)

    def same_block(size):
        return (row // size) == (col // size)

    def each(fn, *lists):
        return [fn(*xs) for xs in zip(*lists)]

    def phase_b(i, tick):
        cs = [i * GROUP + g for g in range(GROUP)]
        sls = [pl.ds(c * CHUNK, CHUNK) for c in cs]
        at = [at_s[sl, :] for sl in sls]
        rt = [rt_s[sl, :] for sl in sls]
        kt = [kt_s[sl, :] for sl in sls]
        bt = [bt_s[sl, :] for sl in sls]
        v = [v_s[sl, :] for sl in sls]
        sc = each(lambda a, r, k, b: _dot_nt(jnp.concatenate([a, r], axis=0),
                                             jnp.concatenate([_stack(k), _stack(b)], axis=0)),
                  at, rt, kt, bt)
        a_ak = [jnp.where(strict, x[:CHUNK, :LANES], 0.0).astype(BF16) for x in sc]
        a_ab = [jnp.where(strict, x[:CHUNK, LANES:], 0.0) for x in sc]
        a_rk = [jnp.where(incl, x[CHUNK:, :LANES], 0.0).astype(BF16) for x in sc]
        a_rb = [jnp.where(incl, x[CHUNK:, LANES:], 0.0).astype(BF16) for x in sc]
        tick()
        d = [jnp.where(same_block(INV_BASE), x, 0.0) for x in a_ab]
        db = [x.astype(BF16) for x in d]
        d2 = [_dot(x, _stack(x)).astype(BF16) for x in db]
        tick()
        p = [eye_p + x for x in d]
        p = each(lambda pp, x2: pp + _dot(pp.astype(BF16), _stack(x2)), p, d2)
        tick()
        s = INV_BASE
        while s < CHUNK:
            lower_left = same_block(2 * s) & ((row // s) % 2 == 1) & ((col // s) % 2 == 0)
            a21 = [jnp.where(lower_left, x, 0.0).astype(BF16) for x in a_ab]
            pb = [x.astype(BF16) for x in p]
            a21_t11 = each(lambda a, t: _dot(a, _stack(t)).astype(BF16), a21, pb)
            tick()
            p = each(lambda pp, t, y: pp + _dot(t, _stack(y)), p, pb, a21_t11)
            tick()
            s *= 2
        tinv = [x.astype(BF16) for x in p]
        vst = [_stack(x) for x in v]
        x1 = each(lambda a, vs: _dot(a, vs).astype(BF16), a_ak, vst)
        tick()
        wu = each(lambda t, a, x: _dot(t, jnp.concatenate([_stack(a), _stack(x)], axis=1)),
                  tinv, at, x1)
        tick()
        wb = [x[:, :LANES].astype(BF16) for x in wu]
        ub = [x[:, LANES:].astype(BF16) for x in wu]
        qy = each(lambda a, w, u: _dot(a, jnp.concatenate([_stack(w), _stack(u)], axis=1)),
                  a_rb, wb, ub)
        tick()
        yb = each(lambda a, vs, x: _dot(a, vs) + x[:, LANES:], a_rk, vst, qy)
        tick()
        for g in range(GROUP):
            q_s[sls[g], :] = (rt[g].astype(F32) + qy[g][:, :LANES]).astype(BF16)
            yb_s[sls[g], :] = yb[g]
        kg = [kg_s[sl, :] for sl in sls]
        bg = [bg_s[sl, :] for sl in sls]
        mm = each(_dot_tn, wb, bg)
        tick()
        nn = each(lambda vv, u, k, b: _dot_tn(jnp.concatenate([vv, u], axis=0),
                                              jnp.concatenate([k, b], axis=0)), v, ub, kg, bg)
        tick()
        for g in range(GROUP):
            m_s[cs[g]] = jnp.where(bd_mask, mm[g], 0.0).astype(BF16)
            n_s[cs[g]] = jnp.where(bd_mask, nn[g], 0.0)

    def c1_step(c, s):
        sb = s.astype(BF16)
        sb_s[c] = sb
        return s * gc_s[c][0:1, :] + _dot(sb, m_s[c]) + n_s[c]

    n_groups = n_chunks // GROUP
    tiles_per_group = GROUP * CHUNK // ROW_TILE
    state = [jnp.zeros((LANES, LANES), F32)]

    def c1_thunk(c):
        def run():
            state[0] = c1_step(c, state[0])
        return run

    def prep_stages(group):
        per_tile = [phase_a_stages(group * tiles_per_group + t) for t in range(tiles_per_group)]
        return [stage for stages in zip(*per_tile) for stage in stages]

    def interleave(xs, ys):
        out = []
        for j in range(max(len(xs), len(ys))):
            out += xs[j:j + 1] + ys[j:j + 1]
        return out

    def run_with_fillers(body, fillers):
        def tick():
            if fillers:
                fillers.pop(0)()
        body(tick)
        while fillers:
            fillers.pop(0)()

    for stage in prep_stages(0):
        stage()
    for i in range(n_groups):
        fillers = interleave(
            [c1_thunk((i - 1) * GROUP + g) for g in range(GROUP)] if i > 0 else [],
            prep_stages(i + 1) if i + 1 < n_groups else [])
        run_with_fillers(functools.partial(phase_b, i), fillers)

    inv_n = 1.0 / RWKV_HEAD

    def phase_c2(i, tick):
        cs = [i * GROUP + g for g in range(GROUP)]
        sls = [pl.ds(c * CHUNK, CHUNK) for c in cs]

        def staged(fn, *lists):
            out = []
            for g, xs in enumerate(zip(*lists)):
                out.append(fn(*xs))
                if g % 4 == 3:
                    tick()
            return out

        y = staged(lambda sl, c: _dot_nt(q_s[sl, :], sb_s[c]) + yb_s[sl, :], sls, cs)
        mean = staged(lambda x: head_sum(x) * inv_n, y)
        yc = each(lambda a, b: a - b, y, mean)
        var = staged(lambda x: head_sum(x * x) * inv_n, yc)
        for g in range(GROUP):
            yn = yc[g] * lax.rsqrt(var[g] + LN_X_EPS) * vec(VEC_LNW) + vec(VEC_LNB)
            o_ref[0, sls[g], :] = ((yn + bonus_s[sls[g], :]) * g_s[sls[g], :]).astype(o_ref.dtype)

    tail = [c1_thunk((n_groups - 1) * GROUP + g) for g in range(GROUP)]
    for i in range(n_groups - 1):
        run_with_fillers(functools.partial(phase_c2, i), tail if i == n_groups - 2 else [])
    while tail:
        tail.pop(0)()
    phase_c2(n_groups - 1, lambda: None)


def _rwkv(p_rkv, p_sm, mu_rkv, mu_sm, w0, a0, k_k, k_a, r_k, ln_w, ln_b,
          w2p, a2p, g2p, bsz, seq):
    rw = w0.shape[-1]
    npair = rw // LANES
    lora_w, lora_a, lora_g = w2p.shape[0], a2p.shape[0], g2p.shape[0]
    wsm = lora_w + lora_a + lora_g
    p3 = p_rkv.reshape(bsz, seq, 3 * rw)
    ps = _lora_in(p_sm, mu_sm, lora_w, lora_a, bsz, seq)
    n_chunks = seq // CHUNK

    def col(off):
        return pl.BlockSpec((1, seq, LANES), lambda b, h: (b, 0, off + h))

    rows = [mu_rkv[0, :rw], mu_rkv[0, rw:2 * rw], mu_rkv[0, 2 * rw:],
            w0, a0, k_k, k_a, r_k.reshape(rw), ln_w, ln_b]
    vecs = jnp.concatenate([jnp.stack(rows), jnp.zeros((VEC_ROWS - len(rows), rw), F32)], axis=0)
    lora = jnp.concatenate([w2p, a2p, g2p], axis=0)

    return pl.pallas_call(
        functools.partial(_rwkv_kernel, seq=seq, lora_w=lora_w, lora_a=lora_a),
        grid=(bsz, npair),
        in_specs=[col(0), col(npair), col(2 * npair),
                  pl.BlockSpec((1, seq, wsm), lambda b, h: (b, 0, 0)),
                  pl.BlockSpec((VEC_ROWS, LANES), lambda b, h: (0, h)),
                  pl.BlockSpec((wsm, LANES), lambda b, h: (0, h))],
        out_specs=pl.BlockSpec((1, seq, LANES), lambda b, h: (b, 0, h)),
        out_shape=jax.ShapeDtypeStruct((bsz, seq, rw), BF16),
        scratch_shapes=(
            [pltpu.VMEM((seq, LANES), BF16) for _ in range(7)]
            + [pltpu.VMEM((seq, LANES), F32) for _ in range(2)]
            + [pltpu.VMEM((n_chunks, 8, LANES), F32),
               pltpu.VMEM((n_chunks, LANES, LANES), BF16),
               pltpu.VMEM((n_chunks, LANES, LANES), F32),
               pltpu.VMEM((seq, LANES), BF16),
               pltpu.VMEM((seq, LANES), F32),
               pltpu.VMEM((n_chunks, LANES, LANES), BF16)]),
        compiler_params=_params("parallel", "parallel"),
        name="rwkv7",
    )(p3, p3, p3, ps, vecs, lora)


def _fgate_kernel(f_ref, bf_ref, col_ref, row_ref, *, seq):
    nblk = seq // LANES
    ri = lax.broadcasted_iota(jnp.int32, (LANES, LANES), 0)
    ci = lax.broadcasted_iota(jnp.int32, (LANES, LANES), 1)
    tri = (ci <= ri).astype(BF16)

    def body(i, carry):
        t0 = pl.multiple_of(i * LANES, LANES)
        z = f_ref[0, pl.ds(t0, LANES), :] + bf_ref[...]
        logf = jnp.minimum(z, 0.0) - jnp.log1p(jnp.exp(-jnp.abs(z)))
        hi = logf.astype(BF16)
        r1 = logf - hi.astype(F32)
        mid = r1.astype(BF16)
        lo = (r1 - mid.astype(F32)).astype(BF16)
        cum = carry + (_dot(tri, hi) + _dot(tri, mid) + _dot(tri, lo))
        col_ref[0, pl.ds(t0, LANES), :] = cum
        row_ref[0, :, pl.ds(t0, LANES)] = cum.T
        return cum[LANES - 1:LANES, :]

    lax.fori_loop(0, nblk, body, jnp.zeros((1, LANES), F32))


def _fgate(p_sm, b_f_pad, off, bsz, seq):
    ps = p_sm.reshape(bsz, seq, p_sm.shape[-1])
    return pl.pallas_call(
        functools.partial(_fgate_kernel, seq=seq),
        grid=(bsz,),
        in_specs=[pl.BlockSpec((1, seq, LANES), lambda b: (b, 0, off // LANES)),
                  pl.BlockSpec((1, LANES), lambda b: (0, 0))],
        out_specs=[pl.BlockSpec((1, seq, LANES), lambda b: (b, 0, 0)),
                   pl.BlockSpec((1, LANES, seq), lambda b: (b, 0, 0))],
        out_shape=[jax.ShapeDtypeStruct((bsz, seq, LANES), F32),
                   jax.ShapeDtypeStruct((bsz, LANES, seq), F32)],
        compiler_params=_params("parallel"),
        name="fox_gates",
    )(ps, b_f_pad)


FOX_HEADS_PER_STEP = 4
LOG2E = 1.4426950408889634


def _fox_kernel(q_ref, k_ref, v_ref, ccol_ref, crow_ref, on_ref, o_ref, *, tq, seq, scale):
    hg = pl.program_id(1)
    heads = range(FOX_HEADS_PER_STEP)
    hsl = [slice(g * FOX_HEAD, (g + 1) * FOX_HEAD) for g in heads]
    lane = lax.broadcasted_iota(jnp.int32, (tq, LANES), 1)
    causal = (lax.broadcasted_iota(jnp.int32, (tq, tq), 1)
              <= lax.broadcasted_iota(jnp.int32, (tq, tq), 0))
    ones_col = jnp.where(lane == 0, 1.0, 0.0).astype(BF16)

    for i in range(seq // tq):
        qrows = pl.ds(i * tq, tq)
        q = [q_ref[0, qrows, sl] for sl in hsl]
        ccol = ccol_ref[0, qrows, :]
        cq = [LOG2E * jnp.sum(jnp.where(lane == hg * FOX_HEADS_PER_STEP + g, ccol, 0.0),
                              axis=-1, keepdims=True) for g in heads]

        def scores(j, q=q, cq=cq, i=i):
            krows = pl.ds(j * tq, tq)
            s = [_dot_nt(q[g], k_ref[0, krows, hsl[g]]) * (scale * LOG2E)
                 + cq[g] - LOG2E * crow_ref[0, g, :, krows] for g in heads]
            if j == i:
                s = [jnp.where(causal, x, -jnp.inf) for x in s]
            return s

        m = [jnp.full((tq, 1), -jnp.inf, F32) for _ in heads]
        acc = [jnp.zeros((tq, 2 * FOX_HEAD), F32) for _ in heads]
        s = scores(0)
        for j in range(i + 1):
            s_next = scores(j + 1) if j < i else None
            vrows = pl.ds(j * tq, tq)
            m_new = [jnp.maximum(m[g], jnp.max(s[g], axis=-1, keepdims=True)) for g in heads]
            alpha = [jnp.exp2(m[g] - m_new[g]) for g in heads]
            p = [jnp.exp2(s[g] - m_new[g]).astype(BF16) for g in heads]
            acc = [alpha[g] * acc[g]
                   + _dot(p[g], jnp.concatenate([v_ref[0, vrows, hsl[g]], ones_col], axis=1))
                   for g in heads]
            m, s = m_new, s_next
        for g in heads:
            o = acc[g][:, :FOX_HEAD] / acc[g][:, FOX_HEAD:FOX_HEAD + 1]
            o = o * lax.rsqrt(jnp.mean(o * o, axis=-1, keepdims=True) + NORM_EPS) * on_ref[g]
            o_ref[0, qrows, hsl[g]] = o.astype(o_ref.dtype)


def _fox(p_fox, ccol, crow, out_norm, bsz, seq):
    fw = p_fox.shape[-1] // 3
    nh = fw // FOX_HEAD
    hps = FOX_HEADS_PER_STEP
    ng = nh // hps
    wid = hps * FOX_HEAD
    tq = _pick(seq, (512, 256, 128))
    p3 = p_fox.reshape(bsz, seq, 3 * fw)
    crow4 = crow.reshape(bsz, LANES, 1, seq)
    on = out_norm.reshape(nh, 1, FOX_HEAD)
    return pl.pallas_call(
        functools.partial(_fox_kernel, tq=tq, seq=seq, scale=FOX_HEAD ** -0.5),
        grid=(bsz, ng),
        in_specs=[pl.BlockSpec((1, seq, wid), lambda b, h: (b, 0, h)),
                  pl.BlockSpec((1, seq, wid), lambda b, h: (b, 0, ng + h)),
                  pl.BlockSpec((1, seq, wid), lambda b, h: (b, 0, 2 * ng + h)),
                  pl.BlockSpec((1, seq, LANES), lambda b, h: (b, 0, 0)),
                  pl.BlockSpec((1, hps, 1, seq), lambda b, h: (b, h, 0, 0)),
                  pl.BlockSpec((hps, 1, FOX_HEAD), lambda b, h: (h, 0, 0))],
        out_specs=pl.BlockSpec((1, seq, wid), lambda b, h: (b, 0, h)),
        out_shape=jax.ShapeDtypeStruct((bsz, seq, fw), BF16),
        compiler_params=_params("parallel", "parallel"),
        name="fox_attention",
    )(p3, p3, p3, ccol, crow4, on)


def _pad_rows(w, rows):
    return jnp.pad(w, ((0, rows - w.shape[0]), (0, 0)))


def _ffn(x2, h, w_gate, w_up, w_down, gate, seq, side=None):
    dff = w_gate.shape[1]
    dffp = _round_up(dff, 512)
    act, wd = _swiglu_up(h, w_gate, w_up, w_down, dffp)
    return _matmul_residual(act, wd, x2, gate, seq, 0.5, dff, side)


def kernel(x, c, w_mod, b_mod, norm_ffn1, ffn1_gate, ffn1_up, ffn1_down, norm_mix, w_in, rwkv_mu, rwkv_w0, rwkv_w2, rwkv_a0, rwkv_a2, rwkv_g2, rwkv_k_k, rwkv_k_a, rwkv_r_k, rwkv_ln_w, rwkv_ln_b, fox_b_f, fox_out_norm, w_out, norm_ffn2, ffn2_gate, ffn2_up, ffn2_down, norm_final):
    bsz, seq, d = x.shape
    depth = w_mod.shape[0]
    rw = rwkv_w0.shape[-1]
    fw = fox_out_norm.shape[-1]
    nfh = fox_b_f.shape[-1]
    dl, al, gl = rwkv_w2.shape[1], rwkv_a2.shape[1], rwkv_g2.shape[1]
    dlp, alp, glp = (_round_up(n, LANES) for n in (dl, al, gl))
    c3 = 3 * rw
    c4, c5, c6 = c3 + dl, c3 + dl + al, c3 + dl + al + gl

    x2 = x.reshape(bsz * seq, d)
    for l in range(depth):
        mod = _mod(c, w_mod[l], b_mod[l])
        sh1, sc1, gt1, sh2, sc2, gt2, sh3, sc3, gt3 = (
            m[:, None, :] for m in jnp.split(mod, 9, axis=-1))

        h = _norm_mod(x2, norm_ffn1[l], sc1, sh1, seq)
        x2, wt = _ffn(x2, h, ffn1_gate[l], ffn1_up[l], ffn1_down[l], gt1, seq,
                      side=jnp.swapaxes(w_in[l], 0, 1))

        h = _norm_mod(x2, norm_mix[l], sc2, sh2, seq)
        mu = rwkv_mu[l]
        wt_sm = jnp.concatenate(
            [_pad_rows(wt[c3:c4], dlp), _pad_rows(wt[c4:c5], alp),
             _pad_rows(wt[c5:c6], glp), _pad_rows(wt[c6 + 3 * fw:], LANES)], axis=0)
        mu_rkv = mu[:c3].reshape(1, c3)
        mu_sm = jnp.concatenate(
            [jnp.pad(mu[c3:c4], (0, dlp - dl)), jnp.pad(mu[c4:c5], (0, alp - al)),
             jnp.pad(mu[c5:c6], (0, glp - gl))]).reshape(1, dlp + alp + glp)
        p_rkv = _matmul_nt(h, wt, BF16, 0, c3)
        p_sm = _matmul_nt(h, wt_sm, F32)
        p_fox = _matmul_nt(h, wt, BF16, c6, 3 * fw)

        y_r = _rwkv(p_rkv, p_sm, mu_rkv, mu_sm,
                    rwkv_w0[l], rwkv_a0[l], rwkv_k_k[l], rwkv_k_a[l], rwkv_r_k[l],
                    rwkv_ln_w[l], rwkv_ln_b[l],
                    _pad_rows(rwkv_w2[l], dlp).astype(BF16),
                    _pad_rows(rwkv_a2[l], alp).astype(BF16),
                    _pad_rows(rwkv_g2[l], glp).astype(BF16), bsz, seq)

        b_f_pad = jnp.pad(fox_b_f[l], (0, LANES - nfh)).reshape(1, LANES)
        ccol, crow = _fgate(p_sm, b_f_pad, dlp + alp + glp, bsz, seq)
        y_f = _fox(p_fox, ccol, crow, fox_out_norm[l], bsz, seq)

        x2 = _matmul_residual2(y_r.reshape(bsz * seq, rw), y_f.reshape(bsz * seq, fw),
                               w_out[l].astype(BF16), x2, gt2, seq, 1.0)

        h = _norm_mod(x2, norm_ffn2[l], sc3, sh3, seq)
        x2 = _ffn(x2, h, ffn2_gate[l], ffn2_up[l], ffn2_down[l], gt3, seq)
    return _final_norm(x2, norm_final).reshape(bsz, seq, d)
```
